```python
import math
import jax, jax.numpy as jnp
from jax import lax
import numpy as np

D_MODEL = 1024
BATCH = 16
SEQ = 2048
DEPTH = 1

HG_HEADS = 4
HG_KDIM = 128
HG_VDIM = 128
HG_KWIDTH = HG_HEADS * HG_KDIM
HG_WIDTH = HG_HEADS * HG_VDIM
DA_HEADS = 4
DA_HEAD_DIM = 64
DA_WIDTH = DA_HEADS * 2 * DA_HEAD_DIM
MIX_WIDTH = HG_WIDTH + DA_WIDTH
IN_SIZES = (HG_KWIDTH, HG_KWIDTH, HG_WIDTH, HG_WIDTH, DA_WIDTH, DA_WIDTH, DA_WIDTH)
IN_WIDTH = sum(IN_SIZES)
D_FF = -(-8 * D_MODEL // (3 * 256)) * 256
CHUNK = 64
Q_BLOCK = 128
EPS = 1e-6
N_MOD = 6

kernel_name = "hymba_hgrn2_diffattn_alibi_adaln_block"


def rmsnorm(x, w):
    x32 = x.astype(jnp.float32)
    y = x32 * lax.rsqrt(jnp.mean(x32 * x32, axis=-1, keepdims=True) + EPS)
    return (y * w.astype(jnp.float32)).astype(x.dtype)


def alibi_slopes(n):
    return jnp.asarray(np.array([2.0 ** (-8.0 * (h + 1) / n) for h in range(n)], dtype=np.float32))


def hgrn2(q, f_logit, i, g, lb, gnorm_w):
    B, T, _ = q.shape
    n = T // CHUNK
    f = lb + (1.0 - lb) * jax.nn.sigmoid(f_logit.astype(jnp.float32))
    logf = jnp.log(f)
    k = 1.0 - f

    def heads(t, d):
        return t.reshape(B, n, CHUNK, HG_HEADS, d).transpose(1, 0, 3, 2, 4)

    qc = heads(q.astype(jnp.float32), HG_KDIM)
    kc = heads(k, HG_KDIM)
    vc = heads(i.astype(jnp.float32), HG_VDIM)
    bc = jnp.cumsum(heads(logf, HG_KDIM), axis=3)
    causal = jnp.tril(jnp.ones((CHUNK, CHUNK), dtype=bool))[:, :, None]

    def step(S, inp):
        qb, kb, vb, bb = inp
        o_inter = jnp.einsum('bhtk,bhkv->bhtv', qb * jnp.exp(bb), S)
        rel = bb[:, :, :, None, :] - bb[:, :, None, :, :]
        decay = jnp.exp(jnp.where(causal, rel, -jnp.inf))
        A = jnp.einsum('bhtk,bhsk,bhtsk->bhts', qb, kb, decay)
        o = o_inter + jnp.einsum('bhts,bhsv->bhtv', A, vb)
        blast = bb[:, :, -1:, :]
        S_new = jnp.exp(blast[:, :, 0, :])[..., None] * S + jnp.einsum('bhsk,bhsv->bhkv', kb * jnp.exp(blast - bb), vb)
        return S_new, o

    S0 = jnp.zeros((B, HG_HEADS, HG_KDIM, HG_VDIM), jnp.float32)
    _, o = lax.scan(step, S0, (qc, kc, vc, bc))
    o = o.transpose(1, 0, 3, 2, 4).reshape(B, T, HG_HEADS, HG_VDIM)
    o = rmsnorm(o, gnorm_w) * jax.nn.silu(g.reshape(B, T, HG_HEADS, HG_VDIM).astype(jnp.float32))
    return o.reshape(B, T, HG_WIDTH).astype(q.dtype)


def diff_attention(q, k, v, lam, lambda_init, subln_w):
    B, T, _ = q.shape
    d = DA_HEAD_DIM
    q = q.reshape(B, T, DA_HEADS, 2, d).transpose(0, 2, 3, 1, 4)
    k = k.reshape(B, T, DA_HEADS, 2, d).transpose(0, 2, 3, 1, 4)
    v = v.reshape(B, T, DA_HEADS, 2 * d).transpose(0, 2, 1, 3)
    scale = d ** -0.5
    slopes = alibi_slopes(DA_HEADS)
    pos = jnp.arange(T)
    outs = []
    for blk in range(T // Q_BLOCK):
        lo, hi = blk * Q_BLOCK, (blk + 1) * Q_BLOCK
        qb, kb, vb = q[:, :, :, lo:hi], k[:, :, :, :hi], v[:, :, :hi]
        s = jnp.einsum('bhmqd,bhmkd->bhmqk', qb, kb).astype(jnp.float32) * scale
        dist = (pos[lo:hi, None] - pos[None, :hi]).astype(jnp.float32)
        s = s - (slopes[:, None, None] * dist)[None, :, None]
        s = jnp.where(dist >= 0, s, -jnp.inf)
        p = jax.nn.softmax(s, axis=-1)
        w = p[:, :, 0] - lam * p[:, :, 1]
        outs.append(jnp.einsum('bhqk,bhkv->bhqv', w.astype(v.dtype), vb))
    o = jnp.concatenate(outs, axis=2)
    o = rmsnorm(o, subln_w) * (1.0 - lambda_init)
    return o.transpose(0, 2, 1, 3).reshape(B, T, DA_WIDTH)


def setup_inputs(seed: int = 0) -> dict:
    key = jax.random.key(seed)
    ks = jax.random.split(key, 20)
    f32 = jnp.float32
    nrm = lambda k, shape, s: jax.random.normal(k, shape, f32) * s
    return {
        "x": nrm(ks[0], (BATCH, SEQ, D_MODEL), 1.0),
        "c": nrm(ks[1], (BATCH, D_MODEL), 1.0),
        "w_ada": nrm(ks[2], (DEPTH, D_MODEL, N_MOD * D_MODEL), 0.5 * D_MODEL ** -0.5),
        "b_ada": nrm(ks[3], (DEPTH, N_MOD * D_MODEL), 0.01),
        "norm1_w": 1.0 + nrm(ks[4], (DEPTH, D_MODEL), 0.02),
        "w_in": nrm(ks[5], (DEPTH, D_MODEL, IN_WIDTH), D_MODEL ** -0.5),
        "hgrn_lb_logits": nrm(ks[6], (DEPTH + 1, HG_KWIDTH), 0.5),
        "hgrn_gnorm_w": 1.0 + nrm(ks[7], (DEPTH, HG_VDIM), 0.02),
        "diff_lambda_q1": nrm(ks[8], (DEPTH, DA_HEAD_DIM), 0.1),
        "diff_lambda_k1": nrm(ks[9], (DEPTH, DA_HEAD_DIM), 0.1),
        "diff_lambda_q2": nrm(ks[10], (DEPTH, DA_HEAD_DIM), 0.1),
        "diff_lambda_k2": nrm(ks[11], (DEPTH, DA_HEAD_DIM), 0.1),
        "diff_subln_w": 1.0 + nrm(ks[12], (DEPTH, 2 * DA_HEAD_DIM), 0.02),
        "w_out": nrm(ks[13], (DEPTH, MIX_WIDTH, D_MODEL), MIX_WIDTH ** -0.5),
        "norm2_w": 1.0 + nrm(ks[14], (DEPTH, D_MODEL), 0.02),
        "w_ffn_gate": nrm(ks[15], (DEPTH, D_MODEL, D_FF), D_MODEL ** -0.5),
        "w_ffn_up": nrm(ks[16], (DEPTH, D_MODEL, D_FF), D_MODEL ** -0.5),
        "w_ffn_down": nrm(ks[17], (DEPTH, D_FF, D_MODEL), D_FF ** -0.5),
        "final_norm_w": 1.0 + nrm(ks[18], (D_MODEL,), 0.02),
    }


def reference(x, c, w_ada, b_ada, norm1_w, w_in, hgrn_lb_logits, hgrn_gnorm_w,
              diff_lambda_q1, diff_lambda_k1, diff_lambda_q2, diff_lambda_k2, diff_subln_w,
              w_out, norm2_w, w_ffn_gate, w_ffn_up, w_ffn_down, final_norm_w):
    offs = np.cumsum(IN_SIZES)[:-1].tolist()
    lower_bounds = jnp.cumsum(jax.nn.softmax(hgrn_lb_logits.astype(jnp.float32), axis=0), axis=0)
    h = x
    for l in range(DEPTH):
        mod = jax.nn.silu(c) @ w_ada[l] + b_ada[l]
        sh1, sc1, g1, sh2, sc2, g2 = jnp.split(mod[:, None, :], N_MOD, axis=-1)
        u = rmsnorm(h, norm1_w[l]) * (1.0 + sc1) + sh1
        proj = u @ w_in[l]
        hq, hf, hi, hg, aq, ak, av = jnp.split(proj, offs, axis=-1)
        o_hg = hgrn2(hq, hf, hi, hg, lower_bounds[l], hgrn_gnorm_w[l])
        lambda_init = 0.8 - 0.6 * math.exp(-0.3 * l)
        lam = (jnp.exp(jnp.sum(diff_lambda_q1[l].astype(jnp.float32) * diff_lambda_k1[l].astype(jnp.float32)))
               - jnp.exp(jnp.sum(diff_lambda_q2[l].astype(jnp.float32) * diff_lambda_k2[l].astype(jnp.float32)))
               + lambda_init)
        o_da = diff_attention(aq, ak, av, lam, lambda_init, diff_subln_w[l])
        mix = jnp.concatenate([o_hg, o_da.astype(o_hg.dtype)], axis=-1) @ w_out[l]
        h = h + g1 * mix
        u = rmsnorm(h, norm2_w[l]) * (1.0 + sc2) + sh2
        ff = (jax.nn.silu(u @ w_ffn_gate[l]) * (u @ w_ffn_up[l])) @ w_ffn_down[l]
        h = h + g2 * ff
    return rmsnorm(h, final_norm_w)
```

```python
import functools
import math

import jax
import jax.numpy as jnp
from jax import lax
from jax.experimental import pallas as pl
from jax.experimental.pallas import tpu as pltpu

F32 = jnp.float32
BF16 = jnp.bfloat16

HG_HEADS = 4
HG_DIM = 128
DA_HEADS = 4
DA_HEAD_DIM = 64
N_MOD = 6
EPS = 1e-6
LAYER = 0
LAMBDA_INIT = 0.8 - 0.6 * math.exp(-0.3 * LAYER)

LANES = 128
SUBLANES = 8
VMEM_LIMIT_BYTES = 56 * 1024 * 1024

HGRN_CHUNK = 64
HGRN_SUB = 16
INPROJ_TM = 512
FFN_TM = 512
FFN_FC = 256
ATT_TQ = 256
ATT_TK = 256
NEG_BIG = -1e30


def _nt_dot(a, b):
    return lax.dot_general(a, b, (((1,), (1,)), ((), ())), preferred_element_type=F32)


def _rms_scale(x):
    return lax.rsqrt(jnp.mean(x * x, axis=-1, keepdims=True) + EPS)


def _ada_kernel(c_ref, w_ref, b_ref, o_ref):
    c = c_ref[...]
    a = c * jax.nn.sigmoid(c)
    o_ref[...] = jnp.dot(a, w_ref[...], preferred_element_type=F32,
                         precision=lax.Precision.HIGHEST) + b_ref[...]


def _ada(c, w_ada, b_ada):
    bsz, d = c.shape
    n = w_ada.shape[1]
    tn = 1536
    return pl.pallas_call(
        _ada_kernel,
        grid=(n // tn,),
        in_specs=[pl.BlockSpec((bsz, d), lambda j: (0, 0)),
                  pl.BlockSpec((d, tn), lambda j: (0, j)),
                  pl.BlockSpec((1, tn), lambda j: (0, j))],
        out_specs=pl.BlockSpec((bsz, tn), lambda j: (0, j)),
        out_shape=jax.ShapeDtypeStruct((bsz, n), F32),
        compiler_params=pltpu.CompilerParams(
            dimension_semantics=("arbitrary",), vmem_limit_bytes=VMEM_LIMIT_BYTES),
        name="ada",
    )(c, w_ada, b_ada.reshape(1, n))


SLOT_HQ, SLOT_HI, SLOT_HG, SLOT_AQ, SLOT_AK, SLOT_AV = 0, 4, 8, 12, 16, 20


def _inproj_kernel(x_ref, mod_ref, n1_ref, w_ref, oa_ref, of_ref):
    x = x_ref[0]
    m = mod_ref[0]
    y = x * _rms_scale(x) * n1_ref[...]
    u = (y * (1.0 + m[1:2]) + m[0:1]).astype(BF16)
    n_blk = w_ref.shape[1] // LANES
    for j in range(n_blk // 2):
        p = jnp.dot(u, w_ref[:, j * 2 * LANES:(j + 1) * 2 * LANES], preferred_element_type=F32)
        for half in range(2):
            cb = 2 * j + half
            ph = p[:, half * LANES:(half + 1) * LANES]
            grp, head = divmod(cb, 4)
            if grp == 1:
                of_ref[0, head] = ph
            else:
                if grp == 4:
                    ph = ph * (DA_HEAD_DIM ** -0.5)
                slot = cb if grp == 0 else cb - 4
                oa_ref[0, slot] = ph.astype(BF16)


def _inproj(x, mod3, norm1_w, w_in_bf):
    bsz, t, d = x.shape
    n = w_in_bf.shape[1]
    tm = INPROJ_TM
    n_slab = n // LANES - HG_HEADS
    return pl.pallas_call(
        _inproj_kernel,
        grid=(bsz, t // tm),
        in_specs=[pl.BlockSpec((1, tm, d), lambda b, i: (b, i, 0)),
                  pl.BlockSpec((1, N_MOD, d), lambda b, i: (b, 0, 0)),
                  pl.BlockSpec((1, d), lambda b, i: (0, 0)),
                  pl.BlockSpec((d, n), lambda b, i: (0, 0), pipeline_mode=pl.Buffered(1))],
        out_specs=[pl.BlockSpec((1, n_slab, tm, LANES), lambda b, i: (b, 0, i, 0)),
                   pl.BlockSpec((1, HG_HEADS, tm, LANES), lambda b, i: (b, 0, i, 0))],
        out_shape=[jax.ShapeDtypeStruct((bsz, n_slab, t, LANES), BF16),
                   jax.ShapeDtypeStruct((bsz, HG_HEADS, t, LANES), F32)],
        compiler_params=pltpu.CompilerParams(
            dimension_semantics=("arbitrary", "arbitrary"), vmem_limit_bytes=VMEM_LIMIT_BYTES),
        name="inproj",
    )(x, mod3, norm1_w.reshape(1, d), w_in_bf)


def _hgrn_diag_block(qb, kb, bb, vb):
    half = SUBLANES
    row = lax.broadcasted_iota(jnp.int32, (half, LANES), 0)
    q_lo, q_hi = qb[:half], qb[half:]
    b_lo, b_hi = bb[:half], bb[half:]
    o_lo = jnp.zeros((half, LANES), F32)
    o_hi = jnp.zeros((half, LANES), F32)
    for s in range(HGRN_SUB):
        ks, bs, vs = kb[s:s + 1], bb[s:s + 1], vb[s:s + 1]
        if s < half:
            rel = jnp.where(row >= s, b_lo - bs, NEG_BIG)
            col = jnp.sum(q_lo * ks * jnp.exp(rel), axis=-1, keepdims=True)
            o_lo = o_lo + col * vs
            col = jnp.sum(q_hi * ks * jnp.exp(b_hi - bs), axis=-1, keepdims=True)
            o_hi = o_hi + col * vs
        else:
            rel = jnp.where(row >= s - half, b_hi - bs, NEG_BIG)
            col = jnp.sum(q_hi * ks * jnp.exp(rel), axis=-1, keepdims=True)
            o_hi = o_hi + col * vs
    return jnp.concatenate([o_lo, o_hi], axis=0)


def _hgrn_kernel(lbl_ref, gw_ref, q_ref, f_ref, i_ref, g_ref, o_ref, st_ref):
    c_len, sub = HGRN_CHUNK, HGRN_SUB
    t_len = q_ref.shape[2]
    l = lbl_ref[...]
    e = jnp.exp(l - jnp.max(l, axis=0, keepdims=True))
    lb = jnp.sum(e[:LAYER + 1], axis=0, keepdims=True) / jnp.sum(e, axis=0, keepdims=True)
    gw = gw_ref[...]
    r_i = lax.broadcasted_iota(jnp.int32, (c_len, c_len), 0)
    c_i = lax.broadcasted_iota(jnp.int32, (c_len, c_len), 1)
    tri = (r_i >= c_i).astype(BF16)
    st_ref[...] = jnp.zeros_like(st_ref)

    def chunk(c, carry):
        r0 = pl.multiple_of(c * c_len, c_len)
        q = q_ref[0, 0, pl.ds(r0, c_len), :].astype(F32)
        v_bf = i_ref[0, 0, pl.ds(r0, c_len), :]
        v = v_bf.astype(F32)
        g = g_ref[0, 0, pl.ds(r0, c_len), :].astype(F32)
        f = lb + (1.0 - lb) * jax.nn.sigmoid(f_ref[0, 0, pl.ds(r0, c_len), :])
        logf = jnp.log(f)
        k = 1.0 - f
        h1 = logf.astype(BF16)
        r1 = logf - h1.astype(F32)
        h2 = r1.astype(BF16)
        h3 = (r1 - h2.astype(F32)).astype(BF16)
        cs = jnp.dot(tri, jnp.concatenate([h1, h2, h3], axis=1), preferred_element_type=F32)
        b = cs[:, :LANES] + cs[:, LANES:2 * LANES] + cs[:, 2 * LANES:]
        b_last = b[c_len - 1:c_len]

        st = st_ref[...]
        o = _nt_dot((q * jnp.exp(b)).astype(BF16), st.astype(BF16))

        a_rows = [jnp.zeros((sub, c_len), F32)]
        for i in range(1, c_len // sub):
            ref = b[i * sub - 1:i * sub]
            qt = (q[i * sub:(i + 1) * sub] * jnp.exp(b[i * sub:(i + 1) * sub] - ref)).astype(BF16)
            kt = k[:i * sub] * jnp.exp(ref - b[:i * sub])
            kt = jnp.concatenate([kt, jnp.zeros((c_len - i * sub, LANES), F32)], axis=0).astype(BF16)
            a_rows.append(_nt_dot(qt, kt))
        a = jnp.concatenate(a_rows, axis=0).astype(BF16)
        o = o + jnp.dot(a, v_bf, preferred_element_type=F32)
        o = o + jnp.concatenate(
            [_hgrn_diag_block(q[j * sub:(j + 1) * sub], k[j * sub:(j + 1) * sub],
                              b[j * sub:(j + 1) * sub], v[j * sub:(j + 1) * sub])
             for j in range(c_len // sub)], axis=0)

        y = o * _rms_scale(o) * gw
        o_ref[0, pl.ds(r0, c_len), :] = (y * (g * jax.nn.sigmoid(g))).astype(o_ref.dtype)

        k_end = (k * jnp.exp(b_last - b)).astype(BF16)
        st_ref[...] = st * jnp.exp(b_last) + jnp.dot(v.T.astype(BF16), k_end,
                                                      preferred_element_type=F32)
        return carry

    lax.fori_loop(0, t_len // c_len, chunk, 0)


def _hgrn(oa, of, lb_logits, gnorm_w):
    bsz, _, t, _ = oa.shape
    slab = lambda slot: pl.BlockSpec((1, 1, t, LANES), lambda b, h: (b, slot + h, 0, 0))
    return pl.pallas_call(
        _hgrn_kernel,
        grid=(bsz, HG_HEADS),
        in_specs=[pl.BlockSpec((lb_logits.shape[0], LANES), lambda b, h: (0, h)),
                  pl.BlockSpec((1, LANES), lambda b, h: (0, 0)),
                  slab(SLOT_HQ),
                  pl.BlockSpec((1, 1, t, LANES), lambda b, h: (b, h, 0, 0)),
                  slab(SLOT_HI), slab(SLOT_HG)],
        out_specs=pl.BlockSpec((1, t, LANES), lambda b, h: (b, 0, h)),
        out_shape=jax.ShapeDtypeStruct((bsz, t, HG_HEADS * LANES), BF16),
        scratch_shapes=[pltpu.VMEM((HG_DIM, HG_DIM), F32)],
        compiler_params=pltpu.CompilerParams(
            dimension_semantics=("arbitrary", "arbitrary"), vmem_limit_bytes=VMEM_LIMIT_BYTES),
        name="hgrn",
    )(lb_logits, gnorm_w.reshape(1, LANES), oa, of, oa, oa)


def _attn_kernel(lq1_ref, lk1_ref, lq2_ref, lk2_ref, sw_ref, q_ref, k_ref, v_ref, o_ref):
    tq, tk = ATT_TQ, ATT_TK
    h = pl.program_id(1)
    qi = pl.program_id(2)
    slope = jnp.float32(0.0)
    for hh in range(DA_HEADS):
        slope = jnp.where(h == hh, jnp.float32(2.0 ** (-8.0 * (hh + 1) / DA_HEADS)), slope)
    lam = (jnp.exp(jnp.sum(lq1_ref[...] * lk1_ref[...], axis=-1, keepdims=True))
           - jnp.exp(jnp.sum(lq2_ref[...] * lk2_ref[...], axis=-1, keepdims=True))
           + LAMBDA_INIT)

    q = q_ref[0, 0]
    lane = lax.broadcasted_iota(jnp.int32, (tq, LANES), 1)
    zero = jnp.zeros_like(q)
    qq = jnp.concatenate([jnp.where(lane < DA_HEAD_DIM, q, zero),
                          jnp.where(lane >= DA_HEAD_DIM, q, zero)], axis=0)
    ones = jnp.ones((tk, LANES), BF16)
    col = lax.broadcasted_iota(jnp.int32, (1, tk), 1)

    def step(j, carry, masked):
        m, acc = carry
        k0 = pl.multiple_of(j * tk, tk)
        k = k_ref[0, 0, pl.ds(k0, tk), :]
        v = v_ref[0, 0, pl.ds(k0, tk), :]
        s = _nt_dot(qq, k) + slope * (col + (j - qi) * tk).astype(F32)
        if masked:
            row = lax.broadcasted_iota(jnp.int32, (2 * tq, tk), 0)
            row = jnp.where(row >= tq, row - tq, row)
            s = jnp.where(col <= row, s, NEG_BIG)
        m_new = jnp.maximum(m, jnp.max(s, axis=-1, keepdims=True))
        p = jnp.exp(s - m_new).astype(BF16)
        acc = jnp.exp(m - m_new) * acc + jnp.dot(
            p, jnp.concatenate([v, ones], axis=1), preferred_element_type=F32)
        return m_new, acc

    carry = (jnp.full((2 * tq, 1), NEG_BIG, F32), jnp.zeros((2 * tq, 2 * LANES), F32))
    carry = lax.fori_loop(0, qi, functools.partial(step, masked=False), carry)
    _, acc = step(qi, carry, masked=True)
    o = (acc[:tq, :LANES] / acc[:tq, LANES:]) - lam * (acc[tq:, :LANES] / acc[tq:, LANES:])
    y = o * _rms_scale(o) * sw_ref[...]
    o_ref[0] = (y * (1.0 - LAMBDA_INIT)).astype(o_ref.dtype)


def _attn(oa, lq1, lk1, lq2, lk2, subln_w):
    bsz, _, t, _ = oa.shape
    assert ATT_TQ == ATT_TK
    vec = lambda n: pl.BlockSpec((1, n), lambda b, h, i: (0, 0))
    return pl.pallas_call(
        _attn_kernel,
        grid=(bsz, DA_HEADS, t // ATT_TQ),
        in_specs=[vec(DA_HEAD_DIM), vec(DA_HEAD_DIM), vec(DA_HEAD_DIM), vec(DA_HEAD_DIM),
                  vec(2 * DA_HEAD_DIM),
                  pl.BlockSpec((1, 1, ATT_TQ, LANES), lambda b, h, i: (b, SLOT_AQ + h, i, 0)),
                  pl.BlockSpec((1, 1, t, LANES), lambda b, h, i: (b, SLOT_AK + h, 0, 0)),
                  pl.BlockSpec((1, 1, t, LANES), lambda b, h, i: (b, SLOT_AV + h, 0, 0))],
        out_specs=pl.BlockSpec((1, ATT_TQ, LANES), lambda b, h, i: (b, i, h)),
        out_shape=jax.ShapeDtypeStruct((bsz, t, DA_HEADS * LANES), BF16),
        compiler_params=pltpu.CompilerParams(
            dimension_semantics=("arbitrary", "arbitrary", "arbitrary"),
            vmem_limit_bytes=VMEM_LIMIT_BYTES),
        name="attn",
    )(lq1, lk1, lq2, lk2, subln_w, oa, oa, oa)


def _ffn_kernel(x_ref, mod_ref, ohg_ref, oda_ref, wo_ref, n2_ref, wg_ref, wu_ref, wd_ref, fw_ref,
                o_ref):
    x = x_ref[0]
    m = mod_ref[0]
    hgw = ohg_ref.shape[2]
    mix = (jnp.dot(ohg_ref[0], wo_ref[:hgw], preferred_element_type=F32)
           + jnp.dot(oda_ref[0], wo_ref[hgw:], preferred_element_type=F32))
    hcur = x + m[2:3] * mix
    u = (hcur * _rms_scale(hcur) * n2_ref[...] * (1.0 + m[4:5]) + m[3:4]).astype(BF16)
    acc = jnp.zeros(x.shape, F32)
    for c in range(wg_ref.shape[1] // FFN_FC):
        sl = slice(c * FFN_FC, (c + 1) * FFN_FC)
        gate = jnp.dot(u, wg_ref[:, sl], preferred_element_type=F32)
        up = jnp.dot(u, wu_ref[:, sl], preferred_element_type=F32)
        act = (gate * jax.nn.sigmoid(gate) * up).astype(BF16)
        acc = acc + jnp.dot(act, wd_ref[sl, :], preferred_element_type=F32)
    hcur = hcur + m[5:6] * acc
    o_ref[0] = hcur * _rms_scale(hcur) * fw_ref[...]


def _ffn(x, mod3, o_hg, o_da, w_out_bf, norm2_w, wg_bf, wu_bf, wd_bf, final_w):
    bsz, t, d = x.shape
    tm = FFN_TM
    dff = wg_bf.shape[1]
    assert dff % FFN_FC == 0
    tile = lambda w: pl.BlockSpec((1, tm, w), lambda b, i: (b, i, 0))
    const = lambda r, c: pl.BlockSpec((r, c), lambda b, i: (0, 0), pipeline_mode=pl.Buffered(1))
    return pl.pallas_call(
        _ffn_kernel,
        grid=(bsz, t // tm),
        in_specs=[tile(d),
                  pl.BlockSpec((1, N_MOD, d), lambda b, i: (b, 0, 0)),
                  tile(o_hg.shape[2]), tile(o_da.shape[2]),
                  const(d, d), const(1, d), const(d, dff), const(d, dff), const(dff, d),
                  const(1, d)],
        out_specs=tile(d),
        out_shape=jax.ShapeDtypeStruct((bsz, t, d), F32),
        compiler_params=pltpu.CompilerParams(
            dimension_semantics=("arbitrary", "arbitrary"), vmem_limit_bytes=VMEM_LIMIT_BYTES),
        name="ffn",
    )(x, mod3, o_hg, o_da, w_out_bf, norm2_w.reshape(1, d), wg_bf, wu_bf, wd_bf,
      final_w.reshape(1, d))


def kernel(x, c, w_ada, b_ada, norm1_w, w_in, hgrn_lb_logits, hgrn_gnorm_w, diff_lambda_q1,
           diff_lambda_k1, diff_lambda_q2, diff_lambda_k2, diff_subln_w, w_out, norm2_w,
           w_ffn_gate, w_ffn_up, w_ffn_down, final_norm_w):
    bsz, t, d = x.shape
    assert w_ada.shape[0] == 1, "single-layer trunk"
    l = LAYER
    mod3 = _ada(c, w_ada[l], b_ada[l]).reshape(bsz, N_MOD, d)
    oa, of = _inproj(x, mod3, norm1_w[l], w_in[l].astype(BF16))
    o_hg = _hgrn(oa, of, hgrn_lb_logits, hgrn_gnorm_w[l])
    o_da = _attn(oa, diff_lambda_q1[l:l + 1], diff_lambda_k1[l:l + 1], diff_lambda_q2[l:l + 1],
                 diff_lambda_k2[l:l + 1], diff_subln_w[l:l + 1])
    return _ffn(x, mod3, o_hg, o_da, w_out[l].astype(BF16), norm2_w[l],
                w_ffn_gate[l].astype(BF16), w_ffn_up[l].astype(BF16),
                w_ffn_down[l].astype(BF16), final_norm_w)
```

```python
import functools
import math

import jax
import jax.numpy as jnp
from jax import lax
from jax.experimental import pallas as pl
from jax.experimental.pallas import tpu as pltpu

F32 = jnp.float32
BF16 = jnp.bfloat16

HG_HEADS = 4
HG_DIM = 128
DA_HEADS = 4
DA_HEAD_DIM = 64
N_MOD = 6
EPS = 1e-6
LAYER = 0
LAMBDA_INIT = 0.8 - 0.6 * math.exp(-0.3 * LAYER)
LOG2_E = math.log2(math.e)

LANES = 128
SUBLANES = 8
BF16_ROWS = 16
VMEM_LIMIT_BYTES = 56 * 1024 * 1024

HGRN_CHUNK = 64
HGRN_SUB = 16
INPROJ_TM = 512
FFN_TM = 512
FFN_FC = 256
ATT_T = 256
NEG_BIG = -1e30
N_BIAS_PIECES = 3


def _nt_dot(a, b):
    return lax.dot_general(a, b, (((1,), (1,)), ((), ())), preferred_element_type=F32)


def _rms_scale(x):
    return lax.rsqrt(jnp.mean(x * x, axis=-1, keepdims=True) + EPS)


def _alibi_slope(head):
    return 2.0 ** (-8.0 * (head + 1) / DA_HEADS)


def _ada_kernel(c_ref, w_ref, b_ref, o_ref):
    c = c_ref[...]
    a = c * jax.nn.sigmoid(c)
    o_ref[...] = jnp.dot(a, w_ref[...], preferred_element_type=F32,
                         precision=lax.Precision.HIGHEST) + b_ref[...]


def _ada(c, w_ada, b_ada):
    bsz, d = c.shape
    n = w_ada.shape[1]
    tn = 1536
    return pl.pallas_call(
        _ada_kernel,
        grid=(n // tn,),
        in_specs=[pl.BlockSpec((bsz, d), lambda j: (0, 0)),
                  pl.BlockSpec((d, tn), lambda j: (0, j)),
                  pl.BlockSpec((1, tn), lambda j: (0, j))],
        out_specs=pl.BlockSpec((bsz, tn), lambda j: (0, j)),
        out_shape=jax.ShapeDtypeStruct((bsz, n), F32),
        compiler_params=pltpu.CompilerParams(
            dimension_semantics=("arbitrary",), vmem_limit_bytes=VMEM_LIMIT_BYTES),
        name="ada",
    )(c, w_ada, b_ada.reshape(1, n))


SLOT_HQ, SLOT_HI, SLOT_HG, SLOT_AQ1, SLOT_AQ2, SLOT_AK1, SLOT_AK2 = 0, 4, 8, 12, 16, 20, 24
N_SLABS = 28
GRP_HQ, GRP_HF, GRP_HI, GRP_HG, GRP_AQ, GRP_AK = range(6)


def _augment(ph, extra, first_map):
    lane = lax.broadcasted_iota(jnp.int32, ph.shape, 1)
    base = DA_HEAD_DIM if first_map else 0
    aug = jnp.zeros_like(ph)
    for n, e in enumerate(extra):
        aug = jnp.where(lane == base + n, e, aug)
    keep = (lane < DA_HEAD_DIM) if first_map else (lane >= DA_HEAD_DIM)
    return jnp.where(keep, ph, aug).astype(BF16)


def _inproj_kernel(x_ref, mod_ref, n1_ref, w_ref, wvt_ref, oa_ref, of_ref, ovt_ref):
    tm = x_ref.shape[1]
    x = x_ref[0]
    m = mod_ref[0]
    y = x * _rms_scale(x) * n1_ref[...]
    u = (y * (1.0 + m[1:2]) + m[0:1]).astype(BF16)
    pos = (pl.program_id(1) * tm + lax.broadcasted_iota(jnp.int32, (tm, LANES), 0)).astype(F32)
    ones = [jnp.ones((tm, LANES), F32)] * N_BIAS_PIECES
    for j in range(w_ref.shape[1] // (2 * LANES)):
        p = jnp.dot(u, w_ref[:, j * 2 * LANES:(j + 1) * 2 * LANES], preferred_element_type=F32)
        for half in range(2):
            ph = p[:, half * LANES:(half + 1) * LANES]
            grp, head = divmod(2 * j + half, 4)
            if grp == GRP_HF:
                of_ref[0, head] = ph
            elif grp == GRP_AQ:
                ph = ph * (LOG2_E * DA_HEAD_DIM ** -0.5)
                oa_ref[0, SLOT_AQ1 + head] = _augment(ph, ones, True)
                oa_ref[0, SLOT_AQ2 + head] = _augment(ph, ones, False)
            elif grp == GRP_AK:
                bias = (LOG2_E * _alibi_slope(head)) * pos
                pieces = []
                for _ in range(N_BIAS_PIECES):
                    piece = bias.astype(BF16).astype(F32)
                    pieces.append(piece)
                    bias = bias - piece
                oa_ref[0, SLOT_AK1 + head] = _augment(ph, pieces, True)
                oa_ref[0, SLOT_AK2 + head] = _augment(ph, pieces, False)
            else:
                slot = {GRP_HQ: SLOT_HQ, GRP_HI: SLOT_HI, GRP_HG: SLOT_HG}[grp]
                oa_ref[0, slot + head] = ph.astype(BF16)
    vt = _nt_dot(wvt_ref[...], u)
    for head in range(DA_HEADS):
        for jt in range(tm // ATT_T):
            ovt_ref[0, head, jt] = vt[head * LANES:(head + 1) * LANES,
                                      jt * ATT_T:(jt + 1) * ATT_T].astype(BF16)


def _inproj(x, mod3, norm1_w, w_main_bf, w_vt_bf):
    bsz, t, d = x.shape
    n = w_main_bf.shape[1]
    tm = INPROJ_TM
    return pl.pallas_call(
        _inproj_kernel,
        grid=(bsz, t // tm),
        in_specs=[pl.BlockSpec((1, tm, d), lambda b, i: (b, i, 0)),
                  pl.BlockSpec((1, N_MOD, d), lambda b, i: (b, 0, 0)),
                  pl.BlockSpec((1, d), lambda b, i: (0, 0)),
                  pl.BlockSpec((d, n), lambda b, i: (0, 0), pipeline_mode=pl.Buffered(1)),
                  pl.BlockSpec(w_vt_bf.shape, lambda b, i: (0, 0), pipeline_mode=pl.Buffered(1))],
        out_specs=[pl.BlockSpec((1, N_SLABS, tm, LANES), lambda b, i: (b, 0, i, 0)),
                   pl.BlockSpec((1, HG_HEADS, tm, LANES), lambda b, i: (b, 0, i, 0)),
                   pl.BlockSpec((1, DA_HEADS, tm // ATT_T, LANES, ATT_T),
                                lambda b, i: (b, 0, i, 0, 0))],
        out_shape=[jax.ShapeDtypeStruct((bsz, N_SLABS, t, LANES), BF16),
                   jax.ShapeDtypeStruct((bsz, HG_HEADS, t, LANES), F32),
                   jax.ShapeDtypeStruct((bsz, DA_HEADS, t // ATT_T, LANES, ATT_T), BF16)],
        compiler_params=pltpu.CompilerParams(
            dimension_semantics=("arbitrary", "arbitrary"), vmem_limit_bytes=VMEM_LIMIT_BYTES),
        name="inproj",
    )(x, mod3, norm1_w.reshape(1, d), w_main_bf, w_vt_bf)


def _hgrn_diag_block(qb, kb, bb, vb):
    half = SUBLANES
    row = lax.broadcasted_iota(jnp.int32, (half, LANES), 0)
    q_lo, q_hi = qb[:half], qb[half:]
    b_lo, b_hi = bb[:half], bb[half:]
    o_lo = jnp.zeros((half, LANES), F32)
    o_hi = jnp.zeros((half, LANES), F32)
    for s in range(HGRN_SUB):
        ks, bs, vs = kb[s:s + 1], bb[s:s + 1], vb[s:s + 1]
        if s < half:
            rel = jnp.where(row >= s, b_lo - bs, NEG_BIG)
            col = jnp.sum(q_lo * ks * jnp.exp(rel), axis=-1, keepdims=True)
            o_lo = o_lo + col * vs
            col = jnp.sum(q_hi * ks * jnp.exp(b_hi - bs), axis=-1, keepdims=True)
            o_hi = o_hi + col * vs
        else:
            rel = jnp.where(row >= s - half, b_hi - bs, NEG_BIG)
            col = jnp.sum(q_hi * ks * jnp.exp(rel), axis=-1, keepdims=True)
            o_hi = o_hi + col * vs
    return jnp.concatenate([o_lo, o_hi], axis=0)


def _hgrn_kernel(lbl_ref, gw_ref, q_ref, f_ref, i_ref, g_ref, o_ref, st_ref):
    c_len, sub = HGRN_CHUNK, HGRN_SUB
    t_len = q_ref.shape[2]
    l = lbl_ref[...]
    e = jnp.exp(l - jnp.max(l, axis=0, keepdims=True))
    lb = jnp.sum(e[:LAYER + 1], axis=0, keepdims=True) / jnp.sum(e, axis=0, keepdims=True)
    gw = gw_ref[...]
    r_i = lax.broadcasted_iota(jnp.int32, (c_len, c_len), 0)
    c_i = lax.broadcasted_iota(jnp.int32, (c_len, c_len), 1)
    tri = (r_i >= c_i).astype(BF16)
    st_ref[...] = jnp.zeros_like(st_ref)

    def chunk(c, carry):
        r0 = pl.multiple_of(c * c_len, c_len)
        q = q_ref[0, 0, pl.ds(r0, c_len), :].astype(F32)
        v_bf = i_ref[0, 0, pl.ds(r0, c_len), :]
        v = v_bf.astype(F32)
        g = g_ref[0, 0, pl.ds(r0, c_len), :].astype(F32)
        f = lb + (1.0 - lb) * jax.nn.sigmoid(f_ref[0, 0, pl.ds(r0, c_len), :])
        logf = jnp.log(f)
        k = 1.0 - f
        h1 = logf.astype(BF16)
        r1 = logf - h1.astype(F32)
        h2 = r1.astype(BF16)
        h3 = (r1 - h2.astype(F32)).astype(BF16)
        cs = jnp.dot(tri, jnp.concatenate([h1, h2, h3], axis=1), preferred_element_type=F32)
        b = cs[:, :LANES] + cs[:, LANES:2 * LANES] + cs[:, 2 * LANES:]
        b_last = b[c_len - 1:c_len]

        st = st_ref[...]
        o = _nt_dot((q * jnp.exp(b)).astype(BF16), st.astype(BF16))

        a_rows = [jnp.zeros((sub, c_len), F32)]
        for i in range(1, c_len // sub):
            ref = b[i * sub - 1:i * sub]
            qt = (q[i * sub:(i + 1) * sub] * jnp.exp(b[i * sub:(i + 1) * sub] - ref)).astype(BF16)
            kt = k[:i * sub] * jnp.exp(ref - b[:i * sub])
            kt = jnp.concatenate([kt, jnp.zeros((c_len - i * sub, LANES), F32)], axis=0).astype(BF16)
            a_rows.append(_nt_dot(qt, kt))
        a = jnp.concatenate(a_rows, axis=0).astype(BF16)
        o = o + jnp.dot(a, v_bf, preferred_element_type=F32)
        o = o + jnp.concatenate(
            [_hgrn_diag_block(q[j * sub:(j + 1) * sub], k[j * sub:(j + 1) * sub],
                              b[j * sub:(j + 1) * sub], v[j * sub:(j + 1) * sub])
             for j in range(c_len // sub)], axis=0)

        y = o * _rms_scale(o) * gw
        o_ref[0, pl.ds(r0, c_len), :] = (y * (g * jax.nn.sigmoid(g))).astype(o_ref.dtype)

        k_end = (k * jnp.exp(b_last - b)).astype(BF16)
        st_ref[...] = st * jnp.exp(b_last) + jnp.dot(v.T.astype(BF16), k_end,
                                                      preferred_element_type=F32)
        return carry

    lax.fori_loop(0, t_len // c_len, chunk, 0)


def _hgrn(oa, of, lb_logits, gnorm_w):
    bsz, _, t, _ = oa.shape
    slab = lambda slot: pl.BlockSpec((1, 1, t, LANES), lambda b, h: (b, slot + h, 0, 0))
    return pl.pallas_call(
        _hgrn_kernel,
        grid=(bsz, HG_HEADS),
        in_specs=[pl.BlockSpec((lb_logits.shape[0], LANES), lambda b, h: (0, h)),
                  pl.BlockSpec((1, LANES), lambda b, h: (0, 0)),
                  slab(SLOT_HQ),
                  pl.BlockSpec((1, 1, t, LANES), lambda b, h: (b, h, 0, 0)),
                  slab(SLOT_HI), slab(SLOT_HG)],
        out_specs=pl.BlockSpec((1, t, LANES), lambda b, h: (b, 0, h)),
        out_shape=jax.ShapeDtypeStruct((bsz, t, HG_HEADS * LANES), BF16),
        scratch_shapes=[pltpu.VMEM((HG_DIM, HG_DIM), F32)],
        compiler_params=pltpu.CompilerParams(
            dimension_semantics=("arbitrary", "arbitrary"), vmem_limit_bytes=VMEM_LIMIT_BYTES),
        name="hgrn",
    )(lb_logits, gnorm_w.reshape(1, LANES), oa, of, oa, oa)


def _attn_kernel(lq1_ref, lk1_ref, lq2_ref, lk2_ref, sw_ref, q1_ref, q2_ref, k1_ref, k2_ref,
                 vt_ref, o_ref, s_ref, mx_ref, m_ref, acc_ref):
    t = ATT_T
    qi = pl.program_id(2)
    lam = (jnp.exp(jnp.sum(lq1_ref[...] * lk1_ref[...], axis=-1, keepdims=True))
           - jnp.exp(jnp.sum(lq2_ref[...] * lk2_ref[...], axis=-1, keepdims=True))
           + LAMBDA_INIT)
    q_maps = (q1_ref[0, 0], q2_ref[0, 0])
    k_refs = (k1_ref, k2_ref)
    ones = jnp.ones((BF16_ROWS, t), BF16)
    key = lax.broadcasted_iota(jnp.int32, (t, t), 0)
    qry = lax.broadcasted_iota(jnp.int32, (t, t), 1)

    m_ref[...] = jnp.full(m_ref.shape, NEG_BIG, F32)
    acc_ref[...] = jnp.zeros(acc_ref.shape, F32)

    def key_max(s):
        quarter = t // 4
        parts = [s[i * quarter:(i + 1) * quarter] for i in range(4)]
        part = jnp.maximum(jnp.maximum(parts[0], parts[1]), jnp.maximum(parts[2], parts[3]))
        return jnp.max(part, axis=0, keepdims=True)

    def scores_to(buf, j):
        k0 = pl.multiple_of(j * t, t)
        for mp in range(2):
            s = _nt_dot(k_refs[mp][0, 0, pl.ds(k0, t), :], q_maps[mp])
            s_ref[buf, mp] = s
            mx_ref[buf, mp] = key_max(s)

    def consume(buf, j, masked):
        vta = jnp.concatenate([vt_ref[0, 0, j], ones], axis=0)
        for mp in range(2):
            s = s_ref[buf, mp]
            if masked:
                s = jnp.where(key <= qry, s, NEG_BIG)
                mx = key_max(s)
            else:
                mx = mx_ref[buf, mp]
            m_old = m_ref[mp]
            m_new = jnp.maximum(m_old, mx)
            p = jnp.exp2(s - m_new).astype(BF16)
            acc_ref[mp] = (jnp.exp2(m_old - m_new) * acc_ref[mp]
                           + jnp.dot(vta, p, preferred_element_type=F32))
            m_ref[mp] = m_new

    scores_to(0, 0)

    def pair(i, carry):
        j = 2 * i
        scores_to(1, j + 1)
        consume(0, j, masked=False)
        scores_to(0, j + 2)
        consume(1, j + 1, masked=False)
        return carry

    lax.fori_loop(0, qi // 2, pair, 0)

    @pl.when(qi % 2 == 1)
    def _():
        scores_to(1, qi)
        consume(0, qi - 1, masked=False)
        consume(1, qi, masked=True)

    @pl.when(qi % 2 == 0)
    def _():
        consume(0, qi, masked=True)

    acc1, acc2 = acc_ref[0], acc_ref[1]
    o =(acc1[:LANES] * (1.0 / acc1[LANES:LANES + 1])
         - acc2[:LANES] * (lam / acc2[LANES:LANES + 1]))
    y = o * lax.rsqrt(jnp.mean(o * o, axis=0, keepdims=True) + EPS) * sw_ref[...]
    o_ref[0] = (y * (1.0 - LAMBDA_INIT)).T.astype(o_ref.dtype)


def _attn(oa, ovt, lq1, lk1, lq2, lk2, subln_w):
    bsz, _, t, _ = oa.shape
    tq = ATT_T
    vec = lambda n: pl.BlockSpec((1, n), lambda b, h, i: (0, 0))
    q_slab = lambda slot: pl.BlockSpec((1, 1, tq, LANES), lambda b, h, i: (b, slot + h, i, 0))
    k_slab = lambda slot: pl.BlockSpec((1, 1, t, LANES), lambda b, h, i: (b, slot + h, 0, 0))
    return pl.pallas_call(
        _attn_kernel,
        grid=(bsz, DA_HEADS, t // tq),
        in_specs=[vec(DA_HEAD_DIM), vec(DA_HEAD_DIM), vec(DA_HEAD_DIM), vec(DA_HEAD_DIM),
                  pl.BlockSpec((2 * DA_HEAD_DIM, 1), lambda b, h, i: (0, 0)),
                  q_slab(SLOT_AQ1), q_slab(SLOT_AQ2), k_slab(SLOT_AK1), k_slab(SLOT_AK2),
                  pl.BlockSpec((1, 1, t // tq, LANES, tq), lambda b, h, i: (b, h, 0, 0, 0))],
        out_specs=pl.BlockSpec((1, tq, LANES), lambda b, h, i: (b, i, h)),
        out_shape=jax.ShapeDtypeStruct((bsz, t, DA_HEADS * LANES), BF16),
        scratch_shapes=[pltpu.VMEM((2, 2, tq, tq), F32),
                        pltpu.VMEM((2, 2, 1, tq), F32),
                        pltpu.VMEM((2, 1, tq), F32),
                        pltpu.VMEM((2, LANES + BF16_ROWS, tq), F32)],
        compiler_params=pltpu.CompilerParams(
            dimension_semantics=("arbitrary", "arbitrary", "arbitrary"),
            vmem_limit_bytes=VMEM_LIMIT_BYTES),
        name="attn",
    )(lq1, lk1, lq2, lk2, subln_w.reshape(2 * DA_HEAD_DIM, 1), oa, oa, oa, oa, ovt)


def _ffn_kernel(x_ref, mod_ref, ohg_ref, oda_ref, wo_ref, n2_ref, wg_ref, wu_ref, wd_ref, fw_ref,
                o_ref):
    x = x_ref[0]
    m = mod_ref[0]
    hgw = ohg_ref.shape[2]
    mix = (jnp.dot(ohg_ref[0], wo_ref[:hgw], preferred_element_type=F32)
           + jnp.dot(oda_ref[0], wo_ref[hgw:], preferred_element_type=F32))
    hcur = x + m[2:3] * mix
    u = (hcur * _rms_scale(hcur) * n2_ref[...] * (1.0 + m[4:5]) + m[3:4]).astype(BF16)
    acc = jnp.zeros(x.shape, F32)
    for c in range(wg_ref.shape[1] // FFN_FC):
        sl = slice(c * FFN_FC, (c + 1) * FFN_FC)
        gate = jnp.dot(u, wg_ref[:, sl], preferred_element_type=F32)
        up = jnp.dot(u, wu_ref[:, sl], preferred_element_type=F32)
        act = (gate * jax.nn.sigmoid(gate) * up).astype(BF16)
        acc = acc + jnp.dot(act, wd_ref[sl, :], preferred_element_type=F32)
    hcur = hcur + m[5:6] * acc
    o_ref[0] = hcur * _rms_scale(hcur) * fw_ref[...]


def _ffn(x, mod3, o_hg, o_da, w_out_bf, norm2_w, wg_bf, wu_bf, wd_bf, final_w):
    bsz, t, d = x.shape
    tm = FFN_TM
    dff = wg_bf.shape[1]
    assert dff % FFN_FC == 0
    tile = lambda w: pl.BlockSpec((1, tm, w), lambda b, i: (b, i, 0))
    const = lambda r, c: pl.BlockSpec((r, c), lambda b, i: (0, 0), pipeline_mode=pl.Buffered(1))
    return pl.pallas_call(
        _ffn_kernel,
        grid=(bsz, t // tm),
        in_specs=[tile(d),
                  pl.BlockSpec((1, N_MOD, d), lambda b, i: (b, 0, 0)),
                  tile(o_hg.shape[2]), tile(o_da.shape[2]),
                  const(d, d), const(1, d), const(d, dff), const(d, dff), const(dff, d),
                  const(1, d)],
        out_specs=tile(d),
        out_shape=jax.ShapeDtypeStruct((bsz, t, d), F32),
        compiler_params=pltpu.CompilerParams(
            dimension_semantics=("arbitrary", "arbitrary"), vmem_limit_bytes=VMEM_LIMIT_BYTES),
        name="ffn",
    )(x, mod3, o_hg, o_da, w_out_bf, norm2_w.reshape(1, d), wg_bf, wu_bf, wd_bf,
      final_w.reshape(1, d))


def kernel(x, c, w_ada, b_ada, norm1_w, w_in, hgrn_lb_logits, hgrn_gnorm_w, diff_lambda_q1,
           diff_lambda_k1, diff_lambda_q2, diff_lambda_k2, diff_subln_w, w_out, norm2_w,
           w_ffn_gate, w_ffn_up, w_ffn_down, final_norm_w):
    bsz, t, d = x.shape
    assert w_ada.shape[0] == 1, "single-layer trunk"
    l = LAYER
    mod3 = _ada(c, w_ada[l], b_ada[l]).reshape(bsz, N_MOD, d)
    n_main = w_in.shape[2] - DA_HEADS * 2 * DA_HEAD_DIM
    oa, of, ovt = _inproj(x, mod3, norm1_w[l], w_in[l, :, :n_main].astype(BF16),
                          w_in[l, :, n_main:].T.astype(BF16))
    o_hg = _hgrn(oa, of, hgrn_lb_logits, hgrn_gnorm_w[l])
    o_da = _attn(oa, ovt, diff_lambda_q1[l:l + 1], diff_lambda_k1[l:l + 1],
                 diff_lambda_q2[l:l + 1], diff_lambda_k2[l:l + 1], diff_subln_w[l])
    return _ffn(x, mod3, o_hg, o_da, w_out[l].astype(BF16), norm2_w[l],
                w_ffn_gate[l].astype(BF16), w_ffn_up[l].astype(BF16),
                w_ffn_down[l].astype(BF16), final_norm_w)
```

```python
import math

import jax
import jax.numpy as jnp
from jax import lax
from jax.experimental import pallas as pl
from jax.experimental.pallas import tpu as pltpu

F32 = jnp.float32
BF16 = jnp.bfloat16

HG_HEADS = 4
HG_DIM = 128
DA_HEADS = 4
DA_HEAD_DIM = 64
N_MOD = 6
EPS = 1e-6
LAYER = 0
LAMBDA_INIT = 0.8 - 0.6 * math.exp(-0.3 * LAYER)
LOG2_E = math.log2(math.e)

LANES = 128
SUBLANES = 8
BF16_ROWS = 16
VMEM_LIMIT_BYTES = 56 * 1024 * 1024

HGRN_CHUNK = 64
HGRN_SUB = 16
INPROJ_TM = 512
FFN_TM = 512
FFN_FC = 256
ATT_T = 256
ATT_LOOKAHEAD = 2
NEG_BIG = -1e30
N_BIAS_PIECES = 3


def _nt_dot(a, b):
    return lax.dot_general(a, b, (((1,), (1,)), ((), ())), preferred_element_type=F32)


def _rms_scale(x):
    return lax.rsqrt(jnp.mean(x * x, axis=-1, keepdims=True) + EPS)


def _alibi_slope(head):
    return 2.0 ** (-8.0 * (head + 1) / DA_HEADS)


def _ada_kernel(c_ref, w_ref, b_ref, o_ref):
    c = c_ref[...]
    a = c * jax.nn.sigmoid(c)
    o_ref[...] = jnp.dot(a, w_ref[...], preferred_element_type=F32,
                         precision=lax.Precision.HIGHEST) + b_ref[...]


def _ada(c, w_ada, b_ada):
    bsz, d = c.shape
    n = w_ada.shape[1]
    tn = 1536
    return pl.pallas_call(
        _ada_kernel,
        grid=(n // tn,),
        in_specs=[pl.BlockSpec((bsz, d), lambda j: (0, 0)),
                  pl.BlockSpec((d, tn), lambda j: (0, j)),
                  pl.BlockSpec((1, tn), lambda j: (0, j))],
        out_specs=pl.BlockSpec((bsz, tn), lambda j: (0, j)),
        out_shape=jax.ShapeDtypeStruct((bsz, n), F32),
        compiler_params=pltpu.CompilerParams(
            dimension_semantics=("arbitrary",), vmem_limit_bytes=VMEM_LIMIT_BYTES),
        name="ada",
    )(c, w_ada, b_ada.reshape(1, n))


SLOT_HQ, SLOT_HI, SLOT_HG, SLOT_AQ1, SLOT_AQ2, SLOT_AK1, SLOT_AK2 = 0, 4, 8, 12, 16, 20, 24
N_SLABS = 28
GRP_HQ, GRP_HF, GRP_HI, GRP_HG, GRP_AQ, GRP_AK = range(6)


def _augment(ph, extra, first_map):
    lane = lax.broadcasted_iota(jnp.int32, ph.shape, 1)
    base = DA_HEAD_DIM if first_map else 0
    aug = jnp.zeros_like(ph)
    for n, e in enumerate(extra):
        aug = jnp.where(lane == base + n, e, aug)
    keep = (lane < DA_HEAD_DIM) if first_map else (lane >= DA_HEAD_DIM)
    return jnp.where(keep, ph, aug).astype(BF16)


def _inproj_kernel(x_ref, mod_ref, n1_ref, w_ref, wvt_ref, oa_ref, of_ref, ovt_ref):
    tm = x_ref.shape[1]
    x = x_ref[0]
    m = mod_ref[0]
    y = x * _rms_scale(x) * n1_ref[...]
    u = (y * (1.0 + m[1:2]) + m[0:1]).astype(BF16)
    pos = (pl.program_id(1) * tm + lax.broadcasted_iota(jnp.int32, (tm, LANES), 0)).astype(F32)
    ones = [jnp.ones((tm, LANES), F32)] * N_BIAS_PIECES
    for j in range(w_ref.shape[1] // (2 * LANES)):
        p = jnp.dot(u, w_ref[:, j * 2 * LANES:(j + 1) * 2 * LANES], preferred_element_type=F32)
        for half in range(2):
            ph = p[:, half * LANES:(half + 1) * LANES]
            grp, head = divmod(2 * j + half, 4)
            if grp == GRP_HF:
                of_ref[0, head] = ph
            elif grp == GRP_AQ:
                ph = ph * (LOG2_E * DA_HEAD_DIM ** -0.5)
                oa_ref[0, SLOT_AQ1 + head] = _augment(ph, ones, True)
                oa_ref[0, SLOT_AQ2 + head] = _augment(ph, ones, False)
            elif grp == GRP_AK:
                bias = (LOG2_E * _alibi_slope(head)) * pos
                pieces = []
                for _ in range(N_BIAS_PIECES):
                    piece = bias.astype(BF16).astype(F32)
                    pieces.append(piece)
                    bias = bias - piece
                oa_ref[0, SLOT_AK1 + head] = _augment(ph, pieces, True)
                oa_ref[0, SLOT_AK2 + head] = _augment(ph, pieces, False)
            else:
                slot = {GRP_HQ: SLOT_HQ, GRP_HI: SLOT_HI, GRP_HG: SLOT_HG}[grp]
                oa_ref[0, slot + head] = ph.astype(BF16)
    vt = _nt_dot(wvt_ref[...], u)
    for head in range(DA_HEADS):
        for jt in range(tm // ATT_T):
            ovt_ref[0, head, jt] = vt[head * LANES:(head + 1) * LANES,
                                      jt * ATT_T:(jt + 1) * ATT_T].astype(BF16)


def _inproj(x, mod3, norm1_w, w_main_bf, w_vt_bf):
    bsz, t, d = x.shape
    n = w_main_bf.shape[1]
    tm = INPROJ_TM
    return pl.pallas_call(
        _inproj_kernel,
        grid=(bsz, t // tm),
        in_specs=[pl.BlockSpec((1, tm, d), lambda b, i: (b, i, 0)),
                  pl.BlockSpec((1, N_MOD, d), lambda b, i: (b, 0, 0)),
                  pl.BlockSpec((1, d), lambda b, i: (0, 0)),
                  pl.BlockSpec((d, n), lambda b, i: (0, 0), pipeline_mode=pl.Buffered(1)),
                  pl.BlockSpec(w_vt_bf.shape, lambda b, i: (0, 0), pipeline_mode=pl.Buffered(1))],
        out_specs=[pl.BlockSpec((1, N_SLABS, tm, LANES), lambda b, i: (b, 0, i, 0)),
                   pl.BlockSpec((1, HG_HEADS, tm, LANES), lambda b, i: (b, 0, i, 0)),
                   pl.BlockSpec((1, DA_HEADS, tm // ATT_T, LANES, ATT_T),
                                lambda b, i: (b, 0, i, 0, 0))],
        out_shape=[jax.ShapeDtypeStruct((bsz, N_SLABS, t, LANES), BF16),
                   jax.ShapeDtypeStruct((bsz, HG_HEADS, t, LANES), F32),
                   jax.ShapeDtypeStruct((bsz, DA_HEADS, t // ATT_T, LANES, ATT_T), BF16)],
        compiler_params=pltpu.CompilerParams(
            dimension_semantics=("arbitrary", "arbitrary"), vmem_limit_bytes=VMEM_LIMIT_BYTES),
        name="inproj",
    )(x, mod3, norm1_w.reshape(1, d), w_main_bf, w_vt_bf)


def _hgrn_diag_block(qb, kb, bb, vb):
    half = SUBLANES
    row = lax.broadcasted_iota(jnp.int32, (half, LANES), 0)
    q_lo, q_hi = qb[:half], qb[half:]
    b_lo, b_hi = bb[:half], bb[half:]
    o_lo = jnp.zeros((half, LANES), F32)
    o_hi = jnp.zeros((half, LANES), F32)
    for s in range(HGRN_SUB):
        ks, bs, vs = kb[s:s + 1], bb[s:s + 1], vb[s:s + 1]
        if s < half:
            rel = jnp.where(row >= s, b_lo - bs, NEG_BIG)
            col = jnp.sum(q_lo * ks * jnp.exp(rel), axis=-1, keepdims=True)
            o_lo = o_lo + col * vs
            col = jnp.sum(q_hi * ks * jnp.exp(b_hi - bs), axis=-1, keepdims=True)
            o_hi = o_hi + col * vs
        else:
            rel = jnp.where(row >= s - half, b_hi - bs, NEG_BIG)
            col = jnp.sum(q_hi * ks * jnp.exp(rel), axis=-1, keepdims=True)
            o_hi = o_hi + col * vs
    return jnp.concatenate([o_lo, o_hi], axis=0)


def _hgrn_kernel(lbl_ref, gw_ref, q_ref, f_ref, i_ref, g_ref, o_ref, st_ref):
    c_len, sub = HGRN_CHUNK, HGRN_SUB
    t_len = q_ref.shape[2]
    l = lbl_ref[...]
    e = jnp.exp(l - jnp.max(l, axis=0, keepdims=True))
    lb = jnp.sum(e[:LAYER + 1], axis=0, keepdims=True) / jnp.sum(e, axis=0, keepdims=True)
    gw = gw_ref[...]
    r_i = lax.broadcasted_iota(jnp.int32, (c_len, c_len), 0)
    c_i = lax.broadcasted_iota(jnp.int32, (c_len, c_len), 1)
    tri = (r_i >= c_i).astype(BF16)
    st_ref[...] = jnp.zeros_like(st_ref)

    def chunk(c, carry):
        r0 = pl.multiple_of(c * c_len, c_len)
        q = q_ref[0, 0, pl.ds(r0, c_len), :].astype(F32)
        v_bf = i_ref[0, 0, pl.ds(r0, c_len), :]
        v = v_bf.astype(F32)
        g = g_ref[0, 0, pl.ds(r0, c_len), :].astype(F32)
        f = lb + (1.0 - lb) * jax.nn.sigmoid(f_ref[0, 0, pl.ds(r0, c_len), :])
        logf = jnp.log(f)
        k = 1.0 - f
        h1 = logf.astype(BF16)
        r1 = logf - h1.astype(F32)
        h2 = r1.astype(BF16)
        h3 = (r1 - h2.astype(F32)).astype(BF16)
        cs = jnp.dot(tri, jnp.concatenate([h1, h2, h3], axis=1), preferred_element_type=F32)
        b = cs[:, :LANES] + cs[:, LANES:2 * LANES] + cs[:, 2 * LANES:]
        b_last = b[c_len - 1:c_len]

        st = st_ref[...]
        o = _nt_dot((q * jnp.exp(b)).astype(BF16), st.astype(BF16))

        a_rows = [jnp.zeros((sub, c_len), F32)]
        for i in range(1, c_len // sub):
            ref = b[i * sub - 1:i * sub]
            qt = (q[i * sub:(i + 1) * sub] * jnp.exp(b[i * sub:(i + 1) * sub] - ref)).astype(BF16)
            kt = k[:i * sub] * jnp.exp(ref - b[:i * sub])
            kt = jnp.concatenate([kt, jnp.zeros((c_len - i * sub, LANES), F32)], axis=0).astype(BF16)
            a_rows.append(_nt_dot(qt, kt))
        a = jnp.concatenate(a_rows, axis=0).astype(BF16)
        o = o + jnp.dot(a, v_bf, preferred_element_type=F32)
        o = o + jnp.concatenate(
            [_hgrn_diag_block(q[j * sub:(j + 1) * sub], k[j * sub:(j + 1) * sub],
                              b[j * sub:(j + 1) * sub], v[j * sub:(j + 1) * sub])
             for j in range(c_len // sub)], axis=0)

        y = o * _rms_scale(o) * gw
        o_ref[0, pl.ds(r0, c_len), :] = (y * (g * jax.nn.sigmoid(g))).astype(o_ref.dtype)

        k_end = (k * jnp.exp(b_last - b)).astype(BF16)
        st_ref[...] = st * jnp.exp(b_last) + jnp.dot(v.T.astype(BF16), k_end,
                                                      preferred_element_type=F32)
        return carry

    lax.fori_loop(0, t_len // c_len, chunk, 0)


def _hgrn(oa, of, lb_logits, gnorm_w):
    bsz, _, t, _ = oa.shape
    slab = lambda slot: pl.BlockSpec((1, 1, t, LANES), lambda b, h: (b, slot + h, 0, 0))
    return pl.pallas_call(
        _hgrn_kernel,
        grid=(bsz, HG_HEADS),
        in_specs=[pl.BlockSpec((lb_logits.shape[0], LANES), lambda b, h: (0, h)),
                  pl.BlockSpec((1, LANES), lambda b, h: (0, 0)),
                  slab(SLOT_HQ),
                  pl.BlockSpec((1, 1, t, LANES), lambda b, h: (b, h, 0, 0)),
                  slab(SLOT_HI), slab(SLOT_HG)],
        out_specs=pl.BlockSpec((1, t, LANES), lambda b, h: (b, 0, h)),
        out_shape=jax.ShapeDtypeStruct((bsz, t, HG_HEADS * LANES), BF16),
        scratch_shapes=[pltpu.VMEM((HG_DIM, HG_DIM), F32)],
        compiler_params=pltpu.CompilerParams(
            dimension_semantics=("arbitrary", "arbitrary"), vmem_limit_bytes=VMEM_LIMIT_BYTES),
        name="hgrn",
    )(lb_logits, gnorm_w.reshape(1, LANES), oa, of, oa, oa)


def _attn_kernel(lq1_ref, lk1_ref, lq2_ref, lk2_ref, sw_ref, q1_ref, q2_ref, k1_ref, k2_ref,
                 vt_ref, o_ref, s_ref, mx_ref):
    t = ATT_T
    nq = q1_ref.shape[2] // t
    lam = (jnp.exp(jnp.sum(lq1_ref[...] * lk1_ref[...], axis=-1, keepdims=True))
           - jnp.exp(jnp.sum(lq2_ref[...] * lk2_ref[...], axis=-1, keepdims=True))
           + LAMBDA_INIT)
    q_refs = (q1_ref, q2_ref)
    k_refs = (k1_ref, k2_ref)
    ones = jnp.ones((BF16_ROWS, t), BF16)
    causal = (lax.broadcasted_iota(jnp.int32, (t, t), 0)
              <= lax.broadcasted_iota(jnp.int32, (t, t), 1))
    tiles = [(qi, j) for qi in range(nq) for j in range(qi + 1)]
    n_buf = ATT_LOOKAHEAD + 1

    def key_max(s):
        quarter = t // 4
        parts = [s[i * quarter:(i + 1) * quarter] for i in range(4)]
        part = jnp.maximum(jnp.maximum(parts[0], parts[1]), jnp.maximum(parts[2], parts[3]))
        return jnp.max(part, axis=0, keepdims=True)

    def produce(n):
        qi, j = tiles[n]
        for mp in range(2):
            s = _nt_dot(k_refs[mp][0, 0, j * t:(j + 1) * t, :],
                        q_refs[mp][0, 0, qi * t:(qi + 1) * t, :])
            if j == qi:
                s = jnp.where(causal, s, NEG_BIG)
            s_ref[n % n_buf, mp] = s
            mx_ref[n % n_buf, mp] = key_max(s)

    def finalize(qi, stats):
        (_, acc1), (_, acc2) = stats
        o = (acc1[:LANES] * (1.0 / acc1[LANES:LANES + 1])
             - acc2[:LANES] * (lam / acc2[LANES:LANES + 1]))
        y = o * lax.rsqrt(jnp.mean(o * o, axis=0, keepdims=True) + EPS) * sw_ref[...]
        o_ref[0, qi * t:(qi + 1) * t, :] = (y * (1.0 - LAMBDA_INIT)).T.astype(o_ref.dtype)

    for n in range(min(ATT_LOOKAHEAD, len(tiles))):
        produce(n)
    stats = None
    for n, (qi, j) in enumerate(tiles):
        if n + ATT_LOOKAHEAD < len(tiles):
            produce(n + ATT_LOOKAHEAD)
        vta = jnp.concatenate([vt_ref[0, 0, j], ones], axis=0)
        new_stats = []
        for mp in range(2):
            s = s_ref[n % n_buf, mp]
            mx = mx_ref[n % n_buf, mp]
            if j == 0:
                m_new = mx
                acc = jnp.dot(vta, jnp.exp2(s - m_new).astype(BF16), preferred_element_type=F32)
            else:
                m_old, acc_old = stats[mp]
                m_new = jnp.maximum(m_old, mx)
                acc = (jnp.exp2(m_old - m_new) * acc_old
                       + jnp.dot(vta, jnp.exp2(s - m_new).astype(BF16),
                                 preferred_element_type=F32))
            new_stats.append((m_new, acc))
        stats = new_stats
        if j == qi:
            finalize(qi, stats)


def _attn(oa, ovt, lq1, lk1, lq2, lk2, subln_w):
    bsz, _, t, _ = oa.shape
    tq = ATT_T
    vec = lambda n: pl.BlockSpec((1, n), lambda b, h: (0, 0))
    slab = lambda slot: pl.BlockSpec((1, 1, t, LANES), lambda b, h: (b, slot + h, 0, 0))
    return pl.pallas_call(
        _attn_kernel,
        grid=(bsz, DA_HEADS),
        in_specs=[vec(DA_HEAD_DIM), vec(DA_HEAD_DIM), vec(DA_HEAD_DIM), vec(DA_HEAD_DIM),
                  pl.BlockSpec((2 * DA_HEAD_DIM, 1), lambda b, h: (0, 0)),
                  slab(SLOT_AQ1), slab(SLOT_AQ2), slab(SLOT_AK1), slab(SLOT_AK2),
                  pl.BlockSpec((1, 1, t // tq, LANES, tq), lambda b, h: (b, h, 0, 0, 0))],
        out_specs=pl.BlockSpec((1, t, LANES), lambda b, h: (b, 0, h)),
        out_shape=jax.ShapeDtypeStruct((bsz, t, DA_HEADS * LANES), BF16),
        scratch_shapes=[pltpu.VMEM((ATT_LOOKAHEAD + 1, 2, tq, tq), F32),
                        pltpu.VMEM((ATT_LOOKAHEAD + 1, 2, 1, tq), F32)],
        compiler_params=pltpu.CompilerParams(
            dimension_semantics=("arbitrary", "arbitrary"), vmem_limit_bytes=VMEM_LIMIT_BYTES),
        name="attn",
    )(lq1, lk1, lq2, lk2, subln_w.reshape(2 * DA_HEAD_DIM, 1), oa, oa, oa, oa, ovt)


def _ffn_kernel(x_ref, mod_ref, ohg_ref, oda_ref, wo_ref, n2_ref, wg_ref, wu_ref, wd_ref, fw_ref,
                o_ref):
    x = x_ref[0]
    m = mod_ref[0]
    hgw = ohg_ref.shape[2]
    mix = (jnp.dot(ohg_ref[0], wo_ref[:hgw], preferred_element_type=F32)
           + jnp.dot(oda_ref[0], wo_ref[hgw:], preferred_element_type=F32))
    hcur = x + m[2:3] * mix
    u = (hcur * _rms_scale(hcur) * n2_ref[...] * (1.0 + m[4:5]) + m[3:4]).astype(BF16)
    acc = jnp.zeros(x.shape, F32)
    for c in range(wg_ref.shape[1] // FFN_FC):
        sl = slice(c * FFN_FC, (c + 1) * FFN_FC)
        gate = jnp.dot(u, wg_ref[:, sl], preferred_element_type=F32)
        up = jnp.dot(u, wu_ref[:, sl], preferred_element_type=F32)
        act = (gate * jax.nn.sigmoid(gate) * up).astype(BF16)
        acc = acc + jnp.dot(act, wd_ref[sl, :], preferred_element_type=F32)
    hcur = hcur + m[5:6] * acc
    o_ref[0] = hcur * _rms_scale(hcur) * fw_ref[...]


def _ffn(x, mod3, o_hg, o_da, w_out_bf, norm2_w, wg_bf, wu_bf, wd_bf, final_w):
    bsz, t, d = x.shape
    tm = FFN_TM
    dff = wg_bf.shape[1]
    assert dff % FFN_FC == 0
    tile = lambda w: pl.BlockSpec((1, tm, w), lambda b, i: (b, i, 0))
    const = lambda r, c: pl.BlockSpec((r, c), lambda b, i: (0, 0), pipeline_mode=pl.Buffered(1))
    return pl.pallas_call(
        _ffn_kernel,
        grid=(bsz, t // tm),
        in_specs=[tile(d),
                  pl.BlockSpec((1, N_MOD, d), lambda b, i: (b, 0, 0)),
                  tile(o_hg.shape[2]), tile(o_da.shape[2]),
                  const(d, d), const(1, d), const(d, dff), const(d, dff), const(dff, d),
                  const(1, d)],
        out_specs=tile(d),
        out_shape=jax.ShapeDtypeStruct((bsz, t, d), F32),
        compiler_params=pltpu.CompilerParams(
            dimension_semantics=("arbitrary", "arbitrary"), vmem_limit_bytes=VMEM_LIMIT_BYTES),
        name="ffn",
    )(x, mod3, o_hg, o_da, w_out_bf, norm2_w.reshape(1, d), wg_bf, wu_bf, wd_bf,
      final_w.reshape(1, d))


def kernel(x, c, w_ada, b_ada, norm1_w, w_in, hgrn_lb_logits, hgrn_gnorm_w, diff_lambda_q1,
           diff_lambda_k1, diff_lambda_q2, diff_lambda_k2, diff_subln_w, w_out, norm2_w,
           w_ffn_gate, w_ffn_up, w_ffn_down, final_norm_w):
    bsz, t, d = x.shape
    assert w_ada.shape[0] == 1, "single-layer trunk"
    l = LAYER
    mod3 = _ada(c, w_ada[l], b_ada[l]).reshape(bsz, N_MOD, d)
    n_main = w_in.shape[2] - DA_HEADS * 2 * DA_HEAD_DIM
    oa, of, ovt = _inproj(x, mod3, norm1_w[l], w_in[l, :, :n_main].astype(BF16),
                          w_in[l, :, n_main:].T.astype(BF16))
    o_hg = _hgrn(oa, of, hgrn_lb_logits, hgrn_gnorm_w[l])
    o_da = _attn(oa, ovt, diff_lambda_q1[l:l + 1], diff_lambda_k1[l:l + 1],
                 diff_lambda_q2[l:l + 1], diff_lambda_k2[l:l + 1], diff_subln_w[l])
    return _ffn(x, mod3, o_hg, o_da, w_out[l].astype(BF16), norm2_w[l],
                w_ffn_gate[l].astype(BF16), w_ffn_up[l].astype(BF16),
                w_ffn_down[l].astype(BF16), final_norm_w)
```

```python
import math

import jax
import jax.numpy as jnp
from jax import lax
from jax.experimental import pallas as pl
from jax.experimental.pallas import tpu as pltpu

F32 = jnp.float32
BF16 = jnp.bfloat16

HG_HEADS = 4
HG_DIM = 128
DA_HEADS = 4
DA_HEAD_DIM = 64
N_MOD = 6
EPS = 1e-6
LAYER = 0
LAMBDA_INIT = 0.8 - 0.6 * math.exp(-0.3 * LAYER)
LOG2_E = math.log2(math.e)

LANES = 128
SUBLANES = 8
BF16_ROWS = 16
VMEM_LIMIT_BYTES = 56 * 1024 * 1024

HGRN_CHUNK = 64
HGRN_SUB = 16
INPROJ_TM = 512
FFN_TM = 512
FFN_FC = 256
ATT_T = 256
ATT_LOOKAHEAD = 2
NEG_BIG = -1e30
N_BIAS_PIECES = 3


def _nt_dot(a, b):
    return lax.dot_general(a, b, (((1,), (1,)), ((), ())), preferred_element_type=F32)


def _rms_scale(x):
    return lax.rsqrt(jnp.mean(x * x, axis=-1, keepdims=True) + EPS)


def _alibi_slope(head):
    return 2.0 ** (-8.0 * (head + 1) / DA_HEADS)


def _ada_kernel(c_ref, w_ref, b_ref, o_ref):
    c = c_ref[...]
    a = c * jax.nn.sigmoid(c)
    o_ref[...] = jnp.dot(a, w_ref[...], preferred_element_type=F32,
                         precision=lax.Precision.HIGHEST) + b_ref[...]


def _ada(c, w_ada, b_ada):
    bsz, d = c.shape
    n = w_ada.shape[1]
    tn = 1536
    return pl.pallas_call(
        _ada_kernel,
        grid=(n // tn,),
        in_specs=[pl.BlockSpec((bsz, d), lambda j: (0, 0)),
                  pl.BlockSpec((d, tn), lambda j: (0, j)),
                  pl.BlockSpec((1, tn), lambda j: (0, j))],
        out_specs=pl.BlockSpec((bsz, tn), lambda j: (0, j)),
        out_shape=jax.ShapeDtypeStruct((bsz, n), F32),
        compiler_params=pltpu.CompilerParams(
            dimension_semantics=("arbitrary",), vmem_limit_bytes=VMEM_LIMIT_BYTES),
        name="ada",
    )(c, w_ada, b_ada.reshape(1, n))


SLOT_HQ, SLOT_HI, SLOT_HG, SLOT_AQ1, SLOT_AQ2, SLOT_AK1, SLOT_AK2 = 0, 4, 8, 12, 16, 20, 24
N_SLABS = 28
GRP_HQ, GRP_HF, GRP_HI, GRP_HG, GRP_AQ, GRP_AK = range(6)


def _augment(ph, extra, first_map):
    lane = lax.broadcasted_iota(jnp.int32, ph.shape, 1)
    base = DA_HEAD_DIM if first_map else 0
    aug = jnp.zeros_like(ph)
    for n, e in enumerate(extra):
        aug = jnp.where(lane == base + n, e, aug)
    keep = (lane < DA_HEAD_DIM) if first_map else (lane >= DA_HEAD_DIM)
    return jnp.where(keep, ph, aug).astype(BF16)


def _inproj_kernel(x_ref, mod_ref, n1_ref, w_ref, wvt_ref, oa_ref, of_ref, ovt_ref):
    tm = x_ref.shape[1]
    x = x_ref[0]
    m = mod_ref[0]
    y = x * _rms_scale(x) * n1_ref[...]
    u = (y * (1.0 + m[1:2]) + m[0:1]).astype(BF16)
    pos = (pl.program_id(1) * tm + lax.broadcasted_iota(jnp.int32, (tm, LANES), 0)).astype(F32)
    ones = [jnp.ones((tm, LANES), F32)] * N_BIAS_PIECES
    for j in range(w_ref.shape[1] // (2 * LANES)):
        p = jnp.dot(u, w_ref[:, j * 2 * LANES:(j + 1) * 2 * LANES], preferred_element_type=F32)
        for half in range(2):
            ph = p[:, half * LANES:(half + 1) * LANES]
            grp, head = divmod(2 * j + half, 4)
            if grp == GRP_HF:
                of_ref[0, head] = ph
            elif grp == GRP_AQ:
                ph = ph * (LOG2_E * DA_HEAD_DIM ** -0.5)
                oa_ref[0, SLOT_AQ1 + head] = _augment(ph, ones, True)
                oa_ref[0, SLOT_AQ2 + head] = _augment(ph, ones, False)
            elif grp == GRP_AK:
                bias = (LOG2_E * _alibi_slope(head)) * pos
                pieces = []
                for _ in range(N_BIAS_PIECES):
                    piece = bias.astype(BF16).astype(F32)
                    pieces.append(piece)
                    bias = bias - piece
                oa_ref[0, SLOT_AK1 + head] = _augment(ph, pieces, True)
                oa_ref[0, SLOT_AK2 + head] = _augment(ph, pieces, False)
            else:
                slot = {GRP_HQ: SLOT_HQ, GRP_HI: SLOT_HI, GRP_HG: SLOT_HG}[grp]
                oa_ref[0, slot + head] = ph.astype(BF16)
    vt = _nt_dot(wvt_ref[...], u)
    for head in range(DA_HEADS):
        for jt in range(tm // ATT_T):
            ovt_ref[0, head, jt] = vt[head * LANES:(head + 1) * LANES,
                                      jt * ATT_T:(jt + 1) * ATT_T].astype(BF16)


def _inproj(x, mod3, norm1_w, w_main_bf, w_vt_bf):
    bsz, t, d = x.shape
    n = w_main_bf.shape[1]
    tm = INPROJ_TM
    return pl.pallas_call(
        _inproj_kernel,
        grid=(bsz, t // tm),
        in_specs=[pl.BlockSpec((1, tm, d), lambda b, i: (b, i, 0)),
                  pl.BlockSpec((1, N_MOD, d), lambda b, i: (b, 0, 0)),
                  pl.BlockSpec((1, d), lambda b, i: (0, 0)),
                  pl.BlockSpec((d, n), lambda b, i: (0, 0), pipeline_mode=pl.Buffered(1)),
                  pl.BlockSpec(w_vt_bf.shape, lambda b, i: (0, 0), pipeline_mode=pl.Buffered(1))],
        out_specs=[pl.BlockSpec((1, N_SLABS, tm, LANES), lambda b, i: (b, 0, i, 0)),
                   pl.BlockSpec((1, HG_HEADS, tm, LANES), lambda b, i: (b, 0, i, 0)),
                   pl.BlockSpec((1, DA_HEADS, tm // ATT_T, LANES, ATT_T),
                                lambda b, i: (b, 0, i, 0, 0))],
        out_shape=[jax.ShapeDtypeStruct((bsz, N_SLABS, t, LANES), BF16),
                   jax.ShapeDtypeStruct((bsz, HG_HEADS, t, LANES), F32),
                   jax.ShapeDtypeStruct((bsz, DA_HEADS, t // ATT_T, LANES, ATT_T), BF16)],
        compiler_params=pltpu.CompilerParams(
            dimension_semantics=("arbitrary", "arbitrary"), vmem_limit_bytes=VMEM_LIMIT_BYTES),
        name="inproj",
    )(x, mod3, norm1_w.reshape(1, d), w_main_bf, w_vt_bf)


def _hgrn_diag_scores(a_blk, qb, bb, c_rows, col0):
    half = SUBLANES
    row = lax.broadcasted_iota(jnp.int32, (half, LANES), 0)
    lane = lax.broadcasted_iota(jnp.int32, (half, a_blk.shape[1]), 1)
    q_lo, q_hi = qb[:half], qb[half:]
    b_lo, b_hi = bb[:half], bb[half:]
    a_lo, a_hi = a_blk[:half], a_blk[half:]
    for s in range(HGRN_SUB):
        cs = c_rows(s)
        here = lane == col0 + s
        if s < half:
            rel = jnp.where(row >= s, b_lo - cs, NEG_BIG)
            a_lo = jnp.where(here, jnp.sum(q_lo * jnp.exp2(rel), axis=-1, keepdims=True), a_lo)
            a_hi = jnp.where(here, jnp.sum(q_hi * jnp.exp2(b_hi - cs), axis=-1, keepdims=True), a_hi)
        else:
            rel = jnp.where(row >= s - half, b_hi - cs, NEG_BIG)
            a_hi = jnp.where(here, jnp.sum(q_hi * jnp.exp2(rel), axis=-1, keepdims=True), a_hi)
    return jnp.concatenate([a_lo, a_hi], axis=0)


def _hgrn_head_chunk(lb, gw, tri, q, f_logit, v_bf, g, st, c_ref):
    c_len, sub = HGRN_CHUNK, HGRN_SUB
    f = lb + (1.0 - lb) * jax.nn.sigmoid(f_logit)
    logf = jnp.log2(f)
    k = 1.0 - f
    h1 = logf.astype(BF16)
    r1 = logf - h1.astype(F32)
    h2 = r1.astype(BF16)
    h3 = (r1 - h2.astype(F32)).astype(BF16)
    cs = jnp.dot(tri, jnp.concatenate([h1, h2, h3], axis=1), preferred_element_type=F32)
    b = cs[:, :LANES] + cs[:, LANES:2 * LANES] + cs[:, 2 * LANES:]
    b_last = b[c_len - 1:c_len]
    c_ref[...] = b - jnp.log2(k)

    o = _nt_dot((q * jnp.exp2(b)).astype(BF16), st.astype(BF16))

    a_rows = []
    for i in range(c_len // sub):
        rows = slice(i * sub, (i + 1) * sub)
        if i == 0:
            a_blk = jnp.zeros((sub, c_len), F32)
        else:
            ref = b[i * sub - 1:i * sub]
            qt = (q[rows] * jnp.exp2(b[rows] - ref)).astype(BF16)
            kt = k[:i * sub] * jnp.exp2(ref - b[:i * sub])
            kt = jnp.concatenate([kt, jnp.zeros((c_len - i * sub, LANES), F32)], axis=0).astype(BF16)
            a_blk = _nt_dot(qt, kt)
        c_rows = lambda s, i=i: c_ref[pl.ds(i * sub + s, SUBLANES, stride=0), :]
        a_rows.append(_hgrn_diag_scores(a_blk, q[rows], b[rows], c_rows, i * sub))
    a = jnp.concatenate(a_rows, axis=0).astype(BF16)
    o = o + jnp.dot(a, v_bf, preferred_element_type=F32)

    y = o * _rms_scale(o) * gw
    out = (y * (g * jax.nn.sigmoid(g))).astype(BF16)

    k_end = (k * jnp.exp2(b_last - b)).astype(BF16)
    st_new = st * jnp.exp2(b_last) + jnp.dot(v_bf.astype(F32).T.astype(BF16), k_end,
                                              preferred_element_type=F32)
    return out, st_new


def _hgrn_kernel(lbl_ref, gw_ref, q_ref, f_ref, i_ref, g_ref, o_ref, st_ref, c_ref):
    c_len = HGRN_CHUNK
    n_heads, t_len = q_ref.shape[1], q_ref.shape[2]
    l = lbl_ref[...]
    e = jnp.exp(l - jnp.max(l, axis=0, keepdims=True))
    lb_all = jnp.sum(e[:LAYER + 1], axis=0, keepdims=True) / jnp.sum(e, axis=0, keepdims=True)
    gw = gw_ref[...]
    r_i = lax.broadcasted_iota(jnp.int32, (c_len, c_len), 0)
    c_i = lax.broadcasted_iota(jnp.int32, (c_len, c_len), 1)
    tri = (r_i >= c_i).astype(BF16)
    st_ref[...] = jnp.zeros_like(st_ref)

    def chunk(c, carry):
        r0 = pl.multiple_of(c * c_len, c_len)
        rows = pl.ds(r0, c_len)
        for h in range(n_heads):
            cols = slice(h * LANES, (h + 1) * LANES)
            out, st_new = _hgrn_head_chunk(
                lb_all[:, cols], gw, tri, q_ref[0, h, rows, :].astype(F32), f_ref[0, h, rows, :],
                i_ref[0, h, rows, :], g_ref[0, h, rows, :].astype(F32), st_ref[h], c_ref.at[h])
            o_ref[0, rows, cols] = out
            st_ref[h] = st_new
        return carry

    lax.fori_loop(0, t_len // c_len, chunk, 0)


def _hgrn(oa, of, lb_logits, gnorm_w):
    bsz, _, t, _ = oa.shape
    width = HG_HEADS * LANES
    slab = lambda slot: pl.BlockSpec((1, HG_HEADS, t, LANES), lambda b: (b, slot // HG_HEADS, 0, 0))
    return pl.pallas_call(
        _hgrn_kernel,
        grid=(bsz,),
        in_specs=[pl.BlockSpec((lb_logits.shape[0], width), lambda b: (0, 0)),
                  pl.BlockSpec((1, LANES), lambda b: (0, 0)),
                  slab(SLOT_HQ),
                  pl.BlockSpec((1, HG_HEADS, t, LANES), lambda b: (b, 0, 0, 0)),
                  slab(SLOT_HI), slab(SLOT_HG)],
        out_specs=pl.BlockSpec((1, t, width), lambda b: (b, 0, 0)),
        out_shape=jax.ShapeDtypeStruct((bsz, t, width), BF16),
        scratch_shapes=[pltpu.VMEM((HG_HEADS, HG_DIM, HG_DIM), F32),
                        pltpu.VMEM((HG_HEADS, HGRN_CHUNK, LANES), F32)],
        compiler_params=pltpu.CompilerParams(
            dimension_semantics=("arbitrary",), vmem_limit_bytes=VMEM_LIMIT_BYTES),
        name="hgrn",
    )(lb_logits, gnorm_w.reshape(1, LANES), oa, of, oa, oa)


def _attn_kernel(lq1_ref, lk1_ref, lq2_ref, lk2_ref, sw_ref, q1_ref, q2_ref, k1_ref, k2_ref,
                 vt_ref, o_ref, s_ref, mx_ref):
    t = ATT_T
    nq = q1_ref.shape[2] // t
    lam = (jnp.exp(jnp.sum(lq1_ref[...] * lk1_ref[...], axis=-1, keepdims=True))
           - jnp.exp(jnp.sum(lq2_ref[...] * lk2_ref[...], axis=-1, keepdims=True))
           + LAMBDA_INIT)
    q_refs = (q1_ref, q2_ref)
    k_refs = (k1_ref, k2_ref)
    ones = jnp.ones((BF16_ROWS, t), BF16)
    causal = (lax.broadcasted_iota(jnp.int32, (t, t), 0)
              <= lax.broadcasted_iota(jnp.int32, (t, t), 1))
    tiles = [(qi, j) for qi in range(nq) for j in range(qi + 1)]
    n_buf = ATT_LOOKAHEAD + 1

    def key_max(s):
        quarter = t // 4
        parts = [s[i * quarter:(i + 1) * quarter] for i in range(4)]
        part = jnp.maximum(jnp.maximum(parts[0], parts[1]), jnp.maximum(parts[2], parts[3]))
        return jnp.max(part, axis=0, keepdims=True)

    def produce(n):
        qi, j = tiles[n]
        for mp in range(2):
            s = _nt_dot(k_refs[mp][0, 0, j * t:(j + 1) * t, :],
                        q_refs[mp][0, 0, qi * t:(qi + 1) * t, :])
            if j == qi:
                s = jnp.where(causal, s, NEG_BIG)
            s_ref[n % n_buf, mp] = s
            mx_ref[n % n_buf, mp] = key_max(s)

    def finalize(qi, stats):
        (_, acc1), (_, acc2) = stats
        o = (acc1[:LANES] * (1.0 / acc1[LANES:LANES + 1])
             - acc2[:LANES] * (lam / acc2[LANES:LANES + 1]))
        y = o * lax.rsqrt(jnp.mean(o * o, axis=0, keepdims=True) + EPS) * sw_ref[...]
        o_ref[0, qi * t:(qi + 1) * t, :] = (y * (1.0 - LAMBDA_INIT)).T.astype(o_ref.dtype)

    for n in range(min(ATT_LOOKAHEAD, len(tiles))):
        produce(n)
    stats = None
    for n, (qi, j) in enumerate(tiles):
        if n + ATT_LOOKAHEAD < len(tiles):
            produce(n + ATT_LOOKAHEAD)
        vta = jnp.concatenate([vt_ref[0, 0, j], ones], axis=0)
        new_stats = []
        for mp in range(2):
            s = s_ref[n % n_buf, mp]
            mx = mx_ref[n % n_buf, mp]
            if j == 0:
                m_new = mx
                acc = jnp.dot(vta, jnp.exp2(s - m_new).astype(BF16), preferred_element_type=F32)
            else:
                m_old, acc_old = stats[mp]
                m_new = jnp.maximum(m_old, mx)
                acc = (jnp.exp2(m_old - m_new) * acc_old
                       + jnp.dot(vta, jnp.exp2(s - m_new).astype(BF16),
                                 preferred_element_type=F32))
            new_stats.append((m_new, acc))
        stats = new_stats
        if j == qi:
            finalize(qi, stats)


def _attn(oa, ovt, lq1, lk1, lq2, lk2, subln_w):
    bsz, _, t, _ = oa.shape
    tq = ATT_T
    vec = lambda n: pl.BlockSpec((1, n), lambda b, h: (0, 0))
    slab = lambda slot: pl.BlockSpec((1, 1, t, LANES), lambda b, h: (b, slot + h, 0, 0))
    return pl.pallas_call(
        _attn_kernel,
        grid=(bsz, DA_HEADS),
        in_specs=[vec(DA_HEAD_DIM), vec(DA_HEAD_DIM), vec(DA_HEAD_DIM), vec(DA_HEAD_DIM),
                  pl.BlockSpec((2 * DA_HEAD_DIM, 1), lambda b, h: (0, 0)),
                  slab(SLOT_AQ1), slab(SLOT_AQ2), slab(SLOT_AK1), slab(SLOT_AK2),
                  pl.BlockSpec((1, 1, t // tq, LANES, tq), lambda b, h: (b, h, 0, 0, 0))],
        out_specs=pl.BlockSpec((1, t, LANES), lambda b, h: (b, 0, h)),
        out_shape=jax.ShapeDtypeStruct((bsz, t, DA_HEADS * LANES), BF16),
        scratch_shapes=[pltpu.VMEM((ATT_LOOKAHEAD + 1, 2, tq, tq), F32),
                        pltpu.VMEM((ATT_LOOKAHEAD + 1, 2, 1, tq), F32)],
        compiler_params=pltpu.CompilerParams(
            dimension_semantics=("arbitrary", "arbitrary"), vmem_limit_bytes=VMEM_LIMIT_BYTES),
        name="attn",
    )(lq1, lk1, lq2, lk2, subln_w.reshape(2 * DA_HEAD_DIM, 1), oa, oa, oa, oa, ovt)


def _ffn_kernel(x_ref, mod_ref, ohg_ref, oda_ref, wo_ref, n2_ref, wg_ref, wu_ref, wd_ref, fw_ref,
                o_ref):
    x = x_ref[0]
    m = mod_ref[0]
    hgw = ohg_ref.shape[2]
    mix = (jnp.dot(ohg_ref[0], wo_ref[:hgw], preferred_element_type=F32)
           + jnp.dot(oda_ref[0], wo_ref[hgw:], preferred_element_type=F32))
    hcur = x + m[2:3] * mix
    u = (hcur * _rms_scale(hcur) * n2_ref[...] * (1.0 + m[4:5]) + m[3:4]).astype(BF16)
    acc = jnp.zeros(x.shape, F32)
    for c in range(wg_ref.shape[1] // FFN_FC):
        sl = slice(c * FFN_FC, (c + 1) * FFN_FC)
        gate = jnp.dot(u, wg_ref[:, sl], preferred_element_type=F32)
        up = jnp.dot(u, wu_ref[:, sl], preferred_element_type=F32)
        act = (gate * jax.nn.sigmoid(gate) * up).astype(BF16)
        acc = acc + jnp.dot(act, wd_ref[sl, :], preferred_element_type=F32)
    hcur = hcur + m[5:6] * acc
    o_ref[0] = hcur * _rms_scale(hcur) * fw_ref[...]


def _ffn(x, mod3, o_hg, o_da, w_out_bf, norm2_w, wg_bf, wu_bf, wd_bf, final_w):
    bsz, t, d = x.shape
    tm = FFN_TM
    dff = wg_bf.shape[1]
    assert dff % FFN_FC == 0
    tile = lambda w: pl.BlockSpec((1, tm, w), lambda b, i: (b, i, 0))
    const = lambda r, c: pl.BlockSpec((r, c), lambda b, i: (0, 0), pipeline_mode=pl.Buffered(1))
    return pl.pallas_call(
        _ffn_kernel,
        grid=(bsz, t // tm),
        in_specs=[tile(d),
                  pl.BlockSpec((1, N_MOD, d), lambda b, i: (b, 0, 0)),
                  tile(o_hg.shape[2]), tile(o_da.shape[2]),
                  const(d, d), const(1, d), const(d, dff), const(d, dff), const(dff, d),
                  const(1, d)],
        out_specs=tile(d),
        out_shape=jax.ShapeDtypeStruct((bsz, t, d), F32),
        compiler_params=pltpu.CompilerParams(
            dimension_semantics=("arbitrary", "arbitrary"), vmem_limit_bytes=VMEM_LIMIT_BYTES),
        name="ffn",
    )(x, mod3, o_hg, o_da, w_out_bf, norm2_w.reshape(1, d), wg_bf, wu_bf, wd_bf,
      final_w.reshape(1, d))


def kernel(x, c, w_ada, b_ada, norm1_w, w_in, hgrn_lb_logits, hgrn_gnorm_w, diff_lambda_q1,
           diff_lambda_k1, diff_lambda_q2, diff_lambda_k2, diff_subln_w, w_out, norm2_w,
           w_ffn_gate, w_ffn_up, w_ffn_down, final_norm_w):
    bsz, t, d = x.shape
    assert w_ada.shape[0] == 1, "single-layer trunk"
    l = LAYER
    mod3 = _ada(c, w_ada[l], b_ada[l]).reshape(bsz, N_MOD, d)
    n_main = w_in.shape[2] - DA_HEADS * 2 * DA_HEAD_DIM
    oa, of, ovt = _inproj(x, mod3, norm1_w[l], w_in[l, :, :n_main].astype(BF16),
                          w_in[l, :, n_main:].T.astype(BF16))
    o_hg = _hgrn(oa, of, hgrn_lb_logits, hgrn_gnorm_w[l])
    o_da = _attn(oa, ovt, diff_lambda_q1[l:l + 1], diff_lambda_k1[l:l + 1],
                 diff_lambda_q2[l:l + 1], diff_lambda_k2[l:l + 1], diff_subln_w[l])
    return _ffn(x, mod3, o_hg, o_da, w_out[l].astype(BF16), norm2_w[l],
                w_ffn_gate[l].astype(BF16), w_ffn_up[l].astype(BF16),
                w_ffn_down[l].astype(BF16), final_norm_w)
```

```python
import math

import jax
import jax.numpy as jnp
from jax import lax
from jax.experimental import pallas as pl
from jax.experimental.pallas import tpu as pltpu

F32 = jnp.float32
BF16 = jnp.bfloat16

HG_HEADS = 4
HG_DIM = 128
DA_HEADS = 4
DA_HEAD_DIM = 64
N_MOD = 6
EPS = 1e-6
LAYER = 0
LAMBDA_INIT = 0.8 - 0.6 * math.exp(-0.3 * LAYER)
LOG2_E = math.log2(math.e)

LANES = 128
SUBLANES = 8
BF16_ROWS = 16
VMEM_LIMIT_BYTES = 56 * 1024 * 1024

HGRN_CHUNK = 64
HGRN_SUB = 16
INPROJ_TM = 512
FFN_TM = 512
FFN_FC = 256
ATT_T = 256
ATT_LOOKAHEAD = 2
NEG_BIG = -1e30
N_BIAS_PIECES = 3


def _nt_dot(a, b):
    return lax.dot_general(a, b, (((1,), (1,)), ((), ())), preferred_element_type=F32)


def _rms_scale(x):
    return lax.rsqrt(jnp.mean(x * x, axis=-1, keepdims=True) + EPS)


def _alibi_slope(head):
    return 2.0 ** (-8.0 * (head + 1) / DA_HEADS)


def _ada_kernel(c_ref, w_ref, b_ref, o_ref):
    c = c_ref[...]
    a = c * jax.nn.sigmoid(c)
    o_ref[...] = jnp.dot(a, w_ref[...], preferred_element_type=F32,
                         precision=lax.Precision.HIGHEST) + b_ref[...]


def _ada(c, w_ada, b_ada):
    bsz, d = c.shape
    n = w_ada.shape[1]
    tn = 1536
    return pl.pallas_call(
        _ada_kernel,
        grid=(n // tn,),
        in_specs=[pl.BlockSpec((bsz, d), lambda j: (0, 0)),
                  pl.BlockSpec((d, tn), lambda j: (0, j)),
                  pl.BlockSpec((1, tn), lambda j: (0, j))],
        out_specs=pl.BlockSpec((bsz, tn), lambda j: (0, j)),
        out_shape=jax.ShapeDtypeStruct((bsz, n), F32),
        compiler_params=pltpu.CompilerParams(
            dimension_semantics=("arbitrary",), vmem_limit_bytes=VMEM_LIMIT_BYTES),
        name="ada",
    )(c, w_ada, b_ada.reshape(1, n))


SLOT_HQ, SLOT_HI, SLOT_HG, SLOT_AQ1, SLOT_AQ2, SLOT_AK1, SLOT_AK2 = 0, 4, 8, 12, 16, 20, 24
N_SLABS = 28
GRP_HQ, GRP_HF, GRP_HI, GRP_HG, GRP_AQ, GRP_AK = range(6)


def _augment(ph, extra, first_map):
    lane = lax.broadcasted_iota(jnp.int32, ph.shape, 1)
    base = DA_HEAD_DIM if first_map else 0
    aug = jnp.zeros_like(ph)
    for n, e in enumerate(extra):
        aug = jnp.where(lane == base + n, e, aug)
    keep = (lane < DA_HEAD_DIM) if first_map else (lane >= DA_HEAD_DIM)
    return jnp.where(keep, ph, aug).astype(BF16)


def _inproj_kernel(x_ref, mod_ref, n1_ref, w_ref, wvt_ref, oa_ref, of_ref, ovt_ref):
    tm = x_ref.shape[1]
    x = x_ref[0]
    m = mod_ref[0]
    y = x * _rms_scale(x) * n1_ref[...]
    u = (y * (1.0 + m[1:2]) + m[0:1]).astype(BF16)
    pos = (pl.program_id(1) * tm + lax.broadcasted_iota(jnp.int32, (tm, LANES), 0)).astype(F32)
    ones = [jnp.ones((tm, LANES), F32)] * N_BIAS_PIECES
    for j in range(w_ref.shape[1] // (2 * LANES)):
        p = jnp.dot(u, w_ref[:, j * 2 * LANES:(j + 1) * 2 * LANES], preferred_element_type=F32)
        for half in range(2):
            ph = p[:, half * LANES:(half + 1) * LANES]
            grp, head = divmod(2 * j + half, 4)
            if grp == GRP_HF:
                of_ref[0, head] = ph
            elif grp == GRP_AQ:
                ph = ph * (LOG2_E * DA_HEAD_DIM ** -0.5)
                oa_ref[0, SLOT_AQ1 + head] = _augment(ph, ones, True)
                oa_ref[0, SLOT_AQ2 + head] = _augment(ph, ones, False)
            elif grp == GRP_AK:
                bias = (LOG2_E * _alibi_slope(head)) * pos
                pieces = []
                for _ in range(N_BIAS_PIECES):
                    piece = bias.astype(BF16).astype(F32)
                    pieces.append(piece)
                    bias = bias - piece
                oa_ref[0, SLOT_AK1 + head] = _augment(ph, pieces, True)
                oa_ref[0, SLOT_AK2 + head] = _augment(ph, pieces, False)
            else:
                slot = {GRP_HQ: SLOT_HQ, GRP_HI: SLOT_HI, GRP_HG: SLOT_HG}[grp]
                oa_ref[0, slot + head] = ph.astype(BF16)
    vt = _nt_dot(wvt_ref[...], u)
    for head in range(DA_HEADS):
        for jt in range(tm // ATT_T):
            ovt_ref[0, head, jt] = vt[head * LANES:(head + 1) * LANES,
                                      jt * ATT_T:(jt + 1) * ATT_T].astype(BF16)


def _inproj(x, mod3, norm1_w, w_main_bf, w_vt_bf):
    bsz, t, d = x.shape
    n = w_main_bf.shape[1]
    tm = INPROJ_TM
    return pl.pallas_call(
        _inproj_kernel,
        grid=(bsz, t // tm),
        in_specs=[pl.BlockSpec((1, tm, d), lambda b, i: (b, i, 0)),
                  pl.BlockSpec((1, N_MOD, d), lambda b, i: (b, 0, 0)),
                  pl.BlockSpec((1, d), lambda b, i: (0, 0)),
                  pl.BlockSpec((d, n), lambda b, i: (0, 0), pipeline_mode=pl.Buffered(1)),
                  pl.BlockSpec(w_vt_bf.shape, lambda b, i: (0, 0), pipeline_mode=pl.Buffered(1))],
        out_specs=[pl.BlockSpec((1, N_SLABS, tm, LANES), lambda b, i: (b, 0, i, 0)),
                   pl.BlockSpec((1, HG_HEADS, tm, LANES), lambda b, i: (b, 0, i, 0)),
                   pl.BlockSpec((1, DA_HEADS, tm // ATT_T, LANES, ATT_T),
                                lambda b, i: (b, 0, i, 0, 0))],
        out_shape=[jax.ShapeDtypeStruct((bsz, N_SLABS, t, LANES), BF16),
                   jax.ShapeDtypeStruct((bsz, HG_HEADS, t, LANES), F32),
                   jax.ShapeDtypeStruct((bsz, DA_HEADS, t // ATT_T, LANES, ATT_T), BF16)],
        compiler_params=pltpu.CompilerParams(
            dimension_semantics=("arbitrary", "arbitrary"), vmem_limit_bytes=VMEM_LIMIT_BYTES),
        name="inproj",
    )(x, mod3, norm1_w.reshape(1, d), w_main_bf, w_vt_bf)


def _hgrn_diag_scores(a_blk, qb, bb, c_rows, col0):
    half = SUBLANES
    row = lax.broadcasted_iota(jnp.int32, (half, LANES), 0)
    lane = lax.broadcasted_iota(jnp.int32, (half, a_blk.shape[1]), 1)
    q_lo, q_hi = qb[:half], qb[half:]
    b_lo, b_hi = bb[:half], bb[half:]
    a_lo, a_hi = a_blk[:half], a_blk[half:]
    for s in range(HGRN_SUB):
        cs = c_rows(s)
        here = lane == col0 + s
        if s < half:
            rel = jnp.where(row >= s, b_lo - cs, NEG_BIG)
            a_lo = jnp.where(here, jnp.sum(q_lo * jnp.exp2(rel), axis=-1, keepdims=True), a_lo)
            a_hi = jnp.where(here, jnp.sum(q_hi * jnp.exp2(b_hi - cs), axis=-1, keepdims=True), a_hi)
        else:
            rel = jnp.where(row >= s - half, b_hi - cs, NEG_BIG)
            a_hi = jnp.where(here, jnp.sum(q_hi * jnp.exp2(rel), axis=-1, keepdims=True), a_hi)
    return jnp.concatenate([a_lo, a_hi], axis=0)


def _hgrn_decays(lb, tri, f_logit, bkc_ref):
    f = lb + (1.0 - lb) * jax.nn.sigmoid(f_logit)
    logf = jnp.log2(f)
    k = 1.0 - f
    h1 = logf.astype(BF16)
    r1 = logf - h1.astype(F32)
    h2 = r1.astype(BF16)
    h3 = (r1 - h2.astype(F32)).astype(BF16)
    cs = jnp.dot(tri, jnp.concatenate([h1, h2, h3], axis=1), preferred_element_type=F32)
    b = cs[:, :LANES] + cs[:, LANES:2 * LANES] + cs[:, 2 * LANES:]
    bkc_ref[0] = b
    bkc_ref[1] = k
    bkc_ref[2] = b - jnp.log2(k)


def _hgrn_intra(q, v_bf, bkc_ref, oin_ref, qdec_ref, kend_ref, dec_ref):
    c_len, sub = HGRN_CHUNK, HGRN_SUB
    b, k = bkc_ref[0], bkc_ref[1]
    b_last = b[c_len - 1:c_len]
    a_rows = []
    for i in range(c_len // sub):
        rows = slice(i * sub, (i + 1) * sub)
        if i == 0:
            a_blk = jnp.zeros((sub, c_len), F32)
        else:
            ref = b[i * sub - 1:i * sub]
            qt = (q[rows] * jnp.exp2(b[rows] - ref)).astype(BF16)
            kt = k[:i * sub] * jnp.exp2(ref - b[:i * sub])
            kt = jnp.concatenate([kt, jnp.zeros((c_len - i * sub, LANES), F32)], axis=0).astype(BF16)
            a_blk = _nt_dot(qt, kt)
        c_rows = lambda s, i=i: bkc_ref[2, pl.ds(i * sub + s, SUBLANES, stride=0), :]
        a_rows.append(_hgrn_diag_scores(a_blk, q[rows], b[rows], c_rows, i * sub))
    a = jnp.concatenate(a_rows, axis=0).astype(BF16)
    oin_ref[...] = jnp.dot(a, v_bf, preferred_element_type=F32)
    qdec_ref[...] = (q * jnp.exp2(b)).astype(BF16)
    kend_ref[...] = (k * jnp.exp2(b_last - b)).astype(BF16)
    dec_ref[...] = jnp.exp2(b_last)


def _hgrn_state(gw, v_bf, g, st, oin_ref, qdec_ref, kend_ref, dec_ref):
    o = oin_ref[...] + _nt_dot(qdec_ref[...], st.astype(BF16))
    y = o * _rms_scale(o) * gw
    out = (y * (g * jax.nn.sigmoid(g))).astype(BF16)
    st_new = st * dec_ref[...] + jnp.dot(v_bf.astype(F32).T.astype(BF16), kend_ref[...],
                                         preferred_element_type=F32)
    return out, st_new


def _hgrn_kernel(lbl_ref, gw_ref, q_ref, f_ref, i_ref, g_ref, o_ref, st_ref, bkc_ref, oin_ref,
                 qdec_ref, kend_ref, dec_ref):
    c_len = HGRN_CHUNK
    n_heads, t_len = q_ref.shape[1], q_ref.shape[2]
    n_chunks = t_len // c_len
    assert n_chunks % 2 == 0
    l = lbl_ref[...]
    e = jnp.exp(l - jnp.max(l, axis=0, keepdims=True))
    lb_all = jnp.sum(e[:LAYER + 1], axis=0, keepdims=True) / jnp.sum(e, axis=0, keepdims=True)
    gw = gw_ref[...]
    r_i = lax.broadcasted_iota(jnp.int32, (c_len, c_len), 0)
    c_i = lax.broadcasted_iota(jnp.int32, (c_len, c_len), 1)
    tri = (r_i >= c_i).astype(BF16)
    st_ref[...] = jnp.zeros_like(st_ref)

    def rows_of(c):
        c = jnp.minimum(c, n_chunks - 1)
        return pl.ds(pl.multiple_of(c * c_len, c_len), c_len)

    def stage0(c, par):
        rows = rows_of(c)
        for h in range(n_heads):
            _hgrn_decays(lb_all[:, h * LANES:(h + 1) * LANES], tri, f_ref[0, h, rows, :],
                         bkc_ref.at[par, h])

    def stage1(c, par):
        rows = rows_of(c)
        for h in range(n_heads):
            _hgrn_intra(q_ref[0, h, rows, :].astype(F32), i_ref[0, h, rows, :], bkc_ref.at[par, h],
                        oin_ref.at[par, h], qdec_ref.at[par, h], kend_ref.at[par, h],
                        dec_ref.at[par, h])

    def stage2(c, par):
        rows = rows_of(c)
        for h in range(n_heads):
            out, st_new = _hgrn_state(gw, i_ref[0, h, rows, :], g_ref[0, h, rows, :].astype(F32),
                                      st_ref[h], oin_ref.at[par, h], qdec_ref.at[par, h],
                                      kend_ref.at[par, h], dec_ref.at[par, h])
            o_ref[0, rows, h * LANES:(h + 1) * LANES] = out
            st_ref[h] = st_new

    stage0(0, 0)
    stage1(0, 0)
    stage0(1, 1)

    def pair(p, carry):
        c = 2 * p
        stage2(c, 0)
        stage1(c + 1, 1)
        stage0(c + 2, 0)
        stage2(c + 1, 1)
        stage1(c + 2, 0)
        stage0(c + 3, 1)
        return carry

    lax.fori_loop(0, n_chunks // 2, pair, 0)


def _hgrn(oa, of, lb_logits, gnorm_w):
    bsz, _, t, _ = oa.shape
    width = HG_HEADS * LANES
    slab = lambda slot: pl.BlockSpec((1, HG_HEADS, t, LANES), lambda b: (b, slot // HG_HEADS, 0, 0))
    return pl.pallas_call(
        _hgrn_kernel,
        grid=(bsz,),
        in_specs=[pl.BlockSpec((lb_logits.shape[0], width), lambda b: (0, 0)),
                  pl.BlockSpec((1, LANES), lambda b: (0, 0)),
                  slab(SLOT_HQ),
                  pl.BlockSpec((1, HG_HEADS, t, LANES), lambda b: (b, 0, 0, 0)),
                  slab(SLOT_HI), slab(SLOT_HG)],
        out_specs=pl.BlockSpec((1, t, width), lambda b: (b, 0, 0)),
        out_shape=jax.ShapeDtypeStruct((bsz, t, width), BF16),
        scratch_shapes=[pltpu.VMEM((HG_HEADS, HG_DIM, HG_DIM), F32),
                        pltpu.VMEM((2, HG_HEADS, 3, HGRN_CHUNK, LANES), F32),
                        pltpu.VMEM((2, HG_HEADS, HGRN_CHUNK, LANES), F32),
                        pltpu.VMEM((2, HG_HEADS, HGRN_CHUNK, LANES), BF16),
                        pltpu.VMEM((2, HG_HEADS, HGRN_CHUNK, LANES), BF16),
                        pltpu.VMEM((2, HG_HEADS, 1, LANES), F32)],
        compiler_params=pltpu.CompilerParams(
            dimension_semantics=("arbitrary",), vmem_limit_bytes=VMEM_LIMIT_BYTES),
        name="hgrn",
    )(lb_logits, gnorm_w.reshape(1, LANES), oa, of, oa, oa)


def _attn_kernel(lq1_ref, lk1_ref, lq2_ref, lk2_ref, sw_ref, q1_ref, q2_ref, k1_ref, k2_ref,
                 vt_ref, o_ref, s_ref, mx_ref):
    t = ATT_T
    nq = q1_ref.shape[2] // t
    lam = (jnp.exp(jnp.sum(lq1_ref[...] * lk1_ref[...], axis=-1, keepdims=True))
           - jnp.exp(jnp.sum(lq2_ref[...] * lk2_ref[...], axis=-1, keepdims=True))
           + LAMBDA_INIT)
    q_refs = (q1_ref, q2_ref)
    k_refs = (k1_ref, k2_ref)
    ones = jnp.ones((BF16_ROWS, t), BF16)
    causal = (lax.broadcasted_iota(jnp.int32, (t, t), 0)
              <= lax.broadcasted_iota(jnp.int32, (t, t), 1))
    tiles = [(qi, j) for qi in range(nq) for j in range(qi + 1)]
    n_buf = ATT_LOOKAHEAD + 1

    def key_max(s):
        quarter = t // 4
        parts = [s[i * quarter:(i + 1) * quarter] for i in range(4)]
        part = jnp.maximum(jnp.maximum(parts[0], parts[1]), jnp.maximum(parts[2], parts[3]))
        return jnp.max(part, axis=0, keepdims=True)

    def produce(n):
        qi, j = tiles[n]
        for mp in range(2):
            s = _nt_dot(k_refs[mp][0, 0, j * t:(j + 1) * t, :],
                        q_refs[mp][0, 0, qi * t:(qi + 1) * t, :])
            if j == qi:
                s = jnp.where(causal, s, NEG_BIG)
            s_ref[n % n_buf, mp] = s
            mx_ref[n % n_buf, mp] = key_max(s)

    def finalize(qi, stats):
        (_, acc1), (_, acc2) = stats
        o = (acc1[:LANES] * (1.0 / acc1[LANES:LANES + 1])
             - acc2[:LANES] * (lam / acc2[LANES:LANES + 1]))
        y = o * lax.rsqrt(jnp.mean(o * o, axis=0, keepdims=True) + EPS) * sw_ref[...]
        o_ref[0, qi * t:(qi + 1) * t, :] = (y * (1.0 - LAMBDA_INIT)).T.astype(o_ref.dtype)

    for n in range(min(ATT_LOOKAHEAD, len(tiles))):
        produce(n)
    stats = None
    for n, (qi, j) in enumerate(tiles):
        if n + ATT_LOOKAHEAD < len(tiles):
            produce(n + ATT_LOOKAHEAD)
        vta = jnp.concatenate([vt_ref[0, 0, j], ones], axis=0)
        new_stats = []
        for mp in range(2):
            s = s_ref[n % n_buf, mp]
            mx = mx_ref[n % n_buf, mp]
            if j == 0:
                m_new = mx
                acc = jnp.dot(vta, jnp.exp2(s - m_new).astype(BF16), preferred_element_type=F32)
            else:
                m_old, acc_old = stats[mp]
                m_new = jnp.maximum(m_old, mx)
                acc = (jnp.exp2(m_old - m_new) * acc_old
                       + jnp.dot(vta, jnp.exp2(s - m_new).astype(BF16),
                                 preferred_element_type=F32))
            new_stats.append((m_new, acc))
        stats = new_stats
        if j == qi:
            finalize(qi, stats)


def _attn(oa, ovt, lq1, lk1, lq2, lk2, subln_w):
    bsz, _, t, _ = oa.shape
    tq = ATT_T
    vec = lambda n: pl.BlockSpec((1, n), lambda b, h: (0, 0))
    slab = lambda slot: pl.BlockSpec((1, 1, t, LANES), lambda b, h: (b, slot + h, 0, 0))
    return pl.pallas_call(
        _attn_kernel,
        grid=(bsz, DA_HEADS),
        in_specs=[vec(DA_HEAD_DIM), vec(DA_HEAD_DIM), vec(DA_HEAD_DIM), vec(DA_HEAD_DIM),
                  pl.BlockSpec((2 * DA_HEAD_DIM, 1), lambda b, h: (0, 0)),
                  slab(SLOT_AQ1), slab(SLOT_AQ2), slab(SLOT_AK1), slab(SLOT_AK2),
                  pl.BlockSpec((1, 1, t // tq, LANES, tq), lambda b, h: (b, h, 0, 0, 0))],
        out_specs=pl.BlockSpec((1, t, LANES), lambda b, h: (b, 0, h)),
        out_shape=jax.ShapeDtypeStruct((bsz, t, DA_HEADS * LANES), BF16),
        scratch_shapes=[pltpu.VMEM((ATT_LOOKAHEAD + 1, 2, tq, tq), F32),
                        pltpu.VMEM((ATT_LOOKAHEAD + 1, 2, 1, tq), F32)],
        compiler_params=pltpu.CompilerParams(
            dimension_semantics=("arbitrary", "arbitrary"), vmem_limit_bytes=VMEM_LIMIT_BYTES),
        name="attn",
    )(lq1, lk1, lq2, lk2, subln_w.reshape(2 * DA_HEAD_DIM, 1), oa, oa, oa, oa, ovt)


def _ffn_kernel(x_ref, mod_ref, ohg_ref, oda_ref, wo_ref, n2_ref, wg_ref, wu_ref, wd_ref, fw_ref,
                o_ref):
    x = x_ref[0]
    m = mod_ref[0]
    hgw = ohg_ref.shape[2]
    mix = (jnp.dot(ohg_ref[0], wo_ref[:hgw], preferred_element_type=F32)
           + jnp.dot(oda_ref[0], wo_ref[hgw:], preferred_element_type=F32))
    hcur = x + m[2:3] * mix
    u = (hcur * _rms_scale(hcur) * n2_ref[...] * (1.0 + m[4:5]) + m[3:4]).astype(BF16)
    acc = jnp.zeros(x.shape, F32)
    for c in range(wg_ref.shape[1] // FFN_FC):
        sl = slice(c * FFN_FC, (c + 1) * FFN_FC)
        gate = jnp.dot(u, wg_ref[:, sl], preferred_element_type=F32)
        up = jnp.dot(u, wu_ref[:, sl], preferred_element_type=F32)
        act = (gate * jax.nn.sigmoid(gate) * up).astype(BF16)
        acc = acc + jnp.dot(act, wd_ref[sl, :], preferred_element_type=F32)
    hcur = hcur + m[5:6] * acc
    o_ref[0] = hcur * _rms_scale(hcur) * fw_ref[...]


def _ffn(x, mod3, o_hg, o_da, w_out_bf, norm2_w, wg_bf, wu_bf, wd_bf, final_w):
    bsz, t, d = x.shape
    tm = FFN_TM
    dff = wg_bf.shape[1]
    assert dff % FFN_FC == 0
    tile = lambda w: pl.BlockSpec((1, tm, w), lambda b, i: (b, i, 0))
    const = lambda r, c: pl.BlockSpec((r, c), lambda b, i: (0, 0), pipeline_mode=pl.Buffered(1))
    return pl.pallas_call(
        _ffn_kernel,
        grid=(bsz, t // tm),
        in_specs=[tile(d),
                  pl.BlockSpec((1, N_MOD, d), lambda b, i: (b, 0, 0)),
                  tile(o_hg.shape[2]), tile(o_da.shape[2]),
                  const(d, d), const(1, d), const(d, dff), const(d, dff), const(dff, d),
                  const(1, d)],
        out_specs=tile(d),
        out_shape=jax.ShapeDtypeStruct((bsz, t, d), F32),
        compiler_params=pltpu.CompilerParams(
            dimension_semantics=("arbitrary", "arbitrary"), vmem_limit_bytes=VMEM_LIMIT_BYTES),
        name="ffn",
    )(x, mod3, o_hg, o_da, w_out_bf, norm2_w.reshape(1, d), wg_bf, wu_bf, wd_bf,
      final_w.reshape(1, d))


def kernel(x, c, w_ada, b_ada, norm1_w, w_in, hgrn_lb_logits, hgrn_gnorm_w, diff_lambda_q1,
           diff_lambda_k1, diff_lambda_q2, diff_lambda_k2, diff_subln_w, w_out, norm2_w,
           w_ffn_gate, w_ffn_up, w_ffn_down, final_norm_w):
    bsz, t, d = x.shape
    assert w_ada.shape[0] == 1, "single-layer trunk"
    l = LAYER
    mod3 = _ada(c, w_ada[l], b_ada[l]).reshape(bsz, N_MOD, d)
    n_main = w_in.shape[2] - DA_HEADS * 2 * DA_HEAD_DIM
    oa, of, ovt = _inproj(x, mod3, norm1_w[l], w_in[l, :, :n_main].astype(BF16),
                          w_in[l, :, n_main:].T.astype(BF16))
    o_hg = _hgrn(oa, of, hgrn_lb_logits, hgrn_gnorm_w[l])
    o_da = _attn(oa, ovt, diff_lambda_q1[l:l + 1], diff_lambda_k1[l:l + 1],
                 diff_lambda_q2[l:l + 1], diff_lambda_k2[l:l + 1], diff_subln_w[l])
    return _ffn(x, mod3, o_hg, o_da, w_out[l].astype(BF16), norm2_w[l],
                w_ffn_gate[l].astype(BF16), w_ffn_up[l].astype(BF16),
                w_ffn_down[l].astype(BF16), final_norm_w)
```

```python
import math

import jax
import jax.numpy as jnp
from jax import lax
from jax.experimental import pallas as pl
from jax.experimental.pallas import tpu as pltpu

F32 = jnp.float32
BF16 = jnp.bfloat16

HG_HEADS = 4
HG_DIM = 128
DA_HEADS = 4
DA_HEAD_DIM = 64
N_MOD = 6
EPS = 1e-6
LAYER = 0
LAMBDA_INIT = 0.8 - 0.6 * math.exp(-0.3 * LAYER)
LOG2_E = math.log2(math.e)

LANES = 128
SUBLANES = 8
BF16_ROWS = 16
VMEM_LIMIT_BYTES = 56 * 1024 * 1024

HGRN_CHUNK = 64
HGRN_SUB = 8
INPROJ_TM = 512
FFN_TM = 512
FFN_FC = 256
ATT_T = 256
ATT_LOOKAHEAD = 2
NEG_BIG = -1e30
N_BIAS_PIECES = 3


def _nt_dot(a, b):
    return lax.dot_general(a, b, (((1,), (1,)), ((), ())), preferred_element_type=F32)


def _rms_scale(x):
    return lax.rsqrt(jnp.mean(x * x, axis=-1, keepdims=True) + EPS)


def _alibi_slope(head):
    return 2.0 ** (-8.0 * (head + 1) / DA_HEADS)


def _ada_kernel(c_ref, w_ref, b_ref, o_ref):
    c = c_ref[...]
    a = c * jax.nn.sigmoid(c)
    o_ref[...] = jnp.dot(a, w_ref[...], preferred_element_type=F32,
                         precision=lax.Precision.HIGHEST) + b_ref[...]


def _ada(c, w_ada, b_ada):
    bsz, d = c.shape
    n = w_ada.shape[1]
    tn = 1536
    return pl.pallas_call(
        _ada_kernel,
        grid=(n // tn,),
        in_specs=[pl.BlockSpec((bsz, d), lambda j: (0, 0)),
                  pl.BlockSpec((d, tn), lambda j: (0, j)),
                  pl.BlockSpec((1, tn), lambda j: (0, j))],
        out_specs=pl.BlockSpec((bsz, tn), lambda j: (0, j)),
        out_shape=jax.ShapeDtypeStruct((bsz, n), F32),
        compiler_params=pltpu.CompilerParams(
            dimension_semantics=("arbitrary",), vmem_limit_bytes=VMEM_LIMIT_BYTES),
        name="ada",
    )(c, w_ada, b_ada.reshape(1, n))


SLOT_HQ, SLOT_HI, SLOT_HG, SLOT_AQ1, SLOT_AQ2, SLOT_AK1, SLOT_AK2 = 0, 4, 8, 12, 16, 20, 24
N_SLABS = 28
GRP_HQ, GRP_HF, GRP_HI, GRP_HG, GRP_AQ, GRP_AK = range(6)


def _augment(ph, extra, first_map):
    lane = lax.broadcasted_iota(jnp.int32, ph.shape, 1)
    base = DA_HEAD_DIM if first_map else 0
    aug = jnp.zeros_like(ph)
    for n, e in enumerate(extra):
        aug = jnp.where(lane == base + n, e, aug)
    keep = (lane < DA_HEAD_DIM) if first_map else (lane >= DA_HEAD_DIM)
    return jnp.where(keep, ph, aug).astype(BF16)


def _inproj_kernel(x_ref, mod_ref, n1_ref, w_ref, wvt_ref, oa_ref, of_ref, ovt_ref):
    tm = x_ref.shape[1]
    x = x_ref[0]
    m = mod_ref[0]
    y = x * _rms_scale(x) * n1_ref[...]
    u = (y * (1.0 + m[1:2]) + m[0:1]).astype(BF16)
    pos = (pl.program_id(1) * tm + lax.broadcasted_iota(jnp.int32, (tm, LANES), 0)).astype(F32)
    ones = [jnp.ones((tm, LANES), F32)] * N_BIAS_PIECES
    for j in range(w_ref.shape[1] // (2 * LANES)):
        p = jnp.dot(u, w_ref[:, j * 2 * LANES:(j + 1) * 2 * LANES], preferred_element_type=F32)
        for half in range(2):
            ph = p[:, half * LANES:(half + 1) * LANES]
            grp, head = divmod(2 * j + half, 4)
            if grp == GRP_HF:
                of_ref[0, head] = ph
            elif grp == GRP_AQ:
                ph = ph * (LOG2_E * DA_HEAD_DIM ** -0.5)
                oa_ref[0, SLOT_AQ1 + head] = _augment(ph, ones, True)
                oa_ref[0, SLOT_AQ2 + head] = _augment(ph, ones, False)
            elif grp == GRP_AK:
                bias = (LOG2_E * _alibi_slope(head)) * pos
                pieces = []
                for _ in range(N_BIAS_PIECES):
                    piece = bias.astype(BF16).astype(F32)
                    pieces.append(piece)
                    bias = bias - piece
                oa_ref[0, SLOT_AK1 + head] = _augment(ph, pieces, True)
                oa_ref[0, SLOT_AK2 + head] = _augment(ph, pieces, False)
            else:
                slot = {GRP_HQ: SLOT_HQ, GRP_HI: SLOT_HI, GRP_HG: SLOT_HG}[grp]
                oa_ref[0, slot + head] = ph.astype(BF16)
    vt = _nt_dot(wvt_ref[...], u)
    for head in range(DA_HEADS):
        for jt in range(tm // ATT_T):
            ovt_ref[0, head, jt] = vt[head * LANES:(head + 1) * LANES,
                                      jt * ATT_T:(jt + 1) * ATT_T].astype(BF16)


def _inproj(x, mod3, norm1_w, w_main_bf, w_vt_bf):
    bsz, t, d = x.shape
    n = w_main_bf.shape[1]
    tm = INPROJ_TM
    return pl.pallas_call(
        _inproj_kernel,
        grid=(bsz, t // tm),
        in_specs=[pl.BlockSpec((1, tm, d), lambda b, i: (b, i, 0)),
                  pl.BlockSpec((1, N_MOD, d), lambda b, i: (b, 0, 0)),
                  pl.BlockSpec((1, d), lambda b, i: (0, 0)),
                  pl.BlockSpec((d, n), lambda b, i: (0, 0), pipeline_mode=pl.Buffered(1)),
                  pl.BlockSpec(w_vt_bf.shape, lambda b, i: (0, 0), pipeline_mode=pl.Buffered(1))],
        out_specs=[pl.BlockSpec((1, N_SLABS, tm, LANES), lambda b, i: (b, 0, i, 0)),
                   pl.BlockSpec((1, HG_HEADS, tm, LANES), lambda b, i: (b, 0, i, 0)),
                   pl.BlockSpec((1, DA_HEADS, tm // ATT_T, LANES, ATT_T),
                                lambda b, i: (b, 0, i, 0, 0))],
        out_shape=[jax.ShapeDtypeStruct((bsz, N_SLABS, t, LANES), BF16),
                   jax.ShapeDtypeStruct((bsz, HG_HEADS, t, LANES), F32),
                   jax.ShapeDtypeStruct((bsz, DA_HEADS, t // ATT_T, LANES, ATT_T), BF16)],
        compiler_params=pltpu.CompilerParams(
            dimension_semantics=("arbitrary", "arbitrary"), vmem_limit_bytes=VMEM_LIMIT_BYTES),
        name="inproj",
    )(x, mod3, norm1_w.reshape(1, d), w_main_bf, w_vt_bf)


def _hgrn_diag_scores(a_blks, q_blks, b_blks, c_rows, col0):
    sub, chunk = a_blks[0].shape
    n_sl = sub // SUBLANES
    row = lax.broadcasted_iota(jnp.int32, (SUBLANES, chunk), 0)
    lane = lax.broadcasted_iota(jnp.int32, (SUBLANES, chunk), 1)
    sl = [slice(r * SUBLANES, (r + 1) * SUBLANES) for r in range(n_sl)]
    parts = [[a[sl[r]] for r in range(n_sl)] for a in a_blks]
    for s in range(sub):
        first = s // SUBLANES
        here = lane == col0 + s
        here_diag = here & (row >= s - first * SUBLANES)
        for h, (qb, bb) in enumerate(zip(q_blks, b_blks)):
            cs = c_rows(h, s)
            for r in range(first, n_sl):
                col = jnp.sum(qb[sl[r]] * jnp.exp2(bb[sl[r]] - cs), axis=-1, keepdims=True)
                parts[h][r] = jnp.where(here_diag if r == first else here, col, parts[h][r])
    return [jnp.concatenate(p, axis=0) if n_sl > 1 else p[0] for p in parts]


def _hgrn_decays(lb, tri, f_logit, bkc_ref):
    f = lb + (1.0 - lb) * jax.nn.sigmoid(f_logit)
    logf = jnp.log2(f)
    k = 1.0 - f
    h1 = logf.astype(BF16)
    r1 = logf - h1.astype(F32)
    h2 = r1.astype(BF16)
    h3 = (r1 - h2.astype(F32)).astype(BF16)
    cs = jnp.dot(tri, jnp.concatenate([h1, h2, h3], axis=1), preferred_element_type=F32)
    b = cs[:, :LANES] + cs[:, LANES:2 * LANES] + cs[:, 2 * LANES:]
    bkc_ref[0] = b
    bkc_ref[1] = k
    bkc_ref[2] = b - jnp.log2(k)


def _hgrn_intra(qs, vs_bf, bkc_ref, oin_ref, qdec_ref, kend_ref, dec_ref):
    c_len, sub = HGRN_CHUNK, HGRN_SUB
    n_heads = len(qs)
    bs = [bkc_ref[h, 0] for h in range(n_heads)]
    ks = [bkc_ref[h, 1] for h in range(n_heads)]
    a_rows = [[] for _ in range(n_heads)]
    for i in range(c_len // sub):
        rows = slice(i * sub, (i + 1) * sub)
        a_blks = []
        for q, b, k in zip(qs, bs, ks):
            if i == 0:
                a_blks.append(jnp.zeros((sub, c_len), F32))
                continue
            ref = b[i * sub - 1:i * sub]
            qt = (q[rows] * jnp.exp2(b[rows] - ref)).astype(BF16)
            kt = k[:i * sub] * jnp.exp2(ref - b[:i * sub])
            kt = jnp.concatenate([kt, jnp.zeros((c_len - i * sub, LANES), F32)], axis=0).astype(BF16)
            a_blks.append(_nt_dot(qt, kt))
        c_rows = lambda h, s, i=i: bkc_ref[h, 2, pl.ds(i * sub + s, SUBLANES, stride=0), :]
        scored = _hgrn_diag_scores(a_blks, [q[rows] for q in qs], [b[rows] for b in bs], c_rows,
                                   i * sub)
        for h in range(n_heads):
            a_rows[h].append(scored[h])
    for h, (q, v_bf, b, k) in enumerate(zip(qs, vs_bf, bs, ks)):
        b_last = b[c_len - 1:c_len]
        a = jnp.concatenate(a_rows[h], axis=0).astype(BF16)
        oin_ref[h] = jnp.dot(a, v_bf, preferred_element_type=F32)
        qdec_ref[h] = (q * jnp.exp2(b)).astype(BF16)
        kend_ref[h] = (k * jnp.exp2(b_last - b)).astype(BF16)
        dec_ref[h] = jnp.exp2(b_last)


def _hgrn_state(gw, v_bf, g, st, oin_ref, qdec_ref, kend_ref, dec_ref):
    o = oin_ref[...] + _nt_dot(qdec_ref[...], st.astype(BF16))
    y = o * _rms_scale(o) * gw
    out = (y * (g * jax.nn.sigmoid(g))).astype(BF16)
    st_new = st * dec_ref[...] + jnp.dot(v_bf.astype(F32).T.astype(BF16), kend_ref[...],
                                         preferred_element_type=F32)
    return out, st_new


def _hgrn_kernel(lbl_ref, gw_ref, q_ref, f_ref, i_ref, g_ref, o_ref, st_ref, bkc_ref, oin_ref,
                 qdec_ref, kend_ref, dec_ref):
    c_len = HGRN_CHUNK
    n_heads, t_len = q_ref.shape[1], q_ref.shape[2]
    n_chunks = t_len // c_len
    assert n_chunks % 2 == 0
    l = lbl_ref[...]
    e = jnp.exp(l - jnp.max(l, axis=0, keepdims=True))
    lb_all = jnp.sum(e[:LAYER + 1], axis=0, keepdims=True) / jnp.sum(e, axis=0, keepdims=True)
    gw = gw_ref[...]
    r_i = lax.broadcasted_iota(jnp.int32, (c_len, c_len), 0)
    c_i = lax.broadcasted_iota(jnp.int32, (c_len, c_len), 1)
    tri = (r_i >= c_i).astype(BF16)
    st_ref[...] = jnp.zeros_like(st_ref)

    def rows_of(c):
        c = jnp.minimum(c, n_chunks - 1)
        return pl.ds(pl.multiple_of(c * c_len, c_len), c_len)

    def stage0(c, par):
        rows = rows_of(c)
        for h in range(n_heads):
            _hgrn_decays(lb_all[:, h * LANES:(h + 1) * LANES], tri, f_ref[0, h, rows, :],
                         bkc_ref.at[par, h])

    def stage1(c, par):
        rows = rows_of(c)
        _hgrn_intra([q_ref[0, h, rows, :].astype(F32) for h in range(n_heads)],
                    [i_ref[0, h, rows, :] for h in range(n_heads)], bkc_ref.at[par],
                    oin_ref.at[par], qdec_ref.at[par], kend_ref.at[par], dec_ref.at[par])

    def stage2(c, par):
        rows = rows_of(c)
        for h in range(n_heads):
            out, st_new = _hgrn_state(gw, i_ref[0, h, rows, :], g_ref[0, h, rows, :].astype(F32),
                                      st_ref[h], oin_ref.at[par, h], qdec_ref.at[par, h],
                                      kend_ref.at[par, h], dec_ref.at[par, h])
            o_ref[0, rows, h * LANES:(h + 1) * LANES] = out
            st_ref[h] = st_new

    stage0(0, 0)
    stage1(0, 0)
    stage0(1, 1)

    def pair(p, carry):
        c = 2 * p
        stage2(c, 0)
        stage1(c + 1, 1)
        stage0(c + 2, 0)
        stage2(c + 1, 1)
        stage1(c + 2, 0)
        stage0(c + 3, 1)
        return carry

    lax.fori_loop(0, n_chunks // 2, pair, 0)


def _hgrn(oa, of, lb_logits, gnorm_w):
    bsz, _, t, _ = oa.shape
    width = HG_HEADS * LANES
    slab = lambda slot: pl.BlockSpec((1, HG_HEADS, t, LANES), lambda b: (b, slot // HG_HEADS, 0, 0))
    return pl.pallas_call(
        _hgrn_kernel,
        grid=(bsz,),
        in_specs=[pl.BlockSpec((lb_logits.shape[0], width), lambda b: (0, 0)),
                  pl.BlockSpec((1, LANES), lambda b: (0, 0)),
                  slab(SLOT_HQ),
                  pl.BlockSpec((1, HG_HEADS, t, LANES), lambda b: (b, 0, 0, 0)),
                  slab(SLOT_HI), slab(SLOT_HG)],
        out_specs=pl.BlockSpec((1, t, width), lambda b: (b, 0, 0)),
        out_shape=jax.ShapeDtypeStruct((bsz, t, width), BF16),
        scratch_shapes=[pltpu.VMEM((HG_HEADS, HG_DIM, HG_DIM), F32),
                        pltpu.VMEM((2, HG_HEADS, 3, HGRN_CHUNK, LANES), F32),
                        pltpu.VMEM((2, HG_HEADS, HGRN_CHUNK, LANES), F32),
                        pltpu.VMEM((2, HG_HEADS, HGRN_CHUNK, LANES), BF16),
                        pltpu.VMEM((2, HG_HEADS, HGRN_CHUNK, LANES), BF16),
                        pltpu.VMEM((2, HG_HEADS, 1, LANES), F32)],
        compiler_params=pltpu.CompilerParams(
            dimension_semantics=("arbitrary",), vmem_limit_bytes=VMEM_LIMIT_BYTES),
        name="hgrn",
    )(lb_logits, gnorm_w.reshape(1, LANES), oa, of, oa, oa)


def _attn_kernel(lq1_ref, lk1_ref, lq2_ref, lk2_ref, sw_ref, q1_ref, q2_ref, k1_ref, k2_ref,
                 vt_ref, o_ref, s_ref, mx_ref):
    t = ATT_T
    nq = q1_ref.shape[2] // t
    lam = (jnp.exp(jnp.sum(lq1_ref[...] * lk1_ref[...], axis=-1, keepdims=True))
           - jnp.exp(jnp.sum(lq2_ref[...] * lk2_ref[...], axis=-1, keepdims=True))
           + LAMBDA_INIT)
    q_refs = (q1_ref, q2_ref)
    k_refs = (k1_ref, k2_ref)
    ones = jnp.ones((BF16_ROWS, t), BF16)
    causal = (lax.broadcasted_iota(jnp.int32, (t, t), 0)
              <= lax.broadcasted_iota(jnp.int32, (t, t), 1))
    tiles = [(qi, j) for qi in range(nq) for j in range(qi + 1)]
    n_buf = ATT_LOOKAHEAD + 1

    def key_max(s):
        quarter = t // 4
        parts = [s[i * quarter:(i + 1) * quarter] for i in range(4)]
        part = jnp.maximum(jnp.maximum(parts[0], parts[1]), jnp.maximum(parts[2], parts[3]))
        return jnp.max(part, axis=0, keepdims=True)

    def produce(n):
        qi, j = tiles[n]
        for mp in range(2):
            s = _nt_dot(k_refs[mp][0, 0, j * t:(j + 1) * t, :],
                        q_refs[mp][0, 0, qi * t:(qi + 1) * t, :])
            if j == qi:
                s = jnp.where(causal, s, NEG_BIG)
            s_ref[n % n_buf, mp] = s
            mx_ref[n % n_buf, mp] = key_max(s)

    def finalize(qi, stats):
        (_, acc1), (_, acc2) = stats
        o = (acc1[:LANES] * (1.0 / acc1[LANES:LANES + 1])
             - acc2[:LANES] * (lam / acc2[LANES:LANES + 1]))
        y = o * lax.rsqrt(jnp.mean(o * o, axis=0, keepdims=True) + EPS) * sw_ref[...]
        o_ref[0, qi * t:(qi + 1) * t, :] = (y * (1.0 - LAMBDA_INIT)).T.astype(o_ref.dtype)

    for n in range(min(ATT_LOOKAHEAD, len(tiles))):
        produce(n)
    stats = None
    for n, (qi, j) in enumerate(tiles):
        if n + ATT_LOOKAHEAD < len(tiles):
            produce(n + ATT_LOOKAHEAD)
        vta = jnp.concatenate([vt_ref[0, 0, j], ones], axis=0)
        new_stats = []
        for mp in range(2):
            s = s_ref[n % n_buf, mp]
            mx = mx_ref[n % n_buf, mp]
            if j == 0:
                m_new = mx
                acc = jnp.dot(vta, jnp.exp2(s - m_new).astype(BF16), preferred_element_type=F32)
            else:
                m_old, acc_old = stats[mp]
                m_new = jnp.maximum(m_old, mx)
                acc = (jnp.exp2(m_old - m_new) * acc_old
                       + jnp.dot(vta, jnp.exp2(s - m_new).astype(BF16),
                                 preferred_element_type=F32))
            new_stats.append((m_new, acc))
        stats = new_stats
        if j == qi:
            finalize(qi, stats)


def _attn(oa, ovt, lq1, lk1, lq2, lk2, subln_w):
    bsz, _, t, _ = oa.shape
    tq = ATT_T
    vec = lambda n: pl.BlockSpec((1, n), lambda b, h: (0, 0))
    slab = lambda slot: pl.BlockSpec((1, 1, t, LANES), lambda b, h: (b, slot + h, 0, 0))
    return pl.pallas_call(
        _attn_kernel,
        grid=(bsz, DA_HEADS),
        in_specs=[vec(DA_HEAD_DIM), vec(DA_HEAD_DIM), vec(DA_HEAD_DIM), vec(DA_HEAD_DIM),
                  pl.BlockSpec((2 * DA_HEAD_DIM, 1), lambda b, h: (0, 0)),
                  slab(SLOT_AQ1), slab(SLOT_AQ2), slab(SLOT_AK1), slab(SLOT_AK2),
                  pl.BlockSpec((1, 1, t // tq, LANES, tq), lambda b, h: (b, h, 0, 0, 0))],
        out_specs=pl.BlockSpec((1, t, LANES), lambda b, h: (b, 0, h)),
        out_shape=jax.ShapeDtypeStruct((bsz, t, DA_HEADS * LANES), BF16),
        scratch_shapes=[pltpu.VMEM((ATT_LOOKAHEAD + 1, 2, tq, tq), F32),
                        pltpu.VMEM((ATT_LOOKAHEAD + 1, 2, 1, tq), F32)],
        compiler_params=pltpu.CompilerParams(
            dimension_semantics=("arbitrary", "arbitrary"), vmem_limit_bytes=VMEM_LIMIT_BYTES),
        name="attn",
    )(lq1, lk1, lq2, lk2, subln_w.reshape(2 * DA_HEAD_DIM, 1), oa, oa, oa, oa, ovt)


def _ffn_kernel(x_ref, mod_ref, ohg_ref, oda_ref, wo_ref, n2_ref, wg_ref, wu_ref, wd_ref, fw_ref,
                o_ref):
    x = x_ref[0]
    m = mod_ref[0]
    hgw = ohg_ref.shape[2]
    mix = (jnp.dot(ohg_ref[0], wo_ref[:hgw], preferred_element_type=F32)
           + jnp.dot(oda_ref[0], wo_ref[hgw:], preferred_element_type=F32))
    hcur = x + m[2:3] * mix
    u = (hcur * _rms_scale(hcur) * n2_ref[...] * (1.0 + m[4:5]) + m[3:4]).astype(BF16)
    acc = jnp.zeros(x.shape, F32)
    for c in range(wg_ref.shape[1] // FFN_FC):
        sl = slice(c * FFN_FC, (c + 1) * FFN_FC)
        gate = jnp.dot(u, wg_ref[:, sl], preferred_element_type=F32)
        up = jnp.dot(u, wu_ref[:, sl], preferred_element_type=F32)
        act = (gate * jax.nn.sigmoid(gate) * up).astype(BF16)
        acc = acc + jnp.dot(act, wd_ref[sl, :], preferred_element_type=F32)
    hcur = hcur + m[5:6] * acc
    o_ref[0] = hcur * _rms_scale(hcur) * fw_ref[...]


def _ffn(x, mod3, o_hg, o_da, w_out_bf, norm2_w, wg_bf, wu_bf, wd_bf, final_w):
    bsz, t, d = x.shape
    tm = FFN_TM
    dff = wg_bf.shape[1]
    assert dff % FFN_FC == 0
    tile = lambda w: pl.BlockSpec((1, tm, w), lambda b, i: (b, i, 0))
    const = lambda r, c: pl.BlockSpec((r, c), lambda b, i: (0, 0), pipeline_mode=pl.Buffered(1))
    return pl.pallas_call(
        _ffn_kernel,
        grid=(bsz, t // tm),
        in_specs=[tile(d),
                  pl.BlockSpec((1, N_MOD, d), lambda b, i: (b, 0, 0)),
                  tile(o_hg.shape[2]), tile(o_da.shape[2]),
                  const(d, d), const(1, d), const(d, dff), const(d, dff), const(dff, d),
                  const(1, d)],
        out_specs=tile(d),
        out_shape=jax.ShapeDtypeStruct((bsz, t, d), F32),
        compiler_params=pltpu.CompilerParams(
            dimension_semantics=("arbitrary", "arbitrary"), vmem_limit_bytes=VMEM_LIMIT_BYTES),
        name="ffn",
    )(x, mod3, o_hg, o_da, w_out_bf, norm2_w.reshape(1, d), wg_bf, wu_bf, wd_bf,
      final_w.reshape(1, d))


def kernel(x, c, w_ada, b_ada, norm1_w, w_in, hgrn_lb_logits, hgrn_gnorm_w, diff_lambda_q1,
           diff_lambda_k1, diff_lambda_q2, diff_lambda_k2, diff_subln_w, w_out, norm2_w,
           w_ffn_gate, w_ffn_up, w_ffn_down, final_norm_w):
    bsz, t, d = x.shape
    assert w_ada.shape[0] == 1, "single-layer trunk"
    l = LAYER
    mod3 = _ada(c, w_ada[l], b_ada[l]).reshape(bsz, N_MOD, d)
    n_main = w_in.shape[2] - DA_HEADS * 2 * DA_HEAD_DIM
    oa, of, ovt = _inproj(x, mod3, norm1_w[l], w_in[l, :, :n_main].astype(BF16),
                          w_in[l, :, n_main:].T.astype(BF16))
    o_hg = _hgrn(oa, of, hgrn_lb_logits, hgrn_gnorm_w[l])
    o_da = _attn(oa, ovt, diff_lambda_q1[l:l + 1], diff_lambda_k1[l:l + 1],
                 diff_lambda_q2[l:l + 1], diff_lambda_k2[l:l + 1], diff_subln_w[l])
    return _ffn(x, mod3, o_hg, o_da, w_out[l].astype(BF16), norm2_w[l],
                w_ffn_gate[l].astype(BF16), w_ffn_up[l].astype(BF16),
                w_ffn_down[l].astype(BF16), final_norm_w)
```

```python
import math

import jax
import jax.numpy as jnp
import numpy as np
from jax import lax
from jax.experimental import pallas as pl
from jax.experimental.pallas import tpu as pltpu

F32 = jnp.float32
BF16 = jnp.bfloat16

HG_HEADS = 4
HG_DIM = 128
DA_HEADS = 4
DA_HEAD_DIM = 64
N_MOD = 6
EPS = 1e-6
LAYER = 0
LAMBDA_INIT = 0.8 - 0.6 * math.exp(-0.3 * LAYER)
LOG2_E = math.log2(math.e)

LANES = 128
SUBLANES = 8
BF16_ROWS = 16
VMEM_LIMIT_BYTES = 56 * 1024 * 1024

HGRN_CHUNK = 64
HGRN_SUB = 8
ADA_TN = 768
INPROJ_TM = 512
FFN_TM = 512
FFN_FC = 256
ATT_T = 256
ATT_LOOKAHEAD = 2
NEG_BIG = -1e30
N_SLOPE_PIECES = 3


def _nt_dot(a, b):
    return lax.dot_general(a, b, (((1,), (1,)), ((), ())), preferred_element_type=F32)


def _rms_scale(x):
    return lax.rsqrt(jnp.mean(x * x, axis=-1, keepdims=True) + EPS)


def _alibi_slope(head):
    return 2.0 ** (-8.0 * (head + 1) / DA_HEADS)


def _ada_kernel(c_ref, w_ref, b_ref, o_ref):
    c = c_ref[...]
    a = c * jax.nn.sigmoid(c)
    o_ref[...] = jnp.dot(a, w_ref[...], preferred_element_type=F32,
                         precision=lax.Precision.HIGHEST) + b_ref[...]


def _ada(c, w_ada, b_ada):
    bsz, d = c.shape
    n = w_ada.shape[1]
    tn = ADA_TN
    return pl.pallas_call(
        _ada_kernel,
        grid=(n // tn,),
        in_specs=[pl.BlockSpec((bsz, d), lambda j: (0, 0)),
                  pl.BlockSpec((d, tn), lambda j: (0, j)),
                  pl.BlockSpec((1, tn), lambda j: (0, j))],
        out_specs=pl.BlockSpec((bsz, tn), lambda j: (0, j)),
        out_shape=jax.ShapeDtypeStruct((bsz, n), F32),
        compiler_params=pltpu.CompilerParams(
            dimension_semantics=("arbitrary",), vmem_limit_bytes=VMEM_LIMIT_BYTES),
        name="ada",
    )(c, w_ada, b_ada.reshape(1, n))


SLOT_HQ, SLOT_HI, SLOT_HG, SLOT_AQ1, SLOT_AQ2, SLOT_AK1, SLOT_AK2 = 0, 4, 8, 12, 16, 20, 24
N_SLABS = 28
GRP_HQ, GRP_HF, GRP_HI, GRP_HG, GRP_AQ, GRP_AK = range(6)


def _bf16_pieces(value, n):
    rest = np.float32(value)
    pieces = []
    for _ in range(n):
        piece = np.float32(np.asarray(rest, dtype=BF16))
        pieces.append(float(piece))
        rest = np.float32(rest - piece)
    return pieces


def _aug_lanes(lane, values, first_map):
    base = DA_HEAD_DIM if first_map else 0
    aug = jnp.zeros(lane.shape, F32)
    for n, e in enumerate(values):
        aug = jnp.where(lane == base + n, e, aug)
    return aug


def _inproj_kernel(x_ref, mod_ref, n1_ref, w_ref, wvt_ref, oa_ref, of_ref, ovt_ref):
    tm = x_ref.shape[1]
    x = x_ref[0]
    m = mod_ref[0]
    y = x * _rms_scale(x) * n1_ref[...]
    u = (y * (1.0 + m[1:2]) + m[0:1]).astype(BF16)

    lane = lax.broadcasted_iota(jnp.int32, (tm, LANES), 1)
    keep = (lane < DA_HEAD_DIM, lane >= DA_HEAD_DIM)
    maps = ((SLOT_AQ1, SLOT_AK1, True), (SLOT_AQ2, SLOT_AK2, False))
    pos = (pl.program_id(1) * tm + lax.broadcasted_iota(jnp.int32, (tm, LANES), 0)).astype(F32)
    pos_hi = pos.astype(BF16).astype(F32)
    pos_pieces = [pos_hi, pos - pos_hi]
    k_aug = [_aug_lanes(lane, pos_pieces * N_SLOPE_PIECES, first) for _, _, first in maps]
    q_aug = []
    for head in range(DA_HEADS):
        c_lanes = [c for c in _bf16_pieces(LOG2_E * _alibi_slope(head), N_SLOPE_PIECES)
                   for _ in pos_pieces]
        q_aug.append([_aug_lanes(lane, c_lanes, first) for _, _, first in maps])

    for j in range(w_ref.shape[1] // (2 * LANES)):
        p = jnp.dot(u, w_ref[:, j * 2 * LANES:(j + 1) * 2 * LANES], preferred_element_type=F32)
        for half in range(2):
            ph = p[:, half * LANES:(half + 1) * LANES]
            grp, head = divmod(2 * j + half, 4)
            if grp == GRP_HF:
                of_ref[0, head] = ph
            elif grp == GRP_AQ:
                ph = ph * (LOG2_E * DA_HEAD_DIM ** -0.5)
                for mp, (q_slot, _, _) in enumerate(maps):
                    oa_ref[0, q_slot + head] = jnp.where(keep[mp], ph, q_aug[head][mp]).astype(BF16)
            elif grp == GRP_AK:
                for mp, (_, k_slot, _) in enumerate(maps):
                    oa_ref[0, k_slot + head] = jnp.where(keep[mp], ph, k_aug[mp]).astype(BF16)
            else:
                slot = {GRP_HQ: SLOT_HQ, GRP_HI: SLOT_HI, GRP_HG: SLOT_HG}[grp]
                oa_ref[0, slot + head] = ph.astype(BF16)
    vt = _nt_dot(wvt_ref[...], u)
    for head in range(DA_HEADS):
        for jt in range(tm // ATT_T):
            ovt_ref[0, head, jt] = vt[head * LANES:(head + 1) * LANES,
                                      jt * ATT_T:(jt + 1) * ATT_T].astype(BF16)


def _inproj(x, mod3, norm1_w, w_main_bf, w_vt_bf):
    bsz, t, d = x.shape
    n = w_main_bf.shape[1]
    tm = INPROJ_TM
    return pl.pallas_call(
        _inproj_kernel,
        grid=(bsz, t // tm),
        in_specs=[pl.BlockSpec((1, tm, d), lambda b, i: (b, i, 0)),
                  pl.BlockSpec((1, N_MOD, d), lambda b, i: (b, 0, 0)),
                  pl.BlockSpec((1, d), lambda b, i: (0, 0)),
                  pl.BlockSpec((d, n), lambda b, i: (0, 0), pipeline_mode=pl.Buffered(1)),
                  pl.BlockSpec(w_vt_bf.shape, lambda b, i: (0, 0), pipeline_mode=pl.Buffered(1))],
        out_specs=[pl.BlockSpec((1, N_SLABS, tm, LANES), lambda b, i: (b, 0, i, 0)),
                   pl.BlockSpec((1, HG_HEADS, tm, LANES), lambda b, i: (b, 0, i, 0)),
                   pl.BlockSpec((1, DA_HEADS, tm // ATT_T, LANES, ATT_T),
                                lambda b, i: (b, 0, i, 0, 0))],
        out_shape=[jax.ShapeDtypeStruct((bsz, N_SLABS, t, LANES), BF16),
                   jax.ShapeDtypeStruct((bsz, HG_HEADS, t, LANES), F32),
                   jax.ShapeDtypeStruct((bsz, DA_HEADS, t // ATT_T, LANES, ATT_T), BF16)],
        compiler_params=pltpu.CompilerParams(
            dimension_semantics=("arbitrary", "arbitrary"), vmem_limit_bytes=VMEM_LIMIT_BYTES),
        name="inproj",
    )(x, mod3, norm1_w.reshape(1, d), w_main_bf, w_vt_bf)


def _hgrn_diag_scores(a_blks, q_blks, b_blks, c_rows, col0):
    sub, chunk = a_blks[0].shape
    n_sl = sub // SUBLANES
    row = lax.broadcasted_iota(jnp.int32, (SUBLANES, chunk), 0)
    lane = lax.broadcasted_iota(jnp.int32, (SUBLANES, chunk), 1)
    sl = [slice(r * SUBLANES, (r + 1) * SUBLANES) for r in range(n_sl)]
    parts = [[a[sl[r]] for r in range(n_sl)] for a in a_blks]
    for s in range(sub):
        first = s // SUBLANES
        here = lane == col0 + s
        here_diag = here & (row >= s - first * SUBLANES)
        for h, (qb, bb) in enumerate(zip(q_blks, b_blks)):
            cs = c_rows(h, s)
            for r in range(first, n_sl):
                col = jnp.sum(qb[sl[r]] * jnp.exp2(bb[sl[r]] - cs), axis=-1, keepdims=True)
                parts[h][r] = jnp.where(here_diag if r == first else here, col, parts[h][r])
    return [jnp.concatenate(p, axis=0) if n_sl > 1 else p[0] for p in parts]


def _hgrn_decays(lb, tri, f_logit, bkc_ref):
    f = lb + (1.0 - lb) * jax.nn.sigmoid(f_logit)
    logf = jnp.log2(f)
    k = 1.0 - f
    h1 = logf.astype(BF16)
    r1 = logf - h1.astype(F32)
    h2 = r1.astype(BF16)
    h3 = (r1 - h2.astype(F32)).astype(BF16)
    cs = jnp.dot(tri, jnp.concatenate([h1, h2, h3], axis=1), preferred_element_type=F32)
    b = cs[:, :LANES] + cs[:, LANES:2 * LANES] + cs[:, 2 * LANES:]
    bkc_ref[0] = b
    bkc_ref[1] = k
    bkc_ref[2] = b - jnp.log2(k)


def _hgrn_intra(qs, vs_bf, bkc_ref, oin_ref, qdec_ref, kend_ref, dec_ref):
    c_len, sub = HGRN_CHUNK, HGRN_SUB
    n_heads = len(qs)
    bs = [bkc_ref[h, 0] for h in range(n_heads)]
    ks = [bkc_ref[h, 1] for h in range(n_heads)]
    a_rows = [[] for _ in range(n_heads)]
    for i in range(c_len // sub):
        rows = slice(i * sub, (i + 1) * sub)
        a_blks = []
        for q, b, k in zip(qs, bs, ks):
            if i == 0:
                a_blks.append(jnp.zeros((sub, c_len), F32))
                continue
            ref = b[i * sub - 1:i * sub]
            qt = (q[rows] * jnp.exp2(b[rows] - ref)).astype(BF16)
            kt = k[:i * sub] * jnp.exp2(ref - b[:i * sub])
            kt = jnp.concatenate([kt, jnp.zeros((c_len - i * sub, LANES), F32)], axis=0).astype(BF16)
            a_blks.append(_nt_dot(qt, kt))
        c_rows = lambda h, s, i=i: bkc_ref[h, 2, pl.ds(i * sub + s, SUBLANES, stride=0), :]
        scored = _hgrn_diag_scores(a_blks, [q[rows] for q in qs], [b[rows] for b in bs], c_rows,
                                   i * sub)
        for h in range(n_heads):
            a_rows[h].append(scored[h])
    for h, (q, v_bf, b, k) in enumerate(zip(qs, vs_bf, bs, ks)):
        b_last = b[c_len - 1:c_len]
        a = jnp.concatenate(a_rows[h], axis=0).astype(BF16)
        oin_ref[h] = jnp.dot(a, v_bf, preferred_element_type=F32)
        qdec_ref[h] = (q * jnp.exp2(b)).astype(BF16)
        kend_ref[h] = (k * jnp.exp2(b_last - b)).astype(BF16)
        dec_ref[h] = jnp.exp2(b_last)


def _hgrn_state(gw, v_bf, g, st, oin_ref, qdec_ref, kend_ref, dec_ref):
    o = oin_ref[...] + _nt_dot(qdec_ref[...], st.astype(BF16))
    y = o * _rms_scale(o) * gw
    out = (y * (g * jax.nn.sigmoid(g))).astype(BF16)
    st_new = st * dec_ref[...] + jnp.dot(v_bf.astype(F32).T.astype(BF16), kend_ref[...],
                                         preferred_element_type=F32)
    return out, st_new


def _hgrn_kernel(lbl_ref, gw_ref, q_ref, f_ref, i_ref, g_ref, o_ref, st_ref, bkc_ref, oin_ref,
                 qdec_ref, kend_ref, dec_ref):
    c_len = HGRN_CHUNK
    n_heads, t_len = q_ref.shape[1], q_ref.shape[2]
    n_chunks = t_len // c_len
    assert n_chunks % 2 == 0
    l = lbl_ref[...]
    e = jnp.exp(l - jnp.max(l, axis=0, keepdims=True))
    lb_all = jnp.sum(e[:LAYER + 1], axis=0, keepdims=True) / jnp.sum(e, axis=0, keepdims=True)
    gw = gw_ref[...]
    r_i = lax.broadcasted_iota(jnp.int32, (c_len, c_len), 0)
    c_i = lax.broadcasted_iota(jnp.int32, (c_len, c_len), 1)
    tri = (r_i >= c_i).astype(BF16)
    st_ref[...] = jnp.zeros_like(st_ref)

    def rows_of(c):
        c = jnp.minimum(c, n_chunks - 1)
        return pl.ds(pl.multiple_of(c * c_len, c_len), c_len)

    def stage0(c, par):
        rows = rows_of(c)
        for h in range(n_heads):
            _hgrn_decays(lb_all[:, h * LANES:(h + 1) * LANES], tri, f_ref[0, h, rows, :],
                         bkc_ref.at[par, h])

    def stage1(c, par):
        rows = rows_of(c)
        _hgrn_intra([q_ref[0, h, rows, :].astype(F32) for h in range(n_heads)],
                    [i_ref[0, h, rows, :] for h in range(n_heads)], bkc_ref.at[par],
                    oin_ref.at[par], qdec_ref.at[par], kend_ref.at[par], dec_ref.at[par])

    def stage2(c, par):
        rows = rows_of(c)
        for h in range(n_heads):
            out, st_new = _hgrn_state(gw, i_ref[0, h, rows, :], g_ref[0, h, rows, :].astype(F32),
                                      st_ref[h], oin_ref.at[par, h], qdec_ref.at[par, h],
                                      kend_ref.at[par, h], dec_ref.at[par, h])
            o_ref[0, rows, h * LANES:(h + 1) * LANES] = out
            st_ref[h] = st_new

    stage0(0, 0)
    stage1(0, 0)
    stage0(1, 1)

    def pair(p, carry):
        c = 2 * p
        stage2(c, 0)
        stage1(c + 1, 1)
        stage0(c + 2, 0)
        stage2(c + 1, 1)
        stage1(c + 2, 0)
        stage0(c + 3, 1)
        return carry

    lax.fori_loop(0, n_chunks // 2, pair, 0)


def _hgrn(oa, of, lb_logits, gnorm_w):
    bsz, _, t, _ = oa.shape
    width = HG_HEADS * LANES
    slab = lambda slot: pl.BlockSpec((1, HG_HEADS, t, LANES), lambda b: (b, slot // HG_HEADS, 0, 0))
    return pl.pallas_call(
        _hgrn_kernel,
        grid=(bsz,),
        in_specs=[pl.BlockSpec((lb_logits.shape[0], width), lambda b: (0, 0)),
                  pl.BlockSpec((1, LANES), lambda b: (0, 0)),
                  slab(SLOT_HQ),
                  pl.BlockSpec((1, HG_HEADS, t, LANES), lambda b: (b, 0, 0, 0)),
                  slab(SLOT_HI), slab(SLOT_HG)],
        out_specs=pl.BlockSpec((1, t, width), lambda b: (b, 0, 0)),
        out_shape=jax.ShapeDtypeStruct((bsz, t, width), BF16),
        scratch_shapes=[pltpu.VMEM((HG_HEADS, HG_DIM, HG_DIM), F32),
                        pltpu.VMEM((2, HG_HEADS, 3, HGRN_CHUNK, LANES), F32),
                        pltpu.VMEM((2, HG_HEADS, HGRN_CHUNK, LANES), F32),
                        pltpu.VMEM((2, HG_HEADS, HGRN_CHUNK, LANES), BF16),
                        pltpu.VMEM((2, HG_HEADS, HGRN_CHUNK, LANES), BF16),
                        pltpu.VMEM((2, HG_HEADS, 1, LANES), F32)],
        compiler_params=pltpu.CompilerParams(
            dimension_semantics=("arbitrary",), vmem_limit_bytes=VMEM_LIMIT_BYTES),
        name="hgrn",
    )(lb_logits, gnorm_w.reshape(1, LANES), oa, of, oa, oa)


def _attn_kernel(lq1_ref, lk1_ref, lq2_ref, lk2_ref, sw_ref, q1_ref, q2_ref, k1_ref, k2_ref,
                 vt_ref, o_ref, s_ref, mx_ref):
    t = ATT_T
    nq = q1_ref.shape[2] // t
    lam = (jnp.exp(jnp.sum(lq1_ref[...] * lk1_ref[...], axis=-1, keepdims=True))
           - jnp.exp(jnp.sum(lq2_ref[...] * lk2_ref[...], axis=-1, keepdims=True))
           + LAMBDA_INIT)
    q_refs = (q1_ref, q2_ref)
    k_refs = (k1_ref, k2_ref)
    ones = jnp.ones((BF16_ROWS, t), BF16)
    causal = (lax.broadcasted_iota(jnp.int32, (t, t), 0)
              <= lax.broadcasted_iota(jnp.int32, (t, t), 1))
    tiles = [(qi, j) for qi in range(nq) for j in range(qi + 1)]
    n_buf = ATT_LOOKAHEAD + 1

    def key_max(s):
        quarter = t // 4
        parts = [s[i * quarter:(i + 1) * quarter] for i in range(4)]
        part = jnp.maximum(jnp.maximum(parts[0], parts[1]), jnp.maximum(parts[2], parts[3]))
        return jnp.max(part, axis=0, keepdims=True)

    def produce(n):
        qi, j = tiles[n]
        for mp in range(2):
            s = _nt_dot(k_refs[mp][0, 0, j * t:(j + 1) * t, :],
                        q_refs[mp][0, 0, qi * t:(qi + 1) * t, :])
            if j == qi:
                s = jnp.where(causal, s, NEG_BIG)
            s_ref[n % n_buf, mp] = s
            mx_ref[n % n_buf, mp] = key_max(s)

    def finalize(qi, stats):
        (_, acc1), (_, acc2) = stats
        o = (acc1[:LANES] * (1.0 / acc1[LANES:LANES + 1])
             - acc2[:LANES] * (lam / acc2[LANES:LANES + 1]))
        y = o * lax.rsqrt(jnp.mean(o * o, axis=0, keepdims=True) + EPS) * sw_ref[...]
        o_ref[0, qi * t:(qi + 1) * t, :] = (y * (1.0 - LAMBDA_INIT)).T.astype(o_ref.dtype)

    for n in range(min(ATT_LOOKAHEAD, len(tiles))):
        produce(n)
    stats = None
    for n, (qi, j) in enumerate(tiles):
        if n + ATT_LOOKAHEAD < len(tiles):
            produce(n + ATT_LOOKAHEAD)
        vta = jnp.concatenate([vt_ref[0, 0, j], ones], axis=0)
        new_stats = []
        for mp in range(2):
            s = s_ref[n % n_buf, mp]
            mx = mx_ref[n % n_buf, mp]
            if j == 0:
                m_new = mx
                acc = jnp.dot(vta, jnp.exp2(s - m_new).astype(BF16), preferred_element_type=F32)
            else:
                m_old, acc_old = stats[mp]
                m_new = jnp.maximum(m_old, mx)
                acc = (jnp.exp2(m_old - m_new) * acc_old
                       + jnp.dot(vta, jnp.exp2(s - m_new).astype(BF16),
                                 preferred_element_type=F32))
            new_stats.append((m_new, acc))
        stats = new_stats
        if j == qi:
            finalize(qi, stats)


def _attn(oa, ovt, lq1, lk1, lq2, lk2, subln_w):
    bsz, _, t, _ = oa.shape
    tq = ATT_T
    vec = lambda n: pl.BlockSpec((1, n), lambda b, h: (0, 0))
    slab = lambda slot: pl.BlockSpec((1, 1, t, LANES), lambda b, h: (b, slot + h, 0, 0))
    return pl.pallas_call(
        _attn_kernel,
        grid=(bsz, DA_HEADS),
        in_specs=[vec(DA_HEAD_DIM), vec(DA_HEAD_DIM), vec(DA_HEAD_DIM), vec(DA_HEAD_DIM),
                  pl.BlockSpec((2 * DA_HEAD_DIM, 1), lambda b, h: (0, 0)),
                  slab(SLOT_AQ1), slab(SLOT_AQ2), slab(SLOT_AK1), slab(SLOT_AK2),
                  pl.BlockSpec((1, 1, t // tq, LANES, tq), lambda b, h: (b, h, 0, 0, 0))],
        out_specs=pl.BlockSpec((1, t, LANES), lambda b, h: (b, 0, h)),
        out_shape=jax.ShapeDtypeStruct((bsz, t, DA_HEADS * LANES), BF16),
        scratch_shapes=[pltpu.VMEM((ATT_LOOKAHEAD + 1, 2, tq, tq), F32),
                        pltpu.VMEM((ATT_LOOKAHEAD + 1, 2, 1, tq), F32)],
        compiler_params=pltpu.CompilerParams(
            dimension_semantics=("arbitrary", "arbitrary"), vmem_limit_bytes=VMEM_LIMIT_BYTES),
        name="attn",
    )(lq1, lk1, lq2, lk2, subln_w.reshape(2 * DA_HEAD_DIM, 1), oa, oa, oa, oa, ovt)


def _ffn_kernel(x_ref, mod_ref, ohg_ref, oda_ref, wo_ref, n2_ref, wg_ref, wu_ref, wd_ref, fw_ref,
                o_ref):
    x = x_ref[0]
    m = mod_ref[0]
    hgw = ohg_ref.shape[2]
    mix = (jnp.dot(ohg_ref[0], wo_ref[:hgw], preferred_element_type=F32)
           + jnp.dot(oda_ref[0], wo_ref[hgw:], preferred_element_type=F32))
    hcur = x + m[2:3] * mix
    u = (hcur * _rms_scale(hcur) * n2_ref[...] * (1.0 + m[4:5]) + m[3:4]).astype(BF16)
    acc = jnp.zeros(x.shape, F32)
    for c in range(wg_ref.shape[1] // FFN_FC):
        sl = slice(c * FFN_FC, (c + 1) * FFN_FC)
        gate = jnp.dot(u, wg_ref[:, sl], preferred_element_type=F32)
        up = jnp.dot(u, wu_ref[:, sl], preferred_element_type=F32)
        act = (gate * jax.nn.sigmoid(gate) * up).astype(BF16)
        acc = acc + jnp.dot(act, wd_ref[sl, :], preferred_element_type=F32)
    hcur = hcur + m[5:6] * acc
    o_ref[0] = hcur * _rms_scale(hcur) * fw_ref[...]


def _ffn(x, mod3, o_hg, o_da, w_out_bf, norm2_w, wg_bf, wu_bf, wd_bf, final_w):
    bsz, t, d = x.shape
    tm = FFN_TM
    dff = wg_bf.shape[1]
    assert dff % FFN_FC == 0
    tile = lambda w: pl.BlockSpec((1, tm, w), lambda b, i: (b, i, 0))
    const = lambda r, c: pl.BlockSpec((r, c), lambda b, i: (0, 0), pipeline_mode=pl.Buffered(1))
    return pl.pallas_call(
        _ffn_kernel,
        grid=(bsz, t // tm),
        in_specs=[tile(d),
                  pl.BlockSpec((1, N_MOD, d), lambda b, i: (b, 0, 0)),
                  tile(o_hg.shape[2]), tile(o_da.shape[2]),
                  const(d, d), const(1, d), const(d, dff), const(d, dff), const(dff, d),
                  const(1, d)],
        out_specs=tile(d),
        out_shape=jax.ShapeDtypeStruct((bsz, t, d), F32),
        compiler_params=pltpu.CompilerParams(
            dimension_semantics=("arbitrary", "arbitrary"), vmem_limit_bytes=VMEM_LIMIT_BYTES),
        name="ffn",
    )(x, mod3, o_hg, o_da, w_out_bf, norm2_w.reshape(1, d), wg_bf, wu_bf, wd_bf,
      final_w.reshape(1, d))


def kernel(x, c, w_ada, b_ada, norm1_w, w_in, hgrn_lb_logits, hgrn_gnorm_w, diff_lambda_q1,
           diff_lambda_k1, diff_lambda_q2, diff_lambda_k2, diff_subln_w, w_out, norm2_w,
           w_ffn_gate, w_ffn_up, w_ffn_down, final_norm_w):
    bsz, t, d = x.shape
    assert w_ada.shape[0] == 1, "single-layer trunk"
    l = LAYER
    mod3 = _ada(c, w_ada[l], b_ada[l]).reshape(bsz, N_MOD, d)
    n_main = w_in.shape[2] - DA_HEADS * 2 * DA_HEAD_DIM
    oa, of, ovt = _inproj(x, mod3, norm1_w[l], w_in[l, :, :n_main].astype(BF16),
                          w_in[l, :, n_main:].T.astype(BF16))
    o_hg = _hgrn(oa, of, hgrn_lb_logits, hgrn_gnorm_w[l])
    o_da = _attn(oa, ovt, diff_lambda_q1[l:l + 1], diff_lambda_k1[l:l + 1],
                 diff_lambda_q2[l:l + 1], diff_lambda_k2[l:l + 1], diff_subln_w[l])
    return _ffn(x, mod3, o_hg, o_da, w_out[l].astype(BF16), norm2_w[l],
                w_ffn_gate[l].astype(BF16), w_ffn_up[l].astype(BF16),
                w_ffn_down[l].astype(BF16), final_norm_w)
```

```python
import math

import jax
import jax.numpy as jnp
import numpy as np
from jax import lax
from jax.experimental import pallas as pl
from jax.experimental.pallas import tpu as pltpu

F32 = jnp.float32
BF16 = jnp.bfloat16

HG_HEADS = 4
HG_DIM = 128
DA_HEADS = 4
DA_HEAD_DIM = 64
N_MOD = 6
EPS = 1e-6
LAYER = 0
LAMBDA_INIT = 0.8 - 0.6 * math.exp(-0.3 * LAYER)
LOG2_E = math.log2(math.e)

LANES = 128
SUBLANES = 8
BF16_ROWS = 16
VMEM_LIMIT_BYTES = 56 * 1024 * 1024

HGRN_CHUNK = 64
HGRN_SUB = 8
HGRN_MAX_SUB_DECAY_LOG2 = 64.0
ADA_TN = 768
INPROJ_TM = 512
FFN_TM = 512
FFN_FC = 256
ATT_T = 256
ATT_LOOKAHEAD = 2
NEG_BIG = -1e30
N_SLOPE_PIECES = 3


def _nt_dot(a, b):
    return lax.dot_general(a, b, (((1,), (1,)), ((), ())), preferred_element_type=F32)


def _rms_scale(x):
    return lax.rsqrt(jnp.mean(x * x, axis=-1, keepdims=True) + EPS)


def _alibi_slope(head):
    return 2.0 ** (-8.0 * (head + 1) / DA_HEADS)


def _ada_kernel(c_ref, w_ref, b_ref, o_ref):
    c = c_ref[...]
    a = c * jax.nn.sigmoid(c)
    o_ref[...] = jnp.dot(a, w_ref[...], preferred_element_type=F32,
                         precision=lax.Precision.HIGHEST) + b_ref[...]


def _ada(c, w_ada, b_ada):
    bsz, d = c.shape
    n = w_ada.shape[1]
    tn = ADA_TN
    return pl.pallas_call(
        _ada_kernel,
        grid=(n // tn,),
        in_specs=[pl.BlockSpec((bsz, d), lambda j: (0, 0)),
                  pl.BlockSpec((d, tn), lambda j: (0, j)),
                  pl.BlockSpec((1, tn), lambda j: (0, j))],
        out_specs=pl.BlockSpec((bsz, tn), lambda j: (0, j)),
        out_shape=jax.ShapeDtypeStruct((bsz, n), F32),
        compiler_params=pltpu.CompilerParams(
            dimension_semantics=("arbitrary",), vmem_limit_bytes=VMEM_LIMIT_BYTES),
        name="ada",
    )(c, w_ada, b_ada.reshape(1, n))


SLOT_HQ, SLOT_HI, SLOT_HG, SLOT_AQ1, SLOT_AQ2, SLOT_AK1, SLOT_AK2 = 0, 4, 8, 12, 16, 20, 24
N_SLABS = 28
GRP_HQ, GRP_HF, GRP_HI, GRP_HG, GRP_AQ, GRP_AK = range(6)


def _bf16_pieces(value, n):
    rest = np.float32(value)
    pieces = []
    for _ in range(n):
        piece = np.float32(np.asarray(rest, dtype=BF16))
        pieces.append(float(piece))
        rest = np.float32(rest - piece)
    return pieces


def _aug_lanes(lane, values, first_map):
    base = DA_HEAD_DIM if first_map else 0
    aug = jnp.zeros(lane.shape, F32)
    for n, e in enumerate(values):
        aug = jnp.where(lane == base + n, e, aug)
    return aug


def _inproj_kernel(x_ref, mod_ref, n1_ref, w_ref, wvt_ref, oa_ref, of_ref, ovt_ref):
    tm = x_ref.shape[1]
    x = x_ref[0]
    m = mod_ref[0]
    y = x * _rms_scale(x) * n1_ref[...]
    u = (y * (1.0 + m[1:2]) + m[0:1]).astype(BF16)

    lane = lax.broadcasted_iota(jnp.int32, (tm, LANES), 1)
    keep = (lane < DA_HEAD_DIM, lane >= DA_HEAD_DIM)
    maps = ((SLOT_AQ1, SLOT_AK1, True), (SLOT_AQ2, SLOT_AK2, False))
    pos = (pl.program_id(1) * tm + lax.broadcasted_iota(jnp.int32, (tm, LANES), 0)).astype(F32)
    pos_hi = pos.astype(BF16).astype(F32)
    pos_pieces = [pos_hi, pos - pos_hi]
    k_aug = [_aug_lanes(lane, pos_pieces * N_SLOPE_PIECES, first) for _, _, first in maps]
    q_aug = []
    for head in range(DA_HEADS):
        c_lanes = [c for c in _bf16_pieces(LOG2_E * _alibi_slope(head), N_SLOPE_PIECES)
                   for _ in pos_pieces]
        q_aug.append([_aug_lanes(lane, c_lanes, first) for _, _, first in maps])

    for j in range(w_ref.shape[1] // (2 * LANES)):
        p = jnp.dot(u, w_ref[:, j * 2 * LANES:(j + 1) * 2 * LANES], preferred_element_type=F32)
        for half in range(2):
            ph = p[:, half * LANES:(half + 1) * LANES]
            grp, head = divmod(2 * j + half, 4)
            if grp == GRP_HF:
                of_ref[0, head] = ph
            elif grp == GRP_AQ:
                ph = ph * (LOG2_E * DA_HEAD_DIM ** -0.5)
                for mp, (q_slot, _, _) in enumerate(maps):
                    oa_ref[0, q_slot + head] = jnp.where(keep[mp], ph, q_aug[head][mp]).astype(BF16)
            elif grp == GRP_AK:
                for mp, (_, k_slot, _) in enumerate(maps):
                    oa_ref[0, k_slot + head] = jnp.where(keep[mp], ph, k_aug[mp]).astype(BF16)
            else:
                slot = {GRP_HQ: SLOT_HQ, GRP_HI: SLOT_HI, GRP_HG: SLOT_HG}[grp]
                oa_ref[0, slot + head] = ph.astype(BF16)
    vt = _nt_dot(wvt_ref[...], u)
    for head in range(DA_HEADS):
        for jt in range(tm // ATT_T):
            ovt_ref[0, head, jt] = vt[head * LANES:(head + 1) * LANES,
                                      jt * ATT_T:(jt + 1) * ATT_T].astype(BF16)


def _inproj(x, mod3, norm1_w, w_main_bf, w_vt_bf):
    bsz, t, d = x.shape
    n = w_main_bf.shape[1]
    tm = INPROJ_TM
    return pl.pallas_call(
        _inproj_kernel,
        grid=(bsz, t // tm),
        in_specs=[pl.BlockSpec((1, tm, d), lambda b, i: (b, i, 0)),
                  pl.BlockSpec((1, N_MOD, d), lambda b, i: (b, 0, 0)),
                  pl.BlockSpec((1, d), lambda b, i: (0, 0)),
                  pl.BlockSpec((d, n), lambda b, i: (0, 0), pipeline_mode=pl.Buffered(1)),
                  pl.BlockSpec(w_vt_bf.shape, lambda b, i: (0, 0), pipeline_mode=pl.Buffered(1))],
        out_specs=[pl.BlockSpec((1, N_SLABS, tm, LANES), lambda b, i: (b, 0, i, 0)),
                   pl.BlockSpec((1, HG_HEADS, tm, LANES), lambda b, i: (b, 0, i, 0)),
                   pl.BlockSpec((1, DA_HEADS, tm // ATT_T, LANES, ATT_T),
                                lambda b, i: (b, 0, i, 0, 0))],
        out_shape=[jax.ShapeDtypeStruct((bsz, N_SLABS, t, LANES), BF16),
                   jax.ShapeDtypeStruct((bsz, HG_HEADS, t, LANES), F32),
                   jax.ShapeDtypeStruct((bsz, DA_HEADS, t // ATT_T, LANES, ATT_T), BF16)],
        compiler_params=pltpu.CompilerParams(
            dimension_semantics=("arbitrary", "arbitrary"), vmem_limit_bytes=VMEM_LIMIT_BYTES),
        name="inproj",
    )(x, mod3, norm1_w.reshape(1, d), w_main_bf, w_vt_bf)


def _hgrn_diag_scores(a_blks, q_blks, b_blks, c_rows, col0):
    sub, chunk = a_blks[0].shape
    n_sl = sub // SUBLANES
    row = lax.broadcasted_iota(jnp.int32, (SUBLANES, chunk), 0)
    lane = lax.broadcasted_iota(jnp.int32, (SUBLANES, chunk), 1)
    sl = [slice(r * SUBLANES, (r + 1) * SUBLANES) for r in range(n_sl)]
    parts = [[a[sl[r]] for r in range(n_sl)] for a in a_blks]
    for s in range(sub):
        first = s // SUBLANES
        here = lane == col0 + s
        here_diag = here & (row >= s - first * SUBLANES)
        for h, (qb, bb) in enumerate(zip(q_blks, b_blks)):
            cs = c_rows(h, s)
            for r in range(first, n_sl):
                col = jnp.sum(qb[sl[r]] * jnp.exp2(bb[sl[r]] - cs), axis=-1, keepdims=True)
                parts[h][r] = jnp.where(here_diag if r == first else here, col, parts[h][r])
    return [jnp.concatenate(p, axis=0) if n_sl > 1 else p[0] for p in parts]


def _hgrn_decays(lb, tri, f_logit, bkc_ref):
    f = lb + (1.0 - lb) * jax.nn.sigmoid(f_logit)
    logf = jnp.log2(f)
    k = 1.0 - f
    h1 = logf.astype(BF16)
    r1 = logf - h1.astype(F32)
    h2 = r1.astype(BF16)
    h3 = (r1 - h2.astype(F32)).astype(BF16)
    cs = jnp.dot(tri, jnp.concatenate([h1, h2, h3], axis=1), preferred_element_type=F32)
    b = cs[:, :LANES] + cs[:, LANES:2 * LANES] + cs[:, 2 * LANES:]
    bkc_ref[0] = b
    bkc_ref[1] = k
    bkc_ref[2] = b - jnp.log2(k)


def _hgrn_intra(qs, bkc_ref, a_ref, qdec_ref, kend_ref, dec_ref, bounded_decay):
    c_len, sub = HGRN_CHUNK, HGRN_SUB
    n_heads = len(qs)
    bs = [bkc_ref[h, 0] for h in range(n_heads)]
    ks = [bkc_ref[h, 1] for h in range(n_heads)]
    a_rows = [[] for _ in range(n_heads)]
    causal = (lax.broadcasted_iota(jnp.int32, (sub, c_len), 1)
              - lax.broadcasted_iota(jnp.int32, (sub, c_len), 0))
    for i in range(c_len // sub):
        rows = slice(i * sub, (i + 1) * sub)
        n_cols = (i + 1) * sub if bounded_decay else i * sub
        a_blks = []
        for q, b, k in zip(qs, bs, ks):
            if n_cols == 0:
                a_blks.append(jnp.zeros((sub, c_len), F32))
                continue
            ref = b[i * sub - 1:i * sub] if i else jnp.zeros((1, LANES), F32)
            qt = (q[rows] * jnp.exp2(b[rows] - ref)).astype(BF16)
            kt = k[:n_cols] * jnp.exp2(ref - b[:n_cols])
            if n_cols < c_len:
                kt = jnp.concatenate([kt, jnp.zeros((c_len - n_cols, LANES), F32)], axis=0)
            a_blks.append(_nt_dot(qt, kt.astype(BF16)))
        if bounded_decay:
            scored = [jnp.where(causal <= i * sub, a, 0.0) for a in a_blks]
        else:
            c_rows = lambda h, s, i=i: bkc_ref[h, 2, pl.ds(i * sub + s, SUBLANES, stride=0), :]
            scored = _hgrn_diag_scores(a_blks, [q[rows] for q in qs], [b[rows] for b in bs],
                                       c_rows, i * sub)
        for h in range(n_heads):
            a_rows[h].append(scored[h])
    for h, (q, b, k) in enumerate(zip(qs, bs, ks)):
        b_last = b[c_len - 1:c_len]
        a_ref[h] = jnp.concatenate(a_rows[h], axis=0).astype(BF16)
        qdec_ref[h] = (q * jnp.exp2(b)).astype(BF16)
        kend_ref[h] = (k * jnp.exp2(b_last - b)).astype(BF16)
        dec_ref[h] = jnp.exp2(b_last)


def _hgrn_state(gw, v_bf, g, st, a_ref, qdec_ref, kend_ref, dec_ref):
    o = (jnp.dot(a_ref[...], v_bf, preferred_element_type=F32)
         + _nt_dot(qdec_ref[...], st.astype(BF16)))
    y = o * _rms_scale(o) * gw
    out = (y * (g * jax.nn.sigmoid(g))).astype(BF16)
    st_new = st * dec_ref[...] + jnp.dot(v_bf.astype(F32).T.astype(BF16), kend_ref[...],
                                         preferred_element_type=F32)
    return out, st_new


def _hgrn_kernel(lbl_ref, gw_ref, q_ref, f_ref, i_ref, g_ref, o_ref, st_ref, bkc_ref, a_ref,
                 qdec_ref, kend_ref, dec_ref):
    c_len = HGRN_CHUNK
    n_heads, t_len = q_ref.shape[1], q_ref.shape[2]
    n_chunks = t_len // c_len
    assert n_chunks % 2 == 0
    l = lbl_ref[...]
    e = jnp.exp(l - jnp.max(l, axis=0, keepdims=True))
    lb_all = jnp.sum(e[:LAYER + 1], axis=0, keepdims=True) / jnp.sum(e, axis=0, keepdims=True)
    gw = gw_ref[...]
    r_i = lax.broadcasted_iota(jnp.int32, (c_len, c_len), 0)
    c_i = lax.broadcasted_iota(jnp.int32, (c_len, c_len), 1)
    tri = (r_i >= c_i).astype(BF16)
    st_ref[...] = jnp.zeros_like(st_ref)

    def rows_of(c):
        c = jnp.minimum(c, n_chunks - 1)
        return pl.ds(pl.multiple_of(c * c_len, c_len), c_len)

    def stage0(c, par):
        rows = rows_of(c)
        for h in range(n_heads):
            _hgrn_decays(lb_all[:, h * LANES:(h + 1) * LANES], tri, f_ref[0, h, rows, :],
                         bkc_ref.at[par, h])

    def stage1(c, par, bounded_decay):
        rows = rows_of(c)
        _hgrn_intra([q_ref[0, h, rows, :].astype(F32) for h in range(n_heads)], bkc_ref.at[par],
                    a_ref.at[par], qdec_ref.at[par], kend_ref.at[par], dec_ref.at[par],
                    bounded_decay)

    def stage2(c, par):
        rows = rows_of(c)
        for h in range(n_heads):
            out, st_new = _hgrn_state(gw, i_ref[0, h, rows, :], g_ref[0, h, rows, :].astype(F32),
                                      st_ref[h], a_ref.at[par, h], qdec_ref.at[par, h],
                                      kend_ref.at[par, h], dec_ref.at[par, h])
            o_ref[0, rows, h * LANES:(h + 1) * LANES] = out
            st_ref[h] = st_new

    def run(bounded_decay):
        stage0(0, 0)
        stage1(0, 0, bounded_decay)
        stage0(1, 1)

        def pair(p, carry):
            c = 2 * p
            stage0(c + 2, 0)
            stage2(c, 0)
            stage1(c + 1, 1, bounded_decay)
            stage0(c + 3, 1)
            stage2(c + 1, 1)
            stage1(c + 2, 0, bounded_decay)
            return carry

        lax.fori_loop(0, n_chunks // 2, pair, 0)

    bounded = HGRN_SUB * jnp.max(-jnp.log2(lb_all)) <= HGRN_MAX_SUB_DECAY_LOG2
    pl.when(bounded)(lambda: run(True))
    pl.when(jnp.logical_not(bounded))(lambda: run(False))


def _hgrn(oa, of, lb_logits, gnorm_w):
    bsz, _, t, _ = oa.shape
    width = HG_HEADS * LANES
    slab = lambda slot: pl.BlockSpec((1, HG_HEADS, t, LANES), lambda b: (b, slot // HG_HEADS, 0, 0))
    return pl.pallas_call(
        _hgrn_kernel,
        grid=(bsz,),
        in_specs=[pl.BlockSpec((lb_logits.shape[0], width), lambda b: (0, 0)),
                  pl.BlockSpec((1, LANES), lambda b: (0, 0)),
                  slab(SLOT_HQ),
                  pl.BlockSpec((1, HG_HEADS, t, LANES), lambda b: (b, 0, 0, 0)),
                  slab(SLOT_HI), slab(SLOT_HG)],
        out_specs=pl.BlockSpec((1, t, width), lambda b: (b, 0, 0)),
        out_shape=jax.ShapeDtypeStruct((bsz, t, width), BF16),
        scratch_shapes=[pltpu.VMEM((HG_HEADS, HG_DIM, HG_DIM), F32),
                        pltpu.VMEM((2, HG_HEADS, 3, HGRN_CHUNK, LANES), F32),
                        pltpu.VMEM((2, HG_HEADS, HGRN_CHUNK, HGRN_CHUNK), BF16),
                        pltpu.VMEM((2, HG_HEADS, HGRN_CHUNK, LANES), BF16),
                        pltpu.VMEM((2, HG_HEADS, HGRN_CHUNK, LANES), BF16),
                        pltpu.VMEM((2, HG_HEADS, 1, LANES), F32)],
        compiler_params=pltpu.CompilerParams(
            dimension_semantics=("arbitrary",), vmem_limit_bytes=VMEM_LIMIT_BYTES),
        name="hgrn",
    )(lb_logits, gnorm_w.reshape(1, LANES), oa, of, oa, oa)


def _attn_kernel(lq1_ref, lk1_ref, lq2_ref, lk2_ref, sw_ref, q1_ref, q2_ref, k1_ref, k2_ref,
                 vt_ref, o_ref, s_ref, mx_ref):
    t = ATT_T
    nq = q1_ref.shape[2] // t
    lam = (jnp.exp(jnp.sum(lq1_ref[...] * lk1_ref[...], axis=-1, keepdims=True))
           - jnp.exp(jnp.sum(lq2_ref[...] * lk2_ref[...], axis=-1, keepdims=True))
           + LAMBDA_INIT)
    q_refs = (q1_ref, q2_ref)
    k_refs = (k1_ref, k2_ref)
    ones = jnp.ones((BF16_ROWS, t), BF16)
    causal = (lax.broadcasted_iota(jnp.int32, (t, t), 0)
              <= lax.broadcasted_iota(jnp.int32, (t, t), 1))
    tiles = [(qi, j) for qi in range(nq) for j in range(qi + 1)]
    n_buf = ATT_LOOKAHEAD + 1

    def key_max(s):
        quarter = t // 4
        parts = [s[i * quarter:(i + 1) * quarter] for i in range(4)]
        part = jnp.maximum(jnp.maximum(parts[0], parts[1]), jnp.maximum(parts[2], parts[3]))
        return jnp.max(part, axis=0, keepdims=True)

    def produce(n):
        qi, j = tiles[n]
        for mp in range(2):
            s = _nt_dot(k_refs[mp][0, 0, j * t:(j + 1) * t, :],
                        q_refs[mp][0, 0, qi * t:(qi + 1) * t, :])
            if j == qi:
                s = jnp.where(causal, s, NEG_BIG)
            s_ref[n % n_buf, mp] = s
            mx_ref[n % n_buf, mp] = key_max(s)

    def finalize(qi, stats):
        (_, acc1), (_, acc2) = stats
        o = (acc1[:LANES] * (1.0 / acc1[LANES:LANES + 1])
             - acc2[:LANES] * (lam / acc2[LANES:LANES + 1]))
        y = o * lax.rsqrt(jnp.mean(o * o, axis=0, keepdims=True) + EPS) * sw_ref[...]
        o_ref[0, qi * t:(qi + 1) * t, :] = (y * (1.0 - LAMBDA_INIT)).T.astype(o_ref.dtype)

    for n in range(min(ATT_LOOKAHEAD, len(tiles))):
        produce(n)
    stats = None
    for n, (qi, j) in enumerate(tiles):
        if n + ATT_LOOKAHEAD < len(tiles):
            produce(n + ATT_LOOKAHEAD)
        vta = jnp.concatenate([vt_ref[0, 0, j], ones], axis=0)
        new_stats = []
        for mp in range(2):
            s = s_ref[n % n_buf, mp]
            mx = mx_ref[n % n_buf, mp]
            if j == 0:
                m_new = mx
                acc = jnp.dot(vta, jnp.exp2(s - m_new).astype(BF16), preferred_element_type=F32)
            else:
                m_old, acc_old = stats[mp]
                m_new = jnp.maximum(m_old, mx)
                acc = (jnp.exp2(m_old - m_new) * acc_old
                       + jnp.dot(vta, jnp.exp2(s - m_new).astype(BF16),
                                 preferred_element_type=F32))
            new_stats.append((m_new, acc))
        stats = new_stats
        if j == qi:
            finalize(qi, stats)


def _attn(oa, ovt, lq1, lk1, lq2, lk2, subln_w):
    bsz, _, t, _ = oa.shape
    tq = ATT_T
    vec = lambda n: pl.BlockSpec((1, n), lambda b, h: (0, 0))
    slab = lambda slot: pl.BlockSpec((1, 1, t, LANES), lambda b, h: (b, slot + h, 0, 0))
    return pl.pallas_call(
        _attn_kernel,
        grid=(bsz, DA_HEADS),
        in_specs=[vec(DA_HEAD_DIM), vec(DA_HEAD_DIM), vec(DA_HEAD_DIM), vec(DA_HEAD_DIM),
                  pl.BlockSpec((2 * DA_HEAD_DIM, 1), lambda b, h: (0, 0)),
                  slab(SLOT_AQ1), slab(SLOT_AQ2), slab(SLOT_AK1), slab(SLOT_AK2),
                  pl.BlockSpec((1, 1, t // tq, LANES, tq), lambda b, h: (b, h, 0, 0, 0))],
        out_specs=pl.BlockSpec((1, t, LANES), lambda b, h: (b, 0, h)),
        out_shape=jax.ShapeDtypeStruct((bsz, t, DA_HEADS * LANES), BF16),
        scratch_shapes=[pltpu.VMEM((ATT_LOOKAHEAD + 1, 2, tq, tq), F32),
                        pltpu.VMEM((ATT_LOOKAHEAD + 1, 2, 1, tq), F32)],
        compiler_params=pltpu.CompilerParams(
            dimension_semantics=("arbitrary", "arbitrary"), vmem_limit_bytes=VMEM_LIMIT_BYTES),
        name="attn",
    )(lq1, lk1, lq2, lk2, subln_w.reshape(2 * DA_HEAD_DIM, 1), oa, oa, oa, oa, ovt)


def _ffn_kernel(x_ref, mod_ref, ohg_ref, oda_ref, wo_ref, n2_ref, wg_ref, wu_ref, wd_ref, fw_ref,
                o_ref):
    x = x_ref[0]
    m = mod_ref[0]
    hgw = ohg_ref.shape[2]
    mix = (jnp.dot(ohg_ref[0], wo_ref[:hgw], preferred_element_type=F32)
           + jnp.dot(oda_ref[0], wo_ref[hgw:], preferred_element_type=F32))
    hcur = x + m[2:3] * mix
    u = (hcur * _rms_scale(hcur) * n2_ref[...] * (1.0 + m[4:5]) + m[3:4]).astype(BF16)
    acc = jnp.zeros(x.shape, F32)
    for c in range(wg_ref.shape[1] // FFN_FC):
        sl = slice(c * FFN_FC, (c + 1) * FFN_FC)
        gate = jnp.dot(u, wg_ref[:, sl], preferred_element_type=F32)
        up = jnp.dot(u, wu_ref[:, sl], preferred_element_type=F32)
        act = (gate * jax.nn.sigmoid(gate) * up).astype(BF16)
        acc = acc + jnp.dot(act, wd_ref[sl, :], preferred_element_type=F32)
    hcur = hcur + m[5:6] * acc
    o_ref[0] = hcur * _rms_scale(hcur) * fw_ref[...]


def _ffn(x, mod3, o_hg, o_da, w_out_bf, norm2_w, wg_bf, wu_bf, wd_bf, final_w):
    bsz, t, d = x.shape
    tm = FFN_TM
    dff = wg_bf.shape[1]
    assert dff % FFN_FC == 0
    tile = lambda w: pl.BlockSpec((1, tm, w), lambda b, i: (b, i, 0))
    const = lambda r, c: pl.BlockSpec((r, c), lambda b, i: (0, 0), pipeline_mode=pl.Buffered(1))
    return pl.pallas_call(
        _ffn_kernel,
        grid=(bsz, t // tm),
        in_specs=[tile(d),
                  pl.BlockSpec((1, N_MOD, d), lambda b, i: (b, 0, 0)),
                  tile(o_hg.shape[2]), tile(o_da.shape[2]),
                  const(d, d), const(1, d), const(d, dff), const(d, dff), const(dff, d),
                  const(1, d)],
        out_specs=tile(d),
        out_shape=jax.ShapeDtypeStruct((bsz, t, d), F32),
        compiler_params=pltpu.CompilerParams(
            dimension_semantics=("arbitrary", "arbitrary"), vmem_limit_bytes=VMEM_LIMIT_BYTES),
        name="ffn",
    )(x, mod3, o_hg, o_da, w_out_bf, norm2_w.reshape(1, d), wg_bf, wu_bf, wd_bf,
      final_w.reshape(1, d))


def kernel(x, c, w_ada, b_ada, norm1_w, w_in, hgrn_lb_logits, hgrn_gnorm_w, diff_lambda_q1,
           diff_lambda_k1, diff_lambda_q2, diff_lambda_k2, diff_subln_w, w_out, norm2_w,
           w_ffn_gate, w_ffn_up, w_ffn_down, final_norm_w):
    bsz, t, d = x.shape
    assert w_ada.shape[0] == 1, "single-layer trunk"
    l = LAYER
    mod3 = _ada(c, w_ada[l], b_ada[l]).reshape(bsz, N_MOD, d)
    n_main = w_in.shape[2] - DA_HEADS * 2 * DA_HEAD_DIM
    oa, of, ovt = _inproj(x, mod3, norm1_w[l], w_in[l, :, :n_main].astype(BF16),
                          w_in[l, :, n_main:].T.astype(BF16))
    o_hg = _hgrn(oa, of, hgrn_lb_logits, hgrn_gnorm_w[l])
    o_da = _attn(oa, ovt, diff_lambda_q1[l:l + 1], diff_lambda_k1[l:l + 1],
                 diff_lambda_q2[l:l + 1], diff_lambda_k2[l:l + 1], diff_subln_w[l])
    return _ffn(x, mod3, o_hg, o_da, w_out[l].astype(BF16), norm2_w[l],
                w_ffn_gate[l].astype(BF16), w_ffn_up[l].astype(BF16),
                w_ffn_down[l].astype(BF16), final_norm_w)
```

```python
import math

import jax
import jax.numpy as jnp
import numpy as np
from jax import lax
from jax.experimental import pallas as pl
from jax.experimental.pallas import tpu as pltpu

F32 = jnp.float32
BF16 = jnp.bfloat16

HG_HEADS = 4
HG_DIM = 128
DA_HEADS = 4
DA_HEAD_DIM = 64
N_MOD = 6
EPS = 1e-6
LAYER = 0
LAMBDA_INIT = 0.8 - 0.6 * math.exp(-0.3 * LAYER)
LOG2_E = math.log2(math.e)

LANES = 128
SUBLANES = 8
BF16_ROWS = 16
VMEM_LIMIT_BYTES = 56 * 1024 * 1024

HGRN_CHUNK = 64
HGRN_SUB = 8
HGRN_MAX_SUB_DECAY_LOG2 = 64.0
ADA_TN = 768
INPROJ_TM = 512
FFN_TM = 512
FFN_FC = 256
ATT_T = 256
ATT_LOOKAHEAD = 2
NEG_BIG = -1e30
N_SLOPE_PIECES = 3


def _nt_dot(a, b):
    return lax.dot_general(a, b, (((1,), (1,)), ((), ())), preferred_element_type=F32)


def _rms_scale(x):
    return lax.rsqrt(jnp.mean(x * x, axis=-1, keepdims=True) + EPS)


def _alibi_slope(head):
    return 2.0 ** (-8.0 * (head + 1) / DA_HEADS)


def _ada_kernel(c_ref, w_ref, b_ref, o_ref):
    c = c_ref[...]
    a = c * jax.nn.sigmoid(c)
    o_ref[...] = jnp.dot(a, w_ref[...], preferred_element_type=F32,
                         precision=lax.Precision.HIGHEST) + b_ref[...]


def _ada(c, w_ada, b_ada):
    bsz, d = c.shape
    n = w_ada.shape[1]
    tn = ADA_TN
    return pl.pallas_call(
        _ada_kernel,
        grid=(n // tn,),
        in_specs=[pl.BlockSpec((bsz, d), lambda j: (0, 0)),
                  pl.BlockSpec((d, tn), lambda j: (0, j)),
                  pl.BlockSpec((1, tn), lambda j: (0, j))],
        out_specs=pl.BlockSpec((bsz, tn), lambda j: (0, j)),
        out_shape=jax.ShapeDtypeStruct((bsz, n), F32),
        compiler_params=pltpu.CompilerParams(
            dimension_semantics=("arbitrary",), vmem_limit_bytes=VMEM_LIMIT_BYTES),
        name="ada",
    )(c, w_ada, b_ada.reshape(1, n))


SLOT_HQ, SLOT_HI, SLOT_HG, SLOT_AQ1, SLOT_AQ2, SLOT_AK1, SLOT_AK2 = 0, 4, 8, 12, 16, 20, 24
N_SLABS = 28
GRP_HQ, GRP_HF, GRP_HI, GRP_HG, GRP_AQ, GRP_AK, GRP_AV = range(7)


def _bf16_pieces(value, n):
    rest = np.float32(value)
    pieces = []
    for _ in range(n):
        piece = np.float32(np.asarray(rest, dtype=BF16))
        pieces.append(float(piece))
        rest = np.float32(rest - piece)
    return pieces


def _aug_lanes(lane, values, first_map):
    base = DA_HEAD_DIM if first_map else 0
    aug = jnp.zeros(lane.shape, F32)
    for n, e in enumerate(values):
        aug = jnp.where(lane == base + n, e, aug)
    return aug


def _inproj_kernel(x_ref, mod_ref, n1_ref, w_ref, oa_ref, of_ref, ovt_ref):
    tm = x_ref.shape[1]
    x = x_ref[0]
    m = mod_ref[0]
    y = x * _rms_scale(x) * n1_ref[...]
    u = (y * (1.0 + m[1:2]) + m[0:1]).astype(BF16)

    lane = lax.broadcasted_iota(jnp.int32, (tm, LANES), 1)
    keep = (lane < DA_HEAD_DIM, lane >= DA_HEAD_DIM)
    maps = ((SLOT_AQ1, SLOT_AK1, True), (SLOT_AQ2, SLOT_AK2, False))
    pos = (pl.program_id(1) * tm + lax.broadcasted_iota(jnp.int32, (tm, LANES), 0)).astype(F32)
    pos_hi = pos.astype(BF16).astype(F32)
    pos_pieces = [pos_hi, pos - pos_hi]
    k_aug = [_aug_lanes(lane, pos_pieces * N_SLOPE_PIECES, first) for _, _, first in maps]
    q_aug = []
    for head in range(DA_HEADS):
        c_lanes = [c for c in _bf16_pieces(LOG2_E * _alibi_slope(head), N_SLOPE_PIECES)
                   for _ in pos_pieces]
        q_aug.append([_aug_lanes(lane, c_lanes, first) for _, _, first in maps])

    for j in range(w_ref.shape[1] // (2 * LANES)):
        p = jnp.dot(u, w_ref[:, j * 2 * LANES:(j + 1) * 2 * LANES], preferred_element_type=F32)
        for half in range(2):
            ph = p[:, half * LANES:(half + 1) * LANES]
            grp, head = divmod(2 * j + half, 4)
            if grp == GRP_HF:
                of_ref[0, head] = ph
            elif grp == GRP_AQ:
                ph = ph * (LOG2_E * DA_HEAD_DIM ** -0.5)
                for mp, (q_slot, _, _) in enumerate(maps):
                    oa_ref[0, q_slot + head] = jnp.where(keep[mp], ph, q_aug[head][mp]).astype(BF16)
            elif grp == GRP_AK:
                for mp, (_, k_slot, _) in enumerate(maps):
                    oa_ref[0, k_slot + head] = jnp.where(keep[mp], ph, k_aug[mp]).astype(BF16)
            elif grp == GRP_AV:
                vt = ph.T
                for jt in range(tm // ATT_T):
                    ovt_ref[0, head, jt] = vt[:, jt * ATT_T:(jt + 1) * ATT_T].astype(BF16)
            else:
                slot = {GRP_HQ: SLOT_HQ, GRP_HI: SLOT_HI, GRP_HG: SLOT_HG}[grp]
                oa_ref[0, slot + head] = ph.astype(BF16)


def _inproj(x, mod3, norm1_w, w_bf):
    bsz, t, d = x.shape
    n = w_bf.shape[1]
    tm = INPROJ_TM
    return pl.pallas_call(
        _inproj_kernel,
        grid=(bsz, t // tm),
        in_specs=[pl.BlockSpec((1, tm, d), lambda b, i: (b, i, 0)),
                  pl.BlockSpec((1, N_MOD, d), lambda b, i: (b, 0, 0)),
                  pl.BlockSpec((1, d), lambda b, i: (0, 0)),
                  pl.BlockSpec((d, n), lambda b, i: (0, 0), pipeline_mode=pl.Buffered(1))],
        out_specs=[pl.BlockSpec((1, N_SLABS, tm, LANES), lambda b, i: (b, 0, i, 0)),
                   pl.BlockSpec((1, HG_HEADS, tm, LANES), lambda b, i: (b, 0, i, 0)),
                   pl.BlockSpec((1, DA_HEADS, tm // ATT_T, LANES, ATT_T),
                                lambda b, i: (b, 0, i, 0, 0))],
        out_shape=[jax.ShapeDtypeStruct((bsz, N_SLABS, t, LANES), BF16),
                   jax.ShapeDtypeStruct((bsz, HG_HEADS, t, LANES), F32),
                   jax.ShapeDtypeStruct((bsz, DA_HEADS, t // ATT_T, LANES, ATT_T), BF16)],
        compiler_params=pltpu.CompilerParams(
            dimension_semantics=("arbitrary", "arbitrary"), vmem_limit_bytes=VMEM_LIMIT_BYTES),
        name="inproj",
    )(x, mod3, norm1_w.reshape(1, d), w_bf)


def _hgrn_diag_scores(a_blks, q_blks, b_blks, c_rows, col0):
    sub, chunk = a_blks[0].shape
    n_sl = sub // SUBLANES
    row = lax.broadcasted_iota(jnp.int32, (SUBLANES, chunk), 0)
    lane = lax.broadcasted_iota(jnp.int32, (SUBLANES, chunk), 1)
    sl = [slice(r * SUBLANES, (r + 1) * SUBLANES) for r in range(n_sl)]
    parts = [[a[sl[r]] for r in range(n_sl)] for a in a_blks]
    for s in range(sub):
        first = s // SUBLANES
        here = lane == col0 + s
        here_diag = here & (row >= s - first * SUBLANES)
        for h, (qb, bb) in enumerate(zip(q_blks, b_blks)):
            cs = c_rows(h, s)
            for r in range(first, n_sl):
                col = jnp.sum(qb[sl[r]] * jnp.exp2(bb[sl[r]] - cs), axis=-1, keepdims=True)
                parts[h][r] = jnp.where(here_diag if r == first else here, col, parts[h][r])
    return [jnp.concatenate(p, axis=0) if n_sl > 1 else p[0] for p in parts]


def _hgrn_decays(lb, tri, f_logit, bkc_ref):
    f = lb + (1.0 - lb) * jax.nn.sigmoid(f_logit)
    logf = jnp.log2(f)
    k = 1.0 - f
    h1 = logf.astype(BF16)
    r1 = logf - h1.astype(F32)
    h2 = r1.astype(BF16)
    h3 = (r1 - h2.astype(F32)).astype(BF16)
    cs = jnp.dot(tri, jnp.concatenate([h1, h2, h3], axis=1), preferred_element_type=F32)
    b = cs[:, :LANES] + cs[:, LANES:2 * LANES] + cs[:, 2 * LANES:]
    bkc_ref[0] = b
    bkc_ref[1] = k
    bkc_ref[2] = b - jnp.log2(k)


def _hgrn_intra(qs, bkc_ref, a_ref, qdec_ref, kend_ref, dec_ref, bounded_decay):
    c_len, sub = HGRN_CHUNK, HGRN_SUB
    n_heads = len(qs)
    bs = [bkc_ref[h, 0] for h in range(n_heads)]
    ks = [bkc_ref[h, 1] for h in range(n_heads)]
    a_rows = [[] for _ in range(n_heads)]
    causal = (lax.broadcasted_iota(jnp.int32, (sub, c_len), 1)
              - lax.broadcasted_iota(jnp.int32, (sub, c_len), 0))
    for i in range(c_len // sub):
        rows = slice(i * sub, (i + 1) * sub)
        n_cols = (i + 1) * sub if bounded_decay else i * sub
        a_blks = []
        for q, b, k in zip(qs, bs, ks):
            if n_cols == 0:
                a_blks.append(jnp.zeros((sub, c_len), F32))
                continue
            ref = b[i * sub - 1:i * sub] if i else jnp.zeros((1, LANES), F32)
            qt = (q[rows] * jnp.exp2(b[rows] - ref)).astype(BF16)
            kt = k[:n_cols] * jnp.exp2(ref - b[:n_cols])
            if n_cols < c_len:
                kt = jnp.concatenate([kt, jnp.zeros((c_len - n_cols, LANES), F32)], axis=0)
            a_blks.append(_nt_dot(qt, kt.astype(BF16)))
        if bounded_decay:
            scored = [jnp.where(causal <= i * sub, a, 0.0) for a in a_blks]
        else:
            c_rows = lambda h, s, i=i: bkc_ref[h, 2, pl.ds(i * sub + s, SUBLANES, stride=0), :]
            scored = _hgrn_diag_scores(a_blks, [q[rows] for q in qs], [b[rows] for b in bs],
                                       c_rows, i * sub)
        for h in range(n_heads):
            a_rows[h].append(scored[h])
    for h, (q, b, k) in enumerate(zip(qs, bs, ks)):
        b_last = b[c_len - 1:c_len]
        a_ref[h] = jnp.concatenate(a_rows[h], axis=0).astype(BF16)
        qdec_ref[h] = (q * jnp.exp2(b)).astype(BF16)
        kend_ref[h] = (k * jnp.exp2(b_last - b)).astype(BF16)
        dec_ref[h] = jnp.exp2(b_last)


def _hgrn_state(gw, v_bf, g, st, a_ref, qdec_ref, kend_ref, dec_ref):
    o = (jnp.dot(a_ref[...], v_bf, preferred_element_type=F32)
         + _nt_dot(qdec_ref[...], st.astype(BF16)))
    y = o * _rms_scale(o) * gw
    out = (y * (g * jax.nn.sigmoid(g))).astype(BF16)
    st_new = st * dec_ref[...] + jnp.dot(v_bf.astype(F32).T.astype(BF16), kend_ref[...],
                                         preferred_element_type=F32)
    return out, st_new


def _hgrn_kernel(lbl_ref, gw_ref, q_ref, f_ref, i_ref, g_ref, o_ref, st_ref, bkc_ref, a_ref,
                 qdec_ref, kend_ref, dec_ref):
    c_len = HGRN_CHUNK
    n_heads, t_len = q_ref.shape[1], q_ref.shape[2]
    n_chunks = t_len // c_len
    assert n_chunks % 2 == 0
    l = lbl_ref[...]
    e = jnp.exp(l - jnp.max(l, axis=0, keepdims=True))
    lb_all = jnp.sum(e[:LAYER + 1], axis=0, keepdims=True) / jnp.sum(e, axis=0, keepdims=True)
    gw = gw_ref[...]
    r_i = lax.broadcasted_iota(jnp.int32, (c_len, c_len), 0)
    c_i = lax.broadcasted_iota(jnp.int32, (c_len, c_len), 1)
    tri = (r_i >= c_i).astype(BF16)
    st_ref[...] = jnp.zeros_like(st_ref)

    def rows_of(c):
        c = jnp.minimum(c, n_chunks - 1)
        return pl.ds(pl.multiple_of(c * c_len, c_len), c_len)

    def stage0(c, par):
        rows = rows_of(c)
        for h in range(n_heads):
            _hgrn_decays(lb_all[:, h * LANES:(h + 1) * LANES], tri, f_ref[0, h, rows, :],
                         bkc_ref.at[par, h])

    def stage1(c, par, bounded_decay):
        rows = rows_of(c)
        _hgrn_intra([q_ref[0, h, rows, :].astype(F32) for h in range(n_heads)], bkc_ref.at[par],
                    a_ref.at[par], qdec_ref.at[par], kend_ref.at[par], dec_ref.at[par],
                    bounded_decay)

    def stage2(c, par):
        rows = rows_of(c)
        for h in range(n_heads):
            out, st_new = _hgrn_state(gw, i_ref[0, h, rows, :], g_ref[0, h, rows, :].astype(F32),
                                      st_ref[h], a_ref.at[par, h], qdec_ref.at[par, h],
                                      kend_ref.at[par, h], dec_ref.at[par, h])
            o_ref[0, rows, h * LANES:(h + 1) * LANES] = out
            st_ref[h] = st_new

    def run(bounded_decay):
        stage0(0, 0)
        stage1(0, 0, bounded_decay)
        stage0(1, 1)

        def pair(p, carry):
            c = 2 * p
            stage0(c + 2, 0)
            stage2(c, 0)
            stage1(c + 1, 1, bounded_decay)
            stage0(c + 3, 1)
            stage2(c + 1, 1)
            stage1(c + 2, 0, bounded_decay)
            return carry

        lax.fori_loop(0, n_chunks // 2, pair, 0)

    bounded = HGRN_SUB * jnp.max(-jnp.log2(lb_all)) <= HGRN_MAX_SUB_DECAY_LOG2
    pl.when(bounded)(lambda: run(True))
    pl.when(jnp.logical_not(bounded))(lambda: run(False))


def _hgrn(oa, of, lb_logits, gnorm_w):
    bsz, _, t, _ = oa.shape
    width = HG_HEADS * LANES
    slab = lambda slot: pl.BlockSpec((1, HG_HEADS, t, LANES), lambda b: (b, slot // HG_HEADS, 0, 0))
    return pl.pallas_call(
        _hgrn_kernel,
        grid=(bsz,),
        in_specs=[pl.BlockSpec((lb_logits.shape[0], width), lambda b: (0, 0)),
                  pl.BlockSpec((1, LANES), lambda b: (0, 0)),
                  slab(SLOT_HQ),
                  pl.BlockSpec((1, HG_HEADS, t, LANES), lambda b: (b, 0, 0, 0)),
                  slab(SLOT_HI), slab(SLOT_HG)],
        out_specs=pl.BlockSpec((1, t, width), lambda b: (b, 0, 0)),
        out_shape=jax.ShapeDtypeStruct((bsz, t, width), BF16),
        scratch_shapes=[pltpu.VMEM((HG_HEADS, HG_DIM, HG_DIM), F32),
                        pltpu.VMEM((2, HG_HEADS, 3, HGRN_CHUNK, LANES), F32),
                        pltpu.VMEM((2, HG_HEADS, HGRN_CHUNK, HGRN_CHUNK), BF16),
                        pltpu.VMEM((2, HG_HEADS, HGRN_CHUNK, LANES), BF16),
                        pltpu.VMEM((2, HG_HEADS, HGRN_CHUNK, LANES), BF16),
                        pltpu.VMEM((2, HG_HEADS, 1, LANES), F32)],
        compiler_params=pltpu.CompilerParams(
            dimension_semantics=("arbitrary",), vmem_limit_bytes=VMEM_LIMIT_BYTES),
        name="hgrn",
    )(lb_logits, gnorm_w.reshape(1, LANES), oa, of, oa, oa)


def _attn_kernel(lq1_ref, lk1_ref, lq2_ref, lk2_ref, sw_ref, q1_ref, q2_ref, k1_ref, k2_ref,
                 vt_ref, o_ref, s_ref, mx_ref):
    t = ATT_T
    nq = q1_ref.shape[2] // t
    lam = (jnp.exp(jnp.sum(lq1_ref[...] * lk1_ref[...], axis=-1, keepdims=True))
           - jnp.exp(jnp.sum(lq2_ref[...] * lk2_ref[...], axis=-1, keepdims=True))
           + LAMBDA_INIT)
    q_refs = (q1_ref, q2_ref)
    k_refs = (k1_ref, k2_ref)
    ones = jnp.ones((BF16_ROWS, t), BF16)
    causal = (lax.broadcasted_iota(jnp.int32, (t, t), 0)
              <= lax.broadcasted_iota(jnp.int32, (t, t), 1))
    tiles = [(qi, j) for qi in range(nq) for j in range(qi + 1)]
    n_buf = ATT_LOOKAHEAD + 1

    def key_max(s):
        quarter = t // 4
        parts = [s[i * quarter:(i + 1) * quarter] for i in range(4)]
        part = jnp.maximum(jnp.maximum(parts[0], parts[1]), jnp.maximum(parts[2], parts[3]))
        return jnp.max(part, axis=0, keepdims=True)

    def produce(n):
        qi, j = tiles[n]
        for mp in range(2):
            s = _nt_dot(k_refs[mp][0, 0, j * t:(j + 1) * t, :],
                        q_refs[mp][0, 0, qi * t:(qi + 1) * t, :])
            if j == qi:
                s = jnp.where(causal, s, NEG_BIG)
            s_ref[n % n_buf, mp] = s
            mx_ref[n % n_buf, mp] = key_max(s)

    def finalize(qi, stats):
        (_, acc1), (_, acc2) = stats
        o = (acc1[:LANES] * (1.0 / acc1[LANES:LANES + 1])
             - acc2[:LANES] * (lam / acc2[LANES:LANES + 1]))
        y = o * lax.rsqrt(jnp.mean(o * o, axis=0, keepdims=True) + EPS) * sw_ref[...]
        o_ref[0, qi * t:(qi + 1) * t, :] = (y * (1.0 - LAMBDA_INIT)).T.astype(o_ref.dtype)

    for n in range(min(ATT_LOOKAHEAD, len(tiles))):
        produce(n)
    stats = None
    for n, (qi, j) in enumerate(tiles):
        if n + ATT_LOOKAHEAD < len(tiles):
            produce(n + ATT_LOOKAHEAD)
        vta = jnp.concatenate([vt_ref[0, 0, j], ones], axis=0)
        new_stats = []
        for mp in range(2):
            s = s_ref[n % n_buf, mp]
            mx = mx_ref[n % n_buf, mp]
            if j == 0:
                m_new = mx
                acc = jnp.dot(vta, jnp.exp2(s - m_new).astype(BF16), preferred_element_type=F32)
            else:
                m_old, acc_old = stats[mp]
                m_new = jnp.maximum(m_old, mx)
                acc = (jnp.exp2(m_old - m_new) * acc_old
                       + jnp.dot(vta, jnp.exp2(s - m_new).astype(BF16),
                                 preferred_element_type=F32))
            new_stats.append((m_new, acc))
        stats = new_stats
        if j == qi:
            finalize(qi, stats)


def _attn(oa, ovt, lq1, lk1, lq2, lk2, subln_w):
    bsz, _, t, _ = oa.shape
    tq = ATT_T
    vec = lambda n: pl.BlockSpec((1, n), lambda b, h: (0, 0))
    slab = lambda slot: pl.BlockSpec((1, 1, t, LANES), lambda b, h: (b, slot + h, 0, 0))
    return pl.pallas_call(
        _attn_kernel,
        grid=(bsz, DA_HEADS),
        in_specs=[vec(DA_HEAD_DIM), vec(DA_HEAD_DIM), vec(DA_HEAD_DIM), vec(DA_HEAD_DIM),
                  pl.BlockSpec((2 * DA_HEAD_DIM, 1), lambda b, h: (0, 0)),
                  slab(SLOT_AQ1), slab(SLOT_AQ2), slab(SLOT_AK1), slab(SLOT_AK2),
                  pl.BlockSpec((1, 1, t // tq, LANES, tq), lambda b, h: (b, h, 0, 0, 0))],
        out_specs=pl.BlockSpec((1, t, LANES), lambda b, h: (b, 0, h)),
        out_shape=jax.ShapeDtypeStruct((bsz, t, DA_HEADS * LANES), BF16),
        scratch_shapes=[pltpu.VMEM((ATT_LOOKAHEAD + 1, 2, tq, tq), F32),
                        pltpu.VMEM((ATT_LOOKAHEAD + 1, 2, 1, tq), F32)],
        compiler_params=pltpu.CompilerParams(
            dimension_semantics=("arbitrary", "arbitrary"), vmem_limit_bytes=VMEM_LIMIT_BYTES),
        name="attn",
    )(lq1, lk1, lq2, lk2, subln_w.reshape(2 * DA_HEAD_DIM, 1), oa, oa, oa, oa, ovt)


def _ffn_kernel(x_ref, mod_ref, ohg_ref, oda_ref, wo_ref, n2_ref, wg_ref, wu_ref, wd_ref, fw_ref,
                o_ref):
    x = x_ref[0]
    m = mod_ref[0]
    mix = jnp.dot(jnp.concatenate([ohg_ref[0], oda_ref[0]], axis=1), wo_ref[...],
                  preferred_element_type=F32)
    hcur = x + m[2:3] * mix
    u = (hcur * _rms_scale(hcur) * n2_ref[...] * (1.0 + m[4:5]) + m[3:4]).astype(BF16)
    acc = jnp.zeros(x.shape, F32)
    for c in range(wg_ref.shape[1] // FFN_FC):
        sl = slice(c * FFN_FC, (c + 1) * FFN_FC)
        gate = jnp.dot(u, wg_ref[:, sl], preferred_element_type=F32)
        up = jnp.dot(u, wu_ref[:, sl], preferred_element_type=F32)
        act = (gate * jax.nn.sigmoid(gate) * up).astype(BF16)
        acc = acc + jnp.dot(act, wd_ref[sl, :], preferred_element_type=F32)
    hcur = hcur + m[5:6] * acc
    o_ref[0] = hcur * _rms_scale(hcur) * fw_ref[...]


def _ffn(x, mod3, o_hg, o_da, w_out_bf, norm2_w, wg_bf, wu_bf, wd_bf, final_w):
    bsz, t, d = x.shape
    tm = FFN_TM
    dff = wg_bf.shape[1]
    assert dff % FFN_FC == 0
    tile = lambda w: pl.BlockSpec((1, tm, w), lambda b, i: (b, i, 0))
    const = lambda r, c: pl.BlockSpec((r, c), lambda b, i: (0, 0), pipeline_mode=pl.Buffered(1))
    return pl.pallas_call(
        _ffn_kernel,
        grid=(bsz, t // tm),
        in_specs=[tile(d),
                  pl.BlockSpec((1, N_MOD, d), lambda b, i: (b, 0, 0)),
                  tile(o_hg.shape[2]), tile(o_da.shape[2]),
                  const(d, d), const(1, d), const(d, dff), const(d, dff), const(dff, d),
                  const(1, d)],
        out_specs=tile(d),
        out_shape=jax.ShapeDtypeStruct((bsz, t, d), F32),
        compiler_params=pltpu.CompilerParams(
            dimension_semantics=("arbitrary", "arbitrary"), vmem_limit_bytes=VMEM_LIMIT_BYTES),
        name="ffn",
    )(x, mod3, o_hg, o_da, w_out_bf, norm2_w.reshape(1, d), wg_bf, wu_bf, wd_bf,
      final_w.reshape(1, d))


def kernel(x, c, w_ada, b_ada, norm1_w, w_in, hgrn_lb_logits, hgrn_gnorm_w, diff_lambda_q1,
           diff_lambda_k1, diff_lambda_q2, diff_lambda_k2, diff_subln_w, w_out, norm2_w,
           w_ffn_gate, w_ffn_up, w_ffn_down, final_norm_w):
    bsz, t, d = x.shape
    assert w_ada.shape[0] == 1, "single-layer trunk"
    l = LAYER
    mod3 = _ada(c, w_ada[l], b_ada[l]).reshape(bsz, N_MOD, d)
    oa, of, ovt = _inproj(x, mod3, norm1_w[l], w_in[l].astype(BF16))
    o_hg = _hgrn(oa, of, hgrn_lb_logits, hgrn_gnorm_w[l])
    o_da = _attn(oa, ovt, diff_lambda_q1[l:l + 1], diff_lambda_k1[l:l + 1],
                 diff_lambda_q2[l:l + 1], diff_lambda_k2[l:l + 1], diff_subln_w[l])
    return _ffn(x, mod3, o_hg, o_da, w_out[l].astype(BF16), norm2_w[l],
                w_ffn_gate[l].astype(BF16), w_ffn_up[l].astype(BF16),
                w_ffn_down[l].astype(BF16), final_norm_w)
```

```python
import math

import jax
import jax.numpy as jnp
import numpy as np
from jax import lax
from jax.experimental import pallas as pl
from jax.experimental.pallas import tpu as pltpu

F32 = jnp.float32
BF16 = jnp.bfloat16

HG_HEADS = 4
HG_DIM = 128
DA_HEADS = 4
DA_HEAD_DIM = 64
N_MOD = 6
EPS = 1e-6
LAYER = 0
LAMBDA_INIT = 0.8 - 0.6 * math.exp(-0.3 * LAYER)
LOG2_E = math.log2(math.e)

LANES = 128
SUBLANES = 8
BF16_ROWS = 16
VMEM_LIMIT_BYTES = 56 * 1024 * 1024

HGRN_CHUNK = 64
HGRN_SUB = 8
HGRN_MAX_SUB_DECAY_LOG2 = 64.0
ADA_TN = 768
INPROJ_TM = 512
FFN_TM = 512
FFN_FC = 256
ATT_T = 256
ATT_LOOKAHEAD = 2
NEG_BIG = -1e30
N_SLOPE_PIECES = 3


def _nt_dot(a, b):
    return lax.dot_general(a, b, (((1,), (1,)), ((), ())), preferred_element_type=F32)


def _rms_scale(x):
    return lax.rsqrt(jnp.mean(x * x, axis=-1, keepdims=True) + EPS)


def _alibi_slope(head):
    return 2.0 ** (-8.0 * (head + 1) / DA_HEADS)


def _ada_kernel(c_ref, w_ref, b_ref, o_ref):
    c = c_ref[...]
    a = c * jax.nn.sigmoid(c)
    o_ref[...] = jnp.dot(a, w_ref[...], preferred_element_type=F32,
                         precision=lax.Precision.HIGHEST) + b_ref[...]


def _ada(c, w_ada, b_ada):
    bsz, d = c.shape
    n = w_ada.shape[1]
    tn = ADA_TN
    return pl.pallas_call(
        _ada_kernel,
        grid=(n // tn,),
        in_specs=[pl.BlockSpec((bsz, d), lambda j: (0, 0)),
                  pl.BlockSpec((d, tn), lambda j: (0, j)),
                  pl.BlockSpec((1, tn), lambda j: (0, j))],
        out_specs=pl.BlockSpec((bsz, tn), lambda j: (0, j)),
        out_shape=jax.ShapeDtypeStruct((bsz, n), F32),
        compiler_params=pltpu.CompilerParams(
            dimension_semantics=("arbitrary",), vmem_limit_bytes=VMEM_LIMIT_BYTES),
        name="ada",
    )(c, w_ada, b_ada.reshape(1, n))


SLOT_HQ, SLOT_HI, SLOT_HG, SLOT_AQ1, SLOT_AQ2, SLOT_AK1, SLOT_AK2 = 0, 4, 8, 12, 16, 20, 24
N_SLABS = 28
GRP_HQ, GRP_HF, GRP_HI, GRP_HG, GRP_AQ, GRP_AK, GRP_AV = range(7)


def _bf16_pieces(value, n):
    rest = np.float32(value)
    pieces = []
    for _ in range(n):
        piece = np.float32(np.asarray(rest, dtype=BF16))
        pieces.append(float(piece))
        rest = np.float32(rest - piece)
    return pieces


def _aug_lanes(lane, values, first_map):
    base = DA_HEAD_DIM if first_map else 0
    aug = jnp.zeros(lane.shape, F32)
    for n, e in enumerate(values):
        aug = jnp.where(lane == base + n, e, aug)
    return aug


def _inproj_kernel(x_ref, mod_ref, n1_ref, w_ref, oa_ref, of_ref, ovt_ref):
    tm = x_ref.shape[1]
    x = x_ref[0]
    m = mod_ref[0]
    y = x * _rms_scale(x) * n1_ref[...]
    u = (y * (1.0 + m[1:2]) + m[0:1]).astype(BF16)

    lane = lax.broadcasted_iota(jnp.int32, (tm, LANES), 1)
    keep = (lane < DA_HEAD_DIM, lane >= DA_HEAD_DIM)
    maps = ((SLOT_AQ1, SLOT_AK1, True), (SLOT_AQ2, SLOT_AK2, False))
    pos = (pl.program_id(1) * tm + lax.broadcasted_iota(jnp.int32, (tm, LANES), 0)).astype(F32)
    pos_hi = pos.astype(BF16).astype(F32)
    pos_pieces = [pos_hi, pos - pos_hi]
    k_aug = [_aug_lanes(lane, pos_pieces * N_SLOPE_PIECES, first) for _, _, first in maps]
    q_aug = []
    for head in range(DA_HEADS):
        c_lanes = [c for c in _bf16_pieces(LOG2_E * _alibi_slope(head), N_SLOPE_PIECES)
                   for _ in pos_pieces]
        q_aug.append([_aug_lanes(lane, c_lanes, first) for _, _, first in maps])

    for j in range(w_ref.shape[1] // (2 * LANES)):
        p = jnp.dot(u, w_ref[:, j * 2 * LANES:(j + 1) * 2 * LANES], preferred_element_type=F32)
        for half in range(2):
            ph = p[:, half * LANES:(half + 1) * LANES]
            grp, head = divmod(2 * j + half, 4)
            if grp == GRP_HF:
                of_ref[0, head] = ph
            elif grp == GRP_AQ:
                ph = ph * (LOG2_E * DA_HEAD_DIM ** -0.5)
                for mp, (q_slot, _, _) in enumerate(maps):
                    oa_ref[0, q_slot + head] = jnp.where(keep[mp], ph, q_aug[head][mp]).astype(BF16)
            elif grp == GRP_AK:
                for mp, (_, k_slot, _) in enumerate(maps):
                    oa_ref[0, k_slot + head] = jnp.where(keep[mp], ph, k_aug[mp]).astype(BF16)
            elif grp == GRP_AV:
                vt = ph.T
                for jt in range(tm // ATT_T):
                    ovt_ref[0, head, jt] = vt[:, jt * ATT_T:(jt + 1) * ATT_T].astype(BF16)
            else:
                slot = {GRP_HQ: SLOT_HQ, GRP_HI: SLOT_HI, GRP_HG: SLOT_HG}[grp]
                oa_ref[0, slot + head] = ph.astype(BF16)


def _inproj(x, mod3, norm1_w, w_bf):
    bsz, t, d = x.shape
    n = w_bf.shape[1]
    tm = INPROJ_TM
    return pl.pallas_call(
        _inproj_kernel,
        grid=(bsz, t // tm),
        in_specs=[pl.BlockSpec((1, tm, d), lambda b, i: (b, i, 0)),
                  pl.BlockSpec((1, N_MOD, d), lambda b, i: (b, 0, 0)),
                  pl.BlockSpec((1, d), lambda b, i: (0, 0)),
                  pl.BlockSpec((d, n), lambda b, i: (0, 0), pipeline_mode=pl.Buffered(1))],
        out_specs=[pl.BlockSpec((1, N_SLABS, tm, LANES), lambda b, i: (b, 0, i, 0)),
                   pl.BlockSpec((1, HG_HEADS, tm, LANES), lambda b, i: (b, 0, i, 0)),
                   pl.BlockSpec((1, DA_HEADS, tm // ATT_T, LANES, ATT_T),
                                lambda b, i: (b, 0, i, 0, 0))],
        out_shape=[jax.ShapeDtypeStruct((bsz, N_SLABS, t, LANES), BF16),
                   jax.ShapeDtypeStruct((bsz, HG_HEADS, t, LANES), F32),
                   jax.ShapeDtypeStruct((bsz, DA_HEADS, t // ATT_T, LANES, ATT_T), BF16)],
        compiler_params=pltpu.CompilerParams(
            dimension_semantics=("arbitrary", "arbitrary"), vmem_limit_bytes=VMEM_LIMIT_BYTES),
        name="inproj",
    )(x, mod3, norm1_w.reshape(1, d), w_bf)


def _hgrn_diag_scores(a_blks, q_blks, b_blks, c_rows, col0):
    sub, chunk = a_blks[0].shape
    n_sl = sub // SUBLANES
    row = lax.broadcasted_iota(jnp.int32, (SUBLANES, chunk), 0)
    lane = lax.broadcasted_iota(jnp.int32, (SUBLANES, chunk), 1)
    sl = [slice(r * SUBLANES, (r + 1) * SUBLANES) for r in range(n_sl)]
    parts = [[a[sl[r]] for r in range(n_sl)] for a in a_blks]
    for s in range(sub):
        first = s // SUBLANES
        here = lane == col0 + s
        here_diag = here & (row >= s - first * SUBLANES)
        for h, (qb, bb) in enumerate(zip(q_blks, b_blks)):
            cs = c_rows(h, s)
            for r in range(first, n_sl):
                col = jnp.sum(qb[sl[r]] * jnp.exp2(bb[sl[r]] - cs), axis=-1, keepdims=True)
                parts[h][r] = jnp.where(here_diag if r == first else here, col, parts[h][r])
    return [jnp.concatenate(p, axis=0) if n_sl > 1 else p[0] for p in parts]


def _hgrn_decays(lb, tri, f_logit, bkc_ref):
    f = lb + (1.0 - lb) * jax.nn.sigmoid(f_logit)
    logf = jnp.log2(f)
    k = 1.0 - f
    h1 = logf.astype(BF16)
    r1 = logf - h1.astype(F32)
    h2 = r1.astype(BF16)
    h3 = (r1 - h2.astype(F32)).astype(BF16)
    cs = jnp.dot(tri, jnp.concatenate([h1, h2, h3], axis=1), preferred_element_type=F32)
    b = cs[:, :LANES] + cs[:, LANES:2 * LANES] + cs[:, 2 * LANES:]
    bkc_ref[0] = b
    bkc_ref[1] = k
    bkc_ref[2] = b - jnp.log2(k)


def _hgrn_intra(qs, bkc_ref, a_ref, qdec_ref, kend_ref, dec_ref, bounded_decay):
    c_len, sub = HGRN_CHUNK, HGRN_SUB
    n_heads = len(qs)
    bs = [bkc_ref[h, 0] for h in range(n_heads)]
    ks = [bkc_ref[h, 1] for h in range(n_heads)]
    a_rows = [[] for _ in range(n_heads)]
    causal = (lax.broadcasted_iota(jnp.int32, (sub, c_len), 1)
              - lax.broadcasted_iota(jnp.int32, (sub, c_len), 0))
    for i in range(c_len // sub):
        rows = slice(i * sub, (i + 1) * sub)
        n_cols = (i + 1) * sub if bounded_decay else i * sub
        a_blks = []
        for q, b, k in zip(qs, bs, ks):
            if n_cols == 0:
                a_blks.append(jnp.zeros((sub, c_len), F32))
                continue
            ref = b[i * sub - 1:i * sub] if i else jnp.zeros((1, LANES), F32)
            qt = (q[rows] * jnp.exp2(b[rows] - ref)).astype(BF16)
            kt = k[:n_cols] * jnp.exp2(ref - b[:n_cols])
            if n_cols < c_len:
                kt = jnp.concatenate([kt, jnp.zeros((c_len - n_cols, LANES), F32)], axis=0)
            a_blks.append(_nt_dot(qt, kt.astype(BF16)))
        if bounded_decay:
            scored = [jnp.where(causal <= i * sub, a, 0.0) for a in a_blks]
        else:
            c_rows = lambda h, s, i=i: bkc_ref[h, 2, pl.ds(i * sub + s, SUBLANES, stride=0), :]
            scored = _hgrn_diag_scores(a_blks, [q[rows] for q in qs], [b[rows] for b in bs],
                                       c_rows, i * sub)
        for h in range(n_heads):
            a_rows[h].append(scored[h])
    for h, (q, b, k) in enumerate(zip(qs, bs, ks)):
        b_last = b[c_len - 1:c_len]
        a_ref[h] = jnp.concatenate(a_rows[h], axis=0).astype(BF16)
        qdec_ref[h] = (q * jnp.exp2(b)).astype(BF16)
        kend_ref[h] = (k * jnp.exp2(b_last - b)).astype(BF16)
        dec_ref[h] = jnp.exp2(b_last)


def _hgrn_state(gw, v_bf, g, st, a_ref, qdec_ref, kend_ref, dec_ref):
    o = (jnp.dot(a_ref[...], v_bf, preferred_element_type=F32)
         + _nt_dot(qdec_ref[...], st.astype(BF16)))
    y = o * _rms_scale(o) * gw
    out = (y * (g * jax.nn.sigmoid(g))).astype(BF16)
    st_new = st * dec_ref[...] + jnp.dot(v_bf.astype(F32).T.astype(BF16), kend_ref[...],
                                         preferred_element_type=F32)
    return out, st_new


def _mix_kernel(lq1_ref, lk1_ref, lq2_ref, lk2_ref, sw_ref, q1_ref, q2_ref, k1_ref, k2_ref,
                vt_ref, lbl_ref, gw_ref, hq_ref, hf_ref, hi_ref, hg_ref, oda_ref, ohg_ref,
                s_ref, mx_ref, st_ref, bkc_ref, a_ref, qdec_ref, kend_ref, dec_ref):
    step = pl.program_id(1)

    c_len = HGRN_CHUNK
    n_heads, t_len = hq_ref.shape[1], hq_ref.shape[2]
    n_chunks = t_len // c_len
    chunks_per_step = ohg_ref.shape[1] // c_len
    assert chunks_per_step % 2 == 0
    l = lbl_ref[...]
    e = jnp.exp(l - jnp.max(l, axis=0, keepdims=True))
    lb_all = jnp.sum(e[:LAYER + 1], axis=0, keepdims=True) / jnp.sum(e, axis=0, keepdims=True)
    gw = gw_ref[...]
    tri = (lax.broadcasted_iota(jnp.int32, (c_len, c_len), 0)
           >= lax.broadcasted_iota(jnp.int32, (c_len, c_len), 1)).astype(BF16)

    def rows_of(c):
        c = jnp.minimum(c, n_chunks - 1)
        return pl.ds(pl.multiple_of(c * c_len, c_len), c_len)

    def stage0(c, par):
        rows = rows_of(c)
        for h in range(n_heads):
            _hgrn_decays(lb_all[:, h * LANES:(h + 1) * LANES], tri, hf_ref[0, h, rows, :],
                         bkc_ref.at[par, h])

    def stage1(c, par, bounded_decay):
        rows = rows_of(c)
        _hgrn_intra([hq_ref[0, h, rows, :].astype(F32) for h in range(n_heads)], bkc_ref.at[par],
                    a_ref.at[par], qdec_ref.at[par], kend_ref.at[par], dec_ref.at[par],
                    bounded_decay)

    def stage2(c, par):
        rows = rows_of(c)
        out_rows = pl.ds(pl.multiple_of((c - step * chunks_per_step) * c_len, c_len), c_len)
        for h in range(n_heads):
            out, st_new = _hgrn_state(gw, hi_ref[0, h, rows, :], hg_ref[0, h, rows, :].astype(F32),
                                      st_ref[h], a_ref.at[par, h], qdec_ref.at[par, h],
                                      kend_ref.at[par, h], dec_ref.at[par, h])
            ohg_ref[0, out_rows, h * LANES:(h + 1) * LANES] = out
            st_ref[h] = st_new

    def hgrn_stages(bounded_decay):
        calls = []
        for r in range(chunks_per_step // 2):
            c = step * chunks_per_step + 2 * r
            calls += [lambda c=c: stage0(c + 2, 0), lambda c=c: stage2(c, 0),
                      lambda c=c: stage1(c + 1, 1, bounded_decay),
                      lambda c=c: stage0(c + 3, 1), lambda c=c: stage2(c + 1, 1),
                      lambda c=c: stage1(c + 2, 0, bounded_decay)]
        return calls

    t = ATT_T
    nq = q1_ref.shape[2] // t
    lam = (jnp.exp(jnp.sum(lq1_ref[...] * lk1_ref[...], axis=-1, keepdims=True))
           - jnp.exp(jnp.sum(lq2_ref[...] * lk2_ref[...], axis=-1, keepdims=True))
           + LAMBDA_INIT)
    q_refs = (q1_ref, q2_ref)
    k_refs = (k1_ref, k2_ref)
    ones = jnp.ones((BF16_ROWS, t), BF16)
    causal = (lax.broadcasted_iota(jnp.int32, (t, t), 0)
              <= lax.broadcasted_iota(jnp.int32, (t, t), 1))
    tiles = [(qi, j) for qi in range(nq) for j in range(qi + 1)]
    n_buf = ATT_LOOKAHEAD + 1

    def key_max(s):
        quarter = t // 4
        parts = [s[i * quarter:(i + 1) * quarter] for i in range(4)]
        part = jnp.maximum(jnp.maximum(parts[0], parts[1]), jnp.maximum(parts[2], parts[3]))
        return jnp.max(part, axis=0, keepdims=True)

    def produce(n):
        qi, j = tiles[n]
        for mp in range(2):
            s = _nt_dot(k_refs[mp][0, 0, j * t:(j + 1) * t, :],
                        q_refs[mp][0, 0, qi * t:(qi + 1) * t, :])
            if j == qi:
                s = jnp.where(causal, s, NEG_BIG)
            s_ref[n % n_buf, mp] = s
            mx_ref[n % n_buf, mp] = key_max(s)

    def finalize(qi, stats):
        (_, acc1), (_, acc2) = stats
        o = (acc1[:LANES] * (1.0 / acc1[LANES:LANES + 1])
             - acc2[:LANES] * (lam / acc2[LANES:LANES + 1]))
        y = o * lax.rsqrt(jnp.mean(o * o, axis=0, keepdims=True) + EPS) * sw_ref[...]
        oda_ref[0, qi * t:(qi + 1) * t, :] = (y * (1.0 - LAMBDA_INIT)).T.astype(oda_ref.dtype)

    def run(bounded_decay):
        @pl.when(step == 0)
        def _():
            st_ref[...] = jnp.zeros_like(st_ref)
            stage0(0, 0)
            stage1(0, 0, bounded_decay)
            stage0(1, 1)

        stages = hgrn_stages(bounded_decay)
        for n in range(min(ATT_LOOKAHEAD, len(tiles))):
            produce(n)
        stats = None
        issued = 0
        for n, (qi, j) in enumerate(tiles):
            if n + ATT_LOOKAHEAD < len(tiles):
                produce(n + ATT_LOOKAHEAD)
            vta = jnp.concatenate([vt_ref[0, 0, j], ones], axis=0)
            new_stats = []
            for mp in range(2):
                s = s_ref[n % n_buf, mp]
                mx = mx_ref[n % n_buf, mp]
                if j == 0:
                    m_new = mx
                    acc = jnp.dot(vta, jnp.exp2(s - m_new).astype(BF16),
                                  preferred_element_type=F32)
                else:
                    m_old, acc_old = stats[mp]
                    m_new = jnp.maximum(m_old, mx)
                    acc = (jnp.exp2(m_old - m_new) * acc_old
                           + jnp.dot(vta, jnp.exp2(s - m_new).astype(BF16),
                                     preferred_element_type=F32))
                new_stats.append((m_new, acc))
            stats = new_stats
            if j == qi:
                finalize(qi, stats)
            due = (n + 1) * len(stages) // len(tiles)
            while issued < due:
                stages[issued]()
                issued += 1

    bounded = HGRN_SUB * jnp.max(-jnp.log2(lb_all)) <= HGRN_MAX_SUB_DECAY_LOG2
    pl.when(bounded)(lambda: run(True))
    pl.when(jnp.logical_not(bounded))(lambda: run(False))


def _mix(oa, of, ovt, lb_logits, gnorm_w, lq1, lk1, lq2, lk2, subln_w):
    bsz, _, t, _ = oa.shape
    tq = ATT_T
    assert DA_HEADS == HG_HEADS
    width = HG_HEADS * LANES
    vec = lambda n: pl.BlockSpec((1, n), lambda b, h: (0, 0))
    head_slab = lambda slot: pl.BlockSpec((1, 1, t, LANES), lambda b, h: (b, slot + h, 0, 0))
    group_slab = lambda slot: pl.BlockSpec((1, HG_HEADS, t, LANES),
                                           lambda b, h: (b, slot // HG_HEADS, 0, 0))
    return pl.pallas_call(
        _mix_kernel,
        grid=(bsz, DA_HEADS),
        in_specs=[vec(DA_HEAD_DIM), vec(DA_HEAD_DIM), vec(DA_HEAD_DIM), vec(DA_HEAD_DIM),
                  pl.BlockSpec((2 * DA_HEAD_DIM, 1), lambda b, h: (0, 0)),
                  head_slab(SLOT_AQ1), head_slab(SLOT_AQ2), head_slab(SLOT_AK1),
                  head_slab(SLOT_AK2),
                  pl.BlockSpec((1, 1, t // tq, LANES, tq), lambda b, h: (b, h, 0, 0, 0)),
                  pl.BlockSpec((lb_logits.shape[0], width), lambda b, h: (0, 0)),
                  pl.BlockSpec((1, LANES), lambda b, h: (0, 0)),
                  group_slab(SLOT_HQ),
                  pl.BlockSpec((1, HG_HEADS, t, LANES), lambda b, h: (b, 0, 0, 0)),
                  group_slab(SLOT_HI), group_slab(SLOT_HG)],
        out_specs=[pl.BlockSpec((1, t, LANES), lambda b, h: (b, 0, h)),
                   pl.BlockSpec((1, t // DA_HEADS, width), lambda b, h: (b, h, 0))],
        out_shape=[jax.ShapeDtypeStruct((bsz, t, DA_HEADS * LANES), BF16),
                   jax.ShapeDtypeStruct((bsz, t, width), BF16)],
        scratch_shapes=[pltpu.VMEM((ATT_LOOKAHEAD + 1, 2, tq, tq), F32),
                        pltpu.VMEM((ATT_LOOKAHEAD + 1, 2, 1, tq), F32),
                        pltpu.VMEM((HG_HEADS, HG_DIM, HG_DIM), F32),
                        pltpu.VMEM((2, HG_HEADS, 3, HGRN_CHUNK, LANES), F32),
                        pltpu.VMEM((2, HG_HEADS, HGRN_CHUNK, HGRN_CHUNK), BF16),
                        pltpu.VMEM((2, HG_HEADS, HGRN_CHUNK, LANES), BF16),
                        pltpu.VMEM((2, HG_HEADS, HGRN_CHUNK, LANES), BF16),
                        pltpu.VMEM((2, HG_HEADS, 1, LANES), F32)],
        compiler_params=pltpu.CompilerParams(
            dimension_semantics=("arbitrary", "arbitrary"), vmem_limit_bytes=VMEM_LIMIT_BYTES),
        name="mix",
    )(lq1, lk1, lq2, lk2, subln_w.reshape(2 * DA_HEAD_DIM, 1), oa, oa, oa, oa, ovt,
      lb_logits, gnorm_w.reshape(1, LANES), oa, of, oa, oa)


def _ffn_kernel(x_ref, mod_ref, ohg_ref, oda_ref, wo_ref, n2_ref, wg_ref, wu_ref, wd_ref, fw_ref,
                o_ref):
    x = x_ref[0]
    m = mod_ref[0]
    mix = jnp.dot(jnp.concatenate([ohg_ref[0], oda_ref[0]], axis=1), wo_ref[...],
                  preferred_element_type=F32)
    hcur = x + m[2:3] * mix
    u = (hcur * _rms_scale(hcur) * n2_ref[...] * (1.0 + m[4:5]) + m[3:4]).astype(BF16)
    acc = jnp.zeros(x.shape, F32)
    for c in range(wg_ref.shape[1] // FFN_FC):
        sl = slice(c * FFN_FC, (c + 1) * FFN_FC)
        gate = jnp.dot(u, wg_ref[:, sl], preferred_element_type=F32)
        up = jnp.dot(u, wu_ref[:, sl], preferred_element_type=F32)
        act = (gate * jax.nn.sigmoid(gate) * up).astype(BF16)
        acc = acc + jnp.dot(act, wd_ref[sl, :], preferred_element_type=F32)
    hcur = hcur + m[5:6] * acc
    o_ref[0] = hcur * _rms_scale(hcur) * fw_ref[...]


def _ffn(x, mod3, o_hg, o_da, w_out_bf, norm2_w, wg_bf, wu_bf, wd_bf, final_w):
    bsz, t, d = x.shape
    tm = FFN_TM
    dff = wg_bf.shape[1]
    assert dff % FFN_FC == 0
    tile = lambda w: pl.BlockSpec((1, tm, w), lambda b, i: (b, i, 0))
    const = lambda r, c: pl.BlockSpec((r, c), lambda b, i: (0, 0), pipeline_mode=pl.Buffered(1))
    return pl.pallas_call(
        _ffn_kernel,
        grid=(bsz, t // tm),
        in_specs=[tile(d),
                  pl.BlockSpec((1, N_MOD, d), lambda b, i: (b, 0, 0)),
                  tile(o_hg.shape[2]), tile(o_da.shape[2]),
                  const(d, d), const(1, d), const(d, dff), const(d, dff), const(dff, d),
                  const(1, d)],
        out_specs=tile(d),
        out_shape=jax.ShapeDtypeStruct((bsz, t, d), F32),
        compiler_params=pltpu.CompilerParams(
            dimension_semantics=("arbitrary", "arbitrary"), vmem_limit_bytes=VMEM_LIMIT_BYTES),
        name="ffn",
    )(x, mod3, o_hg, o_da, w_out_bf, norm2_w.reshape(1, d), wg_bf, wu_bf, wd_bf,
      final_w.reshape(1, d))


def kernel(x, c, w_ada, b_ada, norm1_w, w_in, hgrn_lb_logits, hgrn_gnorm_w, diff_lambda_q1,
           diff_lambda_k1, diff_lambda_q2, diff_lambda_k2, diff_subln_w, w_out, norm2_w,
           w_ffn_gate, w_ffn_up, w_ffn_down, final_norm_w):
    bsz, t, d = x.shape
    assert w_ada.shape[0] == 1, "single-layer trunk"
    l = LAYER
    mod3 = _ada(c, w_ada[l], b_ada[l]).reshape(bsz, N_MOD, d)
    oa, of, ovt = _inproj(x, mod3, norm1_w[l], w_in[l].astype(BF16))
    o_da, o_hg = _mix(oa, of, ovt, hgrn_lb_logits, hgrn_gnorm_w[l], diff_lambda_q1[l:l + 1],
                      diff_lambda_k1[l:l + 1], diff_lambda_q2[l:l + 1], diff_lambda_k2[l:l + 1],
                      diff_subln_w[l])
    return _ffn(x, mod3, o_hg, o_da, w_out[l].astype(BF16), norm2_w[l],
                w_ffn_gate[l].astype(BF16), w_ffn_up[l].astype(BF16),
                w_ffn_down[l].astype(BF16), final_norm_w)
```

```python
import math

import jax
import jax.numpy as jnp
import numpy as np
from jax import lax
from jax.experimental import pallas as pl
from jax.experimental.pallas import tpu as pltpu

F32 = jnp.float32
BF16 = jnp.bfloat16

HG_HEADS = 4
HG_DIM = 128
DA_HEADS = 4
DA_HEAD_DIM = 64
N_MOD = 6
EPS = 1e-6
LAYER = 0
LAMBDA_INIT = 0.8 - 0.6 * math.exp(-0.3 * LAYER)
LOG2_E = math.log2(math.e)

LANES = 128
SUBLANES = 8
BF16_ROWS = 16
VMEM_LIMIT_BYTES = 56 * 1024 * 1024

HGRN_CHUNK = 64
HGRN_SUB = 8
HGRN_MAX_SUB_DECAY_LOG2 = 64.0
ADA_TN = 768
INPROJ_TM = 512
FFN_TM = 512
FFN_FC = 256
ATT_T = 256
ATT_LOOKAHEAD = 2
NEG_BIG = -1e30
N_SLOPE_PIECES = 3


def _nt_dot(a, b):
    return lax.dot_general(a, b, (((1,), (1,)), ((), ())), preferred_element_type=F32)


def _rms_scale(x):
    return lax.rsqrt(jnp.mean(x * x, axis=-1, keepdims=True) + EPS)


def _alibi_slope(head):
    return 2.0 ** (-8.0 * (head + 1) / DA_HEADS)


def _cast_specs(weights, n_steps, step_of):
    in_specs, out_specs, out_shapes = [], [], []
    for w in weights:
        rows, cols = w.shape
        n_blocks = n_steps
        while rows % n_blocks or (rows // n_blocks) % BF16_ROWS:
            n_blocks -= 1
        index = lambda *g, n_blocks=n_blocks: (jnp.minimum(step_of(*g), n_blocks - 1), 0)
        in_specs.append(pl.BlockSpec((rows // n_blocks, cols), index))
        out_specs.append(pl.BlockSpec((rows // n_blocks, cols), index))
        out_shapes.append(jax.ShapeDtypeStruct(w.shape, BF16))
    return in_specs, out_specs, out_shapes


def _cast_blocks(in_refs, out_refs):
    for src, dst in zip(in_refs, out_refs):
        dst[...] = src[...].astype(dst.dtype)


def _ada_kernel(c_ref, w_ref, b_ref, *refs):
    n_cast = (len(refs) - 1) // 2
    o_ref = refs[n_cast]
    c = c_ref[...]
    a = c * jax.nn.sigmoid(c)
    o_ref[...] = jnp.dot(a, w_ref[...], preferred_element_type=F32,
                         precision=lax.Precision.HIGHEST) + b_ref[...]
    _cast_blocks(refs[:n_cast], refs[n_cast + 1:])


def _ada(c, w_ada, b_ada, cast_weights):
    bsz, d = c.shape
    n = w_ada.shape[1]
    tn = ADA_TN
    cast_in, cast_out, cast_shapes = _cast_specs(cast_weights, n // tn, lambda j: j)
    mod, *cast = pl.pallas_call(
        _ada_kernel,
        grid=(n // tn,),
        in_specs=[pl.BlockSpec((bsz, d), lambda j: (0, 0)),
                  pl.BlockSpec((d, tn), lambda j: (0, j)),
                  pl.BlockSpec((1, tn), lambda j: (0, j))] + cast_in,
        out_specs=[pl.BlockSpec((bsz, tn), lambda j: (0, j))] + cast_out,
        out_shape=[jax.ShapeDtypeStruct((bsz, n), F32)] + cast_shapes,
        compiler_params=pltpu.CompilerParams(
            dimension_semantics=("arbitrary",), vmem_limit_bytes=VMEM_LIMIT_BYTES),
        name="ada",
    )(c, w_ada, b_ada.reshape(1, n), *cast_weights)
    return mod, cast


SLOT_HQ, SLOT_HI, SLOT_HG, SLOT_AQ1, SLOT_AQ2, SLOT_AK1, SLOT_AK2 = 0, 4, 8, 12, 16, 20, 24
N_SLABS = 28
GRP_HQ, GRP_HF, GRP_HI, GRP_HG, GRP_AQ, GRP_AK, GRP_AV = range(7)


def _bf16_pieces(value, n):
    rest = np.float32(value)
    pieces = []
    for _ in range(n):
        piece = np.float32(np.asarray(rest, dtype=BF16))
        pieces.append(float(piece))
        rest = np.float32(rest - piece)
    return pieces


def _aug_lanes(lane, values, first_map):
    base = DA_HEAD_DIM if first_map else 0
    aug = jnp.zeros(lane.shape, F32)
    for n, e in enumerate(values):
        aug = jnp.where(lane == base + n, e, aug)
    return aug


def _inproj_kernel(x_ref, mod_ref, n1_ref, w_ref, *refs):
    n_cast = (len(refs) - 3) // 2
    oa_ref, of_ref, ovt_ref = refs[n_cast:n_cast + 3]
    _cast_blocks(refs[:n_cast], refs[n_cast + 3:])
    tm = x_ref.shape[1]
    x = x_ref[0]
    m = mod_ref[0]
    y = x * _rms_scale(x) * n1_ref[...]
    u = (y * (1.0 + m[1:2]) + m[0:1]).astype(BF16)

    lane = lax.broadcasted_iota(jnp.int32, (tm, LANES), 1)
    keep = (lane < DA_HEAD_DIM, lane >= DA_HEAD_DIM)
    maps = ((SLOT_AQ1, SLOT_AK1, True), (SLOT_AQ2, SLOT_AK2, False))
    pos = (pl.program_id(1) * tm + lax.broadcasted_iota(jnp.int32, (tm, LANES), 0)).astype(F32)
    pos_hi = pos.astype(BF16).astype(F32)
    pos_pieces = [pos_hi, pos - pos_hi]
    k_aug = [_aug_lanes(lane, pos_pieces * N_SLOPE_PIECES, first) for _, _, first in maps]
    q_aug = []
    for head in range(DA_HEADS):
        c_lanes = [c for c in _bf16_pieces(LOG2_E * _alibi_slope(head), N_SLOPE_PIECES)
                   for _ in pos_pieces]
        q_aug.append([_aug_lanes(lane, c_lanes, first) for _, _, first in maps])

    for j in range(w_ref.shape[1] // (2 * LANES)):
        p = jnp.dot(u, w_ref[:, j * 2 * LANES:(j + 1) * 2 * LANES], preferred_element_type=F32)
        for half in range(2):
            ph = p[:, half * LANES:(half + 1) * LANES]
            grp, head = divmod(2 * j + half, 4)
            if grp == GRP_HF:
                of_ref[0, head] = ph
            elif grp == GRP_AQ:
                ph = ph * (LOG2_E * DA_HEAD_DIM ** -0.5)
                for mp, (q_slot, _, _) in enumerate(maps):
                    oa_ref[0, q_slot + head] = jnp.where(keep[mp], ph, q_aug[head][mp]).astype(BF16)
            elif grp == GRP_AK:
                for mp, (_, k_slot, _) in enumerate(maps):
                    oa_ref[0, k_slot + head] = jnp.where(keep[mp], ph, k_aug[mp]).astype(BF16)
            elif grp == GRP_AV:
                vt = ph.T
                for jt in range(tm // ATT_T):
                    ovt_ref[0, head, jt] = vt[:, jt * ATT_T:(jt + 1) * ATT_T].astype(BF16)
            else:
                slot = {GRP_HQ: SLOT_HQ, GRP_HI: SLOT_HI, GRP_HG: SLOT_HG}[grp]
                oa_ref[0, slot + head] = ph.astype(BF16)


def _inproj(x, mod3, norm1_w, w_bf, cast_weights):
    bsz, t, d = x.shape
    n = w_bf.shape[1]
    tm = INPROJ_TM
    nt = t // tm
    cast_in, cast_out, cast_shapes = _cast_specs(cast_weights, bsz * nt, lambda b, i: b * nt + i)
    oa, of, ovt, *cast = pl.pallas_call(
        _inproj_kernel,
        grid=(bsz, nt),
        in_specs=[pl.BlockSpec((1, tm, d), lambda b, i: (b, i, 0)),
                  pl.BlockSpec((1, N_MOD, d), lambda b, i: (b, 0, 0)),
                  pl.BlockSpec((1, d), lambda b, i: (0, 0)),
                  pl.BlockSpec((d, n), lambda b, i: (0, 0), pipeline_mode=pl.Buffered(1))]
        + cast_in,
        out_specs=[pl.BlockSpec((1, N_SLABS, tm, LANES), lambda b, i: (b, 0, i, 0)),
                   pl.BlockSpec((1, HG_HEADS, tm, LANES), lambda b, i: (b, 0, i, 0)),
                   pl.BlockSpec((1, DA_HEADS, tm // ATT_T, LANES, ATT_T),
                                lambda b, i: (b, 0, i, 0, 0))] + cast_out,
        out_shape=[jax.ShapeDtypeStruct((bsz, N_SLABS, t, LANES), BF16),
                   jax.ShapeDtypeStruct((bsz, HG_HEADS, t, LANES), F32),
                   jax.ShapeDtypeStruct((bsz, DA_HEADS, t // ATT_T, LANES, ATT_T), BF16)]
        + cast_shapes,
        compiler_params=pltpu.CompilerParams(
            dimension_semantics=("arbitrary", "arbitrary"), vmem_limit_bytes=VMEM_LIMIT_BYTES),
        name="inproj",
    )(x, mod3, norm1_w.reshape(1, d), w_bf, *cast_weights)
    return oa, of, ovt, cast


def _hgrn_diag_scores(a_blks, q_blks, b_blks, c_rows, col0):
    sub, chunk = a_blks[0].shape
    n_sl = sub // SUBLANES
    row = lax.broadcasted_iota(jnp.int32, (SUBLANES, chunk), 0)
    lane = lax.broadcasted_iota(jnp.int32, (SUBLANES, chunk), 1)
    sl = [slice(r * SUBLANES, (r + 1) * SUBLANES) for r in range(n_sl)]
    parts = [[a[sl[r]] for r in range(n_sl)] for a in a_blks]
    for s in range(sub):
        first = s // SUBLANES
        here = lane == col0 + s
        here_diag = here & (row >= s - first * SUBLANES)
        for h, (qb, bb) in enumerate(zip(q_blks, b_blks)):
            cs = c_rows(h, s)
            for r in range(first, n_sl):
                col = jnp.sum(qb[sl[r]] * jnp.exp2(bb[sl[r]] - cs), axis=-1, keepdims=True)
                parts[h][r] = jnp.where(here_diag if r == first else here, col, parts[h][r])
    return [jnp.concatenate(p, axis=0) if n_sl > 1 else p[0] for p in parts]


def _hgrn_decays(lb, tri, f_logit, bkc_ref):
    f = lb + (1.0 - lb) * jax.nn.sigmoid(f_logit)
    logf = jnp.log2(f)
    k = 1.0 - f
    h1 = logf.astype(BF16)
    r1 = logf - h1.astype(F32)
    h2 = r1.astype(BF16)
    h3 = (r1 - h2.astype(F32)).astype(BF16)
    cs = jnp.dot(tri, jnp.concatenate([h1, h2, h3], axis=1), preferred_element_type=F32)
    b = cs[:, :LANES] + cs[:, LANES:2 * LANES] + cs[:, 2 * LANES:]
    bkc_ref[0] = b
    bkc_ref[1] = k
    bkc_ref[2] = b - jnp.log2(k)


def _hgrn_intra(qs, bkc_ref, a_ref, qdec_ref, kend_ref, dec_ref, bounded_decay):
    c_len, sub = HGRN_CHUNK, HGRN_SUB
    n_heads = len(qs)
    bs = [bkc_ref[h, 0] for h in range(n_heads)]
    ks = [bkc_ref[h, 1] for h in range(n_heads)]
    a_rows = [[] for _ in range(n_heads)]
    causal = (lax.broadcasted_iota(jnp.int32, (sub, c_len), 1)
              - lax.broadcasted_iota(jnp.int32, (sub, c_len), 0))
    for i in range(c_len // sub):
        rows = slice(i * sub, (i + 1) * sub)
        n_cols = (i + 1) * sub if bounded_decay else i * sub
        a_blks = []
        for q, b, k in zip(qs, bs, ks):
            if n_cols == 0:
                a_blks.append(jnp.zeros((sub, c_len), F32))
                continue
            ref = b[i * sub - 1:i * sub] if i else jnp.zeros((1, LANES), F32)
            qt = (q[rows] * jnp.exp2(b[rows] - ref)).astype(BF16)
            kt = k[:n_cols] * jnp.exp2(ref - b[:n_cols])
            if n_cols < c_len:
                kt = jnp.concatenate([kt, jnp.zeros((c_len - n_cols, LANES), F32)], axis=0)
            a_blks.append(_nt_dot(qt, kt.astype(BF16)))
        if bounded_decay:
            scored = [jnp.where(causal <= i * sub, a, 0.0) for a in a_blks]
        else:
            c_rows = lambda h, s, i=i: bkc_ref[h, 2, pl.ds(i * sub + s, SUBLANES, stride=0), :]
            scored = _hgrn_diag_scores(a_blks, [q[rows] for q in qs], [b[rows] for b in bs],
                                       c_rows, i * sub)
        for h in range(n_heads):
            a_rows[h].append(scored[h])
    for h, (q, b, k) in enumerate(zip(qs, bs, ks)):
        b_last = b[c_len - 1:c_len]
        a_ref[h] = jnp.concatenate(a_rows[h], axis=0).astype(BF16)
        qdec_ref[h] = (q * jnp.exp2(b)).astype(BF16)
        kend_ref[h] = (k * jnp.exp2(b_last - b)).astype(BF16)
        dec_ref[h] = jnp.exp2(b_last)


def _hgrn_state(gw, v_bf, g, st, a_ref, qdec_ref, kend_ref, dec_ref):
    o = (jnp.dot(a_ref[...], v_bf, preferred_element_type=F32)
         + _nt_dot(qdec_ref[...], st.astype(BF16)))
    y = o * _rms_scale(o) * gw
    out = (y * (g * jax.nn.sigmoid(g))).astype(BF16)
    st_new = st * dec_ref[...] + jnp.dot(v_bf.astype(F32).T.astype(BF16), kend_ref[...],
                                         preferred_element_type=F32)
    return out, st_new


def _mix_kernel(lq1_ref, lk1_ref, lq2_ref, lk2_ref, sw_ref, q1_ref, q2_ref, k1_ref, k2_ref,
                vt_ref, lbl_ref, gw_ref, hq_ref, hf_ref, hi_ref, hg_ref, oda_ref, ohg_ref,
                s_ref, mx_ref, st_ref, bkc_ref, a_ref, qdec_ref, kend_ref, dec_ref):
    step = pl.program_id(1)

    c_len = HGRN_CHUNK
    n_heads, t_len = hq_ref.shape[1], hq_ref.shape[2]
    n_chunks = t_len // c_len
    chunks_per_step = ohg_ref.shape[1] // c_len
    assert chunks_per_step % 2 == 0
    l = lbl_ref[...]
    e = jnp.exp(l - jnp.max(l, axis=0, keepdims=True))
    lb_all = jnp.sum(e[:LAYER + 1], axis=0, keepdims=True) / jnp.sum(e, axis=0, keepdims=True)
    gw = gw_ref[...]
    tri = (lax.broadcasted_iota(jnp.int32, (c_len, c_len), 0)
           >= lax.broadcasted_iota(jnp.int32, (c_len, c_len), 1)).astype(BF16)

    def rows_of(c):
        c = jnp.minimum(c, n_chunks - 1)
        return pl.ds(pl.multiple_of(c * c_len, c_len), c_len)

    def stage0(c, par):
        rows = rows_of(c)
        for h in range(n_heads):
            _hgrn_decays(lb_all[:, h * LANES:(h + 1) * LANES], tri, hf_ref[0, h, rows, :],
                         bkc_ref.at[par, h])

    def stage1(c, par, bounded_decay):
        rows = rows_of(c)
        _hgrn_intra([hq_ref[0, h, rows, :].astype(F32) for h in range(n_heads)], bkc_ref.at[par],
                    a_ref.at[par], qdec_ref.at[par], kend_ref.at[par], dec_ref.at[par],
                    bounded_decay)

    def stage2(c, par):
        rows = rows_of(c)
        out_rows = pl.ds(pl.multiple_of((c - step * chunks_per_step) * c_len, c_len), c_len)
        for h in range(n_heads):
            out, st_new = _hgrn_state(gw, hi_ref[0, h, rows, :], hg_ref[0, h, rows, :].astype(F32),
                                      st_ref[h], a_ref.at[par, h], qdec_ref.at[par, h],
                                      kend_ref.at[par, h], dec_ref.at[par, h])
            ohg_ref[0, out_rows, h * LANES:(h + 1) * LANES] = out
            st_ref[h] = st_new

    def hgrn_stages(bounded_decay):
        calls = []
        for r in range(chunks_per_step // 2):
            c = step * chunks_per_step + 2 * r
            calls += [lambda c=c: stage0(c + 2, 0), lambda c=c: stage2(c, 0),
                      lambda c=c: stage1(c + 1, 1, bounded_decay),
                      lambda c=c: stage0(c + 3, 1), lambda c=c: stage2(c + 1, 1),
                      lambda c=c: stage1(c + 2, 0, bounded_decay)]
        return calls

    t = ATT_T
    nq = q1_ref.shape[2] // t
    lam = (jnp.exp(jnp.sum(lq1_ref[...] * lk1_ref[...], axis=-1, keepdims=True))
           - jnp.exp(jnp.sum(lq2_ref[...] * lk2_ref[...], axis=-1, keepdims=True))
           + LAMBDA_INIT)
    q_refs = (q1_ref, q2_ref)
    k_refs = (k1_ref, k2_ref)
    ones = jnp.ones((BF16_ROWS, t), BF16)
    causal = (lax.broadcasted_iota(jnp.int32, (t, t), 0)
              <= lax.broadcasted_iota(jnp.int32, (t, t), 1))
    tiles = [(qi, j) for qi in range(nq) for j in range(qi + 1)]
    n_buf = ATT_LOOKAHEAD + 1

    def key_max(s):
        quarter = t // 4
        parts = [s[i * quarter:(i + 1) * quarter] for i in range(4)]
        part = jnp.maximum(jnp.maximum(parts[0], parts[1]), jnp.maximum(parts[2], parts[3]))
        return jnp.max(part, axis=0, keepdims=True)

    def produce(n):
        qi, j = tiles[n]
        for mp in range(2):
            s = _nt_dot(k_refs[mp][0, 0, j * t:(j + 1) * t, :],
                        q_refs[mp][0, 0, qi * t:(qi + 1) * t, :])
            if j == qi:
                s = jnp.where(causal, s, NEG_BIG)
            s_ref[n % n_buf, mp] = s
            mx_ref[n % n_buf, mp] = key_max(s)

    def finalize(qi, stats):
        (_, acc1), (_, acc2) = stats
        o = (acc1[:LANES] * (1.0 / acc1[LANES:LANES + 1])
             - acc2[:LANES] * (lam / acc2[LANES:LANES + 1]))
        y = o * lax.rsqrt(jnp.mean(o * o, axis=0, keepdims=True) + EPS) * sw_ref[...]
        oda_ref[0, qi * t:(qi + 1) * t, :] = (y * (1.0 - LAMBDA_INIT)).T.astype(oda_ref.dtype)

    def run(bounded_decay):
        @pl.when(step == 0)
        def _():
            st_ref[...] = jnp.zeros_like(st_ref)
            stage0(0, 0)
            stage1(0, 0, bounded_decay)
            stage0(1, 1)

        stages = hgrn_stages(bounded_decay)
        for n in range(min(ATT_LOOKAHEAD, len(tiles))):
            produce(n)
        stats = None
        issued = 0
        for n, (qi, j) in enumerate(tiles):
            if n + ATT_LOOKAHEAD < len(tiles):
                produce(n + ATT_LOOKAHEAD)
            vta = jnp.concatenate([vt_ref[0, 0, j], ones], axis=0)
            new_stats = []
            for mp in range(2):
                s = s_ref[n % n_buf, mp]
                mx = mx_ref[n % n_buf, mp]
                if j == 0:
                    m_new = mx
                    acc = jnp.dot(vta, jnp.exp2(s - m_new).astype(BF16),
                                  preferred_element_type=F32)
                else:
                    m_old, acc_old = stats[mp]
                    m_new = jnp.maximum(m_old, mx)
                    acc = (jnp.exp2(m_old - m_new) * acc_old
                           + jnp.dot(vta, jnp.exp2(s - m_new).astype(BF16),
                                     preferred_element_type=F32))
                new_stats.append((m_new, acc))
            stats = new_stats
            if j == qi:
                finalize(qi, stats)
            due = (n + 1) * len(stages) // len(tiles)
            while issued < due:
                stages[issued]()
                issued += 1

    bounded = HGRN_SUB * jnp.max(-jnp.log2(lb_all)) <= HGRN_MAX_SUB_DECAY_LOG2
    pl.when(bounded)(lambda: run(True))
    pl.when(jnp.logical_not(bounded))(lambda: run(False))


def _mix(oa, of, ovt, lb_logits, gnorm_w, lq1, lk1, lq2, lk2, subln_w):
    bsz, _, t, _ = oa.shape
    tq = ATT_T
    assert DA_HEADS == HG_HEADS
    width = HG_HEADS * LANES
    vec = lambda n: pl.BlockSpec((1, n), lambda b, h: (0, 0))
    head_slab = lambda slot: pl.BlockSpec((1, 1, t, LANES), lambda b, h: (b, slot + h, 0, 0))
    group_slab = lambda slot: pl.BlockSpec((1, HG_HEADS, t, LANES),
                                           lambda b, h: (b, slot // HG_HEADS, 0, 0))
    return pl.pallas_call(
        _mix_kernel,
        grid=(bsz, DA_HEADS),
        in_specs=[vec(DA_HEAD_DIM), vec(DA_HEAD_DIM), vec(DA_HEAD_DIM), vec(DA_HEAD_DIM),
                  pl.BlockSpec((2 * DA_HEAD_DIM, 1), lambda b, h: (0, 0)),
                  head_slab(SLOT_AQ1), head_slab(SLOT_AQ2), head_slab(SLOT_AK1),
                  head_slab(SLOT_AK2),
                  pl.BlockSpec((1, 1, t // tq, LANES, tq), lambda b, h: (b, h, 0, 0, 0)),
                  pl.BlockSpec((lb_logits.shape[0], width), lambda b, h: (0, 0)),
                  pl.BlockSpec((1, LANES), lambda b, h: (0, 0)),
                  group_slab(SLOT_HQ),
                  pl.BlockSpec((1, HG_HEADS, t, LANES), lambda b, h: (b, 0, 0, 0)),
                  group_slab(SLOT_HI), group_slab(SLOT_HG)],
        out_specs=[pl.BlockSpec((1, t, LANES), lambda b, h: (b, 0, h)),
                   pl.BlockSpec((1, t // DA_HEADS, width), lambda b, h: (b, h, 0))],
        out_shape=[jax.ShapeDtypeStruct((bsz, t, DA_HEADS * LANES), BF16),
                   jax.ShapeDtypeStruct((bsz, t, width), BF16)],
        scratch_shapes=[pltpu.VMEM((ATT_LOOKAHEAD + 1, 2, tq, tq), F32),
                        pltpu.VMEM((ATT_LOOKAHEAD + 1, 2, 1, tq), F32),
                        pltpu.VMEM((HG_HEADS, HG_DIM, HG_DIM), F32),
                        pltpu.VMEM((2, HG_HEADS, 3, HGRN_CHUNK, LANES), F32),
                        pltpu.VMEM((2, HG_HEADS, HGRN_CHUNK, HGRN_CHUNK), BF16),
                        pltpu.VMEM((2, HG_HEADS, HGRN_CHUNK, LANES), BF16),
                        pltpu.VMEM((2, HG_HEADS, HGRN_CHUNK, LANES), BF16),
                        pltpu.VMEM((2, HG_HEADS, 1, LANES), F32)],
        compiler_params=pltpu.CompilerParams(
            dimension_semantics=("arbitrary", "arbitrary"), vmem_limit_bytes=VMEM_LIMIT_BYTES),
        name="mix",
    )(lq1, lk1, lq2, lk2, subln_w.reshape(2 * DA_HEAD_DIM, 1), oa, oa, oa, oa, ovt,
      lb_logits, gnorm_w.reshape(1, LANES), oa, of, oa, oa)


def _ffn_kernel(x_ref, mod_ref, ohg_ref, oda_ref, wo_ref, n2_ref, wg_ref, wu_ref, wd_ref, fw_ref,
                o_ref):
    x = x_ref[0]
    m = mod_ref[0]
    mix = jnp.dot(jnp.concatenate([ohg_ref[0], oda_ref[0]], axis=1), wo_ref[...],
                  preferred_element_type=F32)
    hcur = x + m[2:3] * mix
    u = (hcur * _rms_scale(hcur) * n2_ref[...] * (1.0 + m[4:5]) + m[3:4]).astype(BF16)
    acc = jnp.zeros(x.shape, F32)
    for c in range(wg_ref.shape[1] // FFN_FC):
        sl = slice(c * FFN_FC, (c + 1) * FFN_FC)
        gate = jnp.dot(u, wg_ref[:, sl], preferred_element_type=F32)
        up = jnp.dot(u, wu_ref[:, sl], preferred_element_type=F32)
        act = (gate * jax.nn.sigmoid(gate) * up).astype(BF16)
        acc = acc + jnp.dot(act, wd_ref[sl, :], preferred_element_type=F32)
    hcur = hcur + m[5:6] * acc
    o_ref[0] = hcur * _rms_scale(hcur) * fw_ref[...]


def _ffn(x, mod3, o_hg, o_da, w_out_bf, norm2_w, wg_bf, wu_bf, wd_bf, final_w):
    bsz, t, d = x.shape
    tm = FFN_TM
    dff = wg_bf.shape[1]
    assert dff % FFN_FC == 0
    tile = lambda w: pl.BlockSpec((1, tm, w), lambda b, i: (b, i, 0))
    const = lambda r, c: pl.BlockSpec((r, c), lambda b, i: (0, 0), pipeline_mode=pl.Buffered(1))
    return pl.pallas_call(
        _ffn_kernel,
        grid=(bsz, t // tm),
        in_specs=[tile(d),
                  pl.BlockSpec((1, N_MOD, d), lambda b, i: (b, 0, 0)),
                  tile(o_hg.shape[2]), tile(o_da.shape[2]),
                  const(d, d), const(1, d), const(d, dff), const(d, dff), const(dff, d),
                  const(1, d)],
        out_specs=tile(d),
        out_shape=jax.ShapeDtypeStruct((bsz, t, d), F32),
        compiler_params=pltpu.CompilerParams(
            dimension_semantics=("arbitrary", "arbitrary"), vmem_limit_bytes=VMEM_LIMIT_BYTES),
        name="ffn",
    )(x, mod3, o_hg, o_da, w_out_bf, norm2_w.reshape(1, d), wg_bf, wu_bf, wd_bf,
      final_w.reshape(1, d))


def kernel(x, c, w_ada, b_ada, norm1_w, w_in, hgrn_lb_logits, hgrn_gnorm_w, diff_lambda_q1,
           diff_lambda_k1, diff_lambda_q2, diff_lambda_k2, diff_subln_w, w_out, norm2_w,
           w_ffn_gate, w_ffn_up, w_ffn_down, final_norm_w):
    bsz, t, d = x.shape
    assert w_ada.shape[0] == 1, "single-layer trunk"
    l = LAYER
    mod, (w_in_bf, w_out_bf) = _ada(c, w_ada[l], b_ada[l], [w_in[l], w_out[l]])
    mod3 = mod.reshape(bsz, N_MOD, d)
    oa, of, ovt, (wg_bf, wu_bf, wd_bf) = _inproj(
        x, mod3, norm1_w[l], w_in_bf, [w_ffn_gate[l], w_ffn_up[l], w_ffn_down[l]])
    o_da, o_hg = _mix(oa, of, ovt, hgrn_lb_logits, hgrn_gnorm_w[l], diff_lambda_q1[l:l + 1],
                      diff_lambda_k1[l:l + 1], diff_lambda_q2[l:l + 1], diff_lambda_k2[l:l + 1],
                      diff_subln_w[l])
    return _ffn(x, mod3, o_hg, o_da, w_out_bf, norm2_w[l], wg_bf, wu_bf, wd_bf, final_norm_w)
```

```python
import math

import jax
import jax.numpy as jnp
import numpy as np
from jax import lax
from jax.experimental import pallas as pl
from jax.experimental.pallas import tpu as pltpu

F32 = jnp.float32
BF16 = jnp.bfloat16

HG_HEADS = 4
HG_DIM = 128
DA_HEADS = 4
DA_HEAD_DIM = 64
N_MOD = 6
EPS = 1e-6
LAYER = 0
LAMBDA_INIT = 0.8 - 0.6 * math.exp(-0.3 * LAYER)
LOG2_E = math.log2(math.e)

LANES = 128
SUBLANES = 8
BF16_ROWS = 16
V7X_VMEM_BYTES = 64 * 1024 * 1024
VMEM_LIMIT_BYTES = V7X_VMEM_BYTES * 7 // 8

HGRN_CHUNK = 64
HGRN_SUB = 8
HGRN_MAX_SUB_DECAY_LOG2 = 64.0
ADA_TN = 768
INPROJ_TM = 512
FFN_TM = 1024
FFN_PARTS = 2
FFN_FC = 256
ATT_T = 256
ATT_LOOKAHEAD = 2
NEG_BIG = -1e30
N_SLOPE_PIECES = 3


def _nt_dot(a, b):
    return lax.dot_general(a, b, (((1,), (1,)), ((), ())), preferred_element_type=F32)


def _rms_scale(x):
    return lax.rsqrt(jnp.mean(x * x, axis=-1, keepdims=True) + EPS)


def _alibi_slope(head):
    return 2.0 ** (-8.0 * (head + 1) / DA_HEADS)


def _cast_specs(weights, n_steps, step_of):
    in_specs, out_specs, out_shapes = [], [], []
    for w in weights:
        rows, cols = w.shape
        n_blocks = n_steps
        while rows % n_blocks or (rows // n_blocks) % BF16_ROWS:
            n_blocks -= 1
        index = lambda *g, n_blocks=n_blocks: (jnp.minimum(step_of(*g), n_blocks - 1), 0)
        in_specs.append(pl.BlockSpec((rows // n_blocks, cols), index))
        out_specs.append(pl.BlockSpec((rows // n_blocks, cols), index))
        out_shapes.append(jax.ShapeDtypeStruct(w.shape, BF16))
    return in_specs, out_specs, out_shapes


def _cast_blocks(in_refs, out_refs):
    for src, dst in zip(in_refs, out_refs):
        dst[...] = src[...].astype(dst.dtype)


def _ada_kernel(c_ref, w_ref, b_ref, *refs):
    n_cast = (len(refs) - 1) // 2
    o_ref = refs[n_cast]
    c = c_ref[...]
    a = c * jax.nn.sigmoid(c)
    o_ref[...] = jnp.dot(a, w_ref[...], preferred_element_type=F32,
                         precision=lax.Precision.HIGHEST) + b_ref[...]
    _cast_blocks(refs[:n_cast], refs[n_cast + 1:])


def _ada(c, w_ada, b_ada, cast_weights):
    bsz, d = c.shape
    n = w_ada.shape[1]
    tn = ADA_TN
    cast_in, cast_out, cast_shapes = _cast_specs(cast_weights, n // tn, lambda j: j)
    mod, *cast = pl.pallas_call(
        _ada_kernel,
        grid=(n // tn,),
        in_specs=[pl.BlockSpec((bsz, d), lambda j: (0, 0)),
                  pl.BlockSpec((d, tn), lambda j: (0, j)),
                  pl.BlockSpec((1, tn), lambda j: (0, j))] + cast_in,
        out_specs=[pl.BlockSpec((bsz, tn), lambda j: (0, j))] + cast_out,
        out_shape=[jax.ShapeDtypeStruct((bsz, n), F32)] + cast_shapes,
        compiler_params=pltpu.CompilerParams(
            dimension_semantics=("arbitrary",), vmem_limit_bytes=VMEM_LIMIT_BYTES),
        name="ada",
    )(c, w_ada, b_ada.reshape(1, n), *cast_weights)
    return mod, cast


SLOT_HQ, SLOT_HI, SLOT_HG, SLOT_AQ1, SLOT_AQ2, SLOT_AK1, SLOT_AK2 = 0, 4, 8, 12, 16, 20, 24
N_SLABS = 28
GRP_HQ, GRP_HF, GRP_HI, GRP_HG, GRP_AQ, GRP_AK, GRP_AV = range(7)


def _bf16_pieces(value, n):
    rest = np.float32(value)
    pieces = []
    for _ in range(n):
        piece = np.float32(np.asarray(rest, dtype=BF16))
        pieces.append(float(piece))
        rest = np.float32(rest - piece)
    return pieces


def _aug_lanes(lane, values, first_map):
    base = DA_HEAD_DIM if first_map else 0
    aug = jnp.zeros(lane.shape, F32)
    for n, e in enumerate(values):
        aug = jnp.where(lane == base + n, e, aug)
    return aug


def _inproj_kernel(x_ref, mod_ref, n1_ref, w_ref, *refs):
    n_cast = (len(refs) - 3) // 2
    oa_ref, of_ref, ovt_ref = refs[n_cast:n_cast + 3]
    _cast_blocks(refs[:n_cast], refs[n_cast + 3:])
    tm = x_ref.shape[1]
    x = x_ref[0]
    m = mod_ref[0]
    y = x * _rms_scale(x) * n1_ref[...]
    u = (y * (1.0 + m[1:2]) + m[0:1]).astype(BF16)

    lane = lax.broadcasted_iota(jnp.int32, (tm, LANES), 1)
    keep = (lane < DA_HEAD_DIM, lane >= DA_HEAD_DIM)
    maps = ((SLOT_AQ1, SLOT_AK1, True), (SLOT_AQ2, SLOT_AK2, False))
    pos = (pl.program_id(1) * tm + lax.broadcasted_iota(jnp.int32, (tm, LANES), 0)).astype(F32)
    pos_hi = pos.astype(BF16).astype(F32)
    pos_pieces = [pos_hi, pos - pos_hi]
    k_aug = [_aug_lanes(lane, pos_pieces * N_SLOPE_PIECES, first) for _, _, first in maps]
    q_aug = []
    for head in range(DA_HEADS):
        c_lanes = [c for c in _bf16_pieces(LOG2_E * _alibi_slope(head), N_SLOPE_PIECES)
                   for _ in pos_pieces]
        q_aug.append([_aug_lanes(lane, c_lanes, first) for _, _, first in maps])

    for j in range(w_ref.shape[1] // (2 * LANES)):
        p = jnp.dot(u, w_ref[:, j * 2 * LANES:(j + 1) * 2 * LANES], preferred_element_type=F32)
        for half in range(2):
            ph = p[:, half * LANES:(half + 1) * LANES]
            grp, head = divmod(2 * j + half, 4)
            if grp == GRP_HF:
                of_ref[0, head] = ph
            elif grp == GRP_AQ:
                ph = ph * (LOG2_E * DA_HEAD_DIM ** -0.5)
                for mp, (q_slot, _, _) in enumerate(maps):
                    oa_ref[0, q_slot + head] = jnp.where(keep[mp], ph, q_aug[head][mp]).astype(BF16)
            elif grp == GRP_AK:
                for mp, (_, k_slot, _) in enumerate(maps):
                    oa_ref[0, k_slot + head] = jnp.where(keep[mp], ph, k_aug[mp]).astype(BF16)
            elif grp == GRP_AV:
                vt = ph.T
                for jt in range(tm // ATT_T):
                    ovt_ref[0, head, jt] = vt[:, jt * ATT_T:(jt + 1) * ATT_T].astype(BF16)
            else:
                slot = {GRP_HQ: SLOT_HQ, GRP_HI: SLOT_HI, GRP_HG: SLOT_HG}[grp]
                oa_ref[0, slot + head] = ph.astype(BF16)


def _inproj(x, mod3, norm1_w, w_bf, cast_weights):
    bsz, t, d = x.shape
    n = w_bf.shape[1]
    tm = INPROJ_TM
    nt = t // tm
    cast_in, cast_out, cast_shapes = _cast_specs(cast_weights, bsz * nt, lambda b, i: b * nt + i)
    oa, of, ovt, *cast = pl.pallas_call(
        _inproj_kernel,
        grid=(bsz, nt),
        in_specs=[pl.BlockSpec((1, tm, d), lambda b, i: (b, i, 0)),
                  pl.BlockSpec((1, N_MOD, d), lambda b, i: (b, 0, 0)),
                  pl.BlockSpec((1, d), lambda b, i: (0, 0)),
                  pl.BlockSpec((d, n), lambda b, i: (0, 0), pipeline_mode=pl.Buffered(1))]
        + cast_in,
        out_specs=[pl.BlockSpec((1, N_SLABS, tm, LANES), lambda b, i: (b, 0, i, 0)),
                   pl.BlockSpec((1, HG_HEADS, tm, LANES), lambda b, i: (b, 0, i, 0)),
                   pl.BlockSpec((1, DA_HEADS, tm // ATT_T, LANES, ATT_T),
                                lambda b, i: (b, 0, i, 0, 0))] + cast_out,
        out_shape=[jax.ShapeDtypeStruct((bsz, N_SLABS, t, LANES), BF16),
                   jax.ShapeDtypeStruct((bsz, HG_HEADS, t, LANES), F32),
                   jax.ShapeDtypeStruct((bsz, DA_HEADS, t // ATT_T, LANES, ATT_T), BF16)]
        + cast_shapes,
        compiler_params=pltpu.CompilerParams(
            dimension_semantics=("arbitrary", "arbitrary"), vmem_limit_bytes=VMEM_LIMIT_BYTES),
        name="inproj",
    )(x, mod3, norm1_w.reshape(1, d), w_bf, *cast_weights)
    return oa, of, ovt, cast


def _hgrn_diag_scores(a_blks, q_blks, b_blks, c_rows, col0):
    sub, chunk = a_blks[0].shape
    n_sl = sub // SUBLANES
    row = lax.broadcasted_iota(jnp.int32, (SUBLANES, chunk), 0)
    lane = lax.broadcasted_iota(jnp.int32, (SUBLANES, chunk), 1)
    sl = [slice(r * SUBLANES, (r + 1) * SUBLANES) for r in range(n_sl)]
    parts = [[a[sl[r]] for r in range(n_sl)] for a in a_blks]
    for s in range(sub):
        first = s // SUBLANES
        here = lane == col0 + s
        here_diag = here & (row >= s - first * SUBLANES)
        for h, (qb, bb) in enumerate(zip(q_blks, b_blks)):
            cs = c_rows(h, s)
            for r in range(first, n_sl):
                col = jnp.sum(qb[sl[r]] * jnp.exp2(bb[sl[r]] - cs), axis=-1, keepdims=True)
                parts[h][r] = jnp.where(here_diag if r == first else here, col, parts[h][r])
    return [jnp.concatenate(p, axis=0) if n_sl > 1 else p[0] for p in parts]


def _hgrn_decays(lb, tri, f_logit, bkc_ref):
    f = lb + (1.0 - lb) * jax.nn.sigmoid(f_logit)
    logf = jnp.log2(f)
    k = 1.0 - f
    h1 = logf.astype(BF16)
    r1 = logf - h1.astype(F32)
    h2 = r1.astype(BF16)
    h3 = (r1 - h2.astype(F32)).astype(BF16)
    cs = jnp.dot(tri, jnp.concatenate([h1, h2, h3], axis=1), preferred_element_type=F32)
    b = cs[:, :LANES] + cs[:, LANES:2 * LANES] + cs[:, 2 * LANES:]
    bkc_ref[0] = b
    bkc_ref[1] = k
    bkc_ref[2] = b - jnp.log2(k)


def _hgrn_intra(qs, bkc_ref, a_ref, qdec_ref, kend_ref, dec_ref, bounded_decay):
    c_len, sub = HGRN_CHUNK, HGRN_SUB
    n_heads = len(qs)
    bs = [bkc_ref[h, 0] for h in range(n_heads)]
    ks = [bkc_ref[h, 1] for h in range(n_heads)]
    a_rows = [[] for _ in range(n_heads)]
    causal = (lax.broadcasted_iota(jnp.int32, (sub, c_len), 1)
              - lax.broadcasted_iota(jnp.int32, (sub, c_len), 0))
    for i in range(c_len // sub):
        rows = slice(i * sub, (i + 1) * sub)
        n_cols = (i + 1) * sub if bounded_decay else i * sub
        a_blks = []
        for q, b, k in zip(qs, bs, ks):
            if n_cols == 0:
                a_blks.append(jnp.zeros((sub, c_len), F32))
                continue
            ref = b[i * sub - 1:i * sub] if i else jnp.zeros((1, LANES), F32)
            qt = (q[rows] * jnp.exp2(b[rows] - ref)).astype(BF16)
            kt = k[:n_cols] * jnp.exp2(ref - b[:n_cols])
            if n_cols < c_len:
                kt = jnp.concatenate([kt, jnp.zeros((c_len - n_cols, LANES), F32)], axis=0)
            a_blks.append(_nt_dot(qt, kt.astype(BF16)))
        if bounded_decay:
            scored = [jnp.where(causal <= i * sub, a, 0.0) for a in a_blks]
        else:
            c_rows = lambda h, s, i=i: bkc_ref[h, 2, pl.ds(i * sub + s, SUBLANES, stride=0), :]
            scored = _hgrn_diag_scores(a_blks, [q[rows] for q in qs], [b[rows] for b in bs],
                                       c_rows, i * sub)
        for h in range(n_heads):
            a_rows[h].append(scored[h])
    for h, (q, b, k) in enumerate(zip(qs, bs, ks)):
        b_last = b[c_len - 1:c_len]
        a_ref[h] = jnp.concatenate(a_rows[h], axis=0).astype(BF16)
        qdec_ref[h] = (q * jnp.exp2(b)).astype(BF16)
        kend_ref[h] = (k * jnp.exp2(b_last - b)).astype(BF16)
        dec_ref[h] = jnp.exp2(b_last)


def _hgrn_state(gw, v_bf, g, st, a_ref, qdec_ref, kend_ref, dec_ref):
    o = (jnp.dot(a_ref[...], v_bf, preferred_element_type=F32)
         + _nt_dot(qdec_ref[...], st.astype(BF16)))
    y = o * _rms_scale(o) * gw
    out = (y * (g * jax.nn.sigmoid(g))).astype(BF16)
    st_new = st * dec_ref[...] + jnp.dot(v_bf.astype(F32).T.astype(BF16), kend_ref[...],
                                         preferred_element_type=F32)
    return out, st_new


def _mix_kernel(lq1_ref, lk1_ref, lq2_ref, lk2_ref, sw_ref, q1_ref, q2_ref, k1_ref, k2_ref,
                vt_ref, lbl_ref, gw_ref, hq_ref, hf_ref, hi_ref, hg_ref, oda_ref, ohg_ref,
                s_ref, mx_ref, st_ref, bkc_ref, a_ref, qdec_ref, kend_ref, dec_ref):
    step = pl.program_id(1)

    c_len = HGRN_CHUNK
    n_heads, t_len = hq_ref.shape[1], hq_ref.shape[2]
    n_chunks = t_len // c_len
    chunks_per_step = ohg_ref.shape[1] // c_len
    assert chunks_per_step % 2 == 0
    l = lbl_ref[...]
    e = jnp.exp(l - jnp.max(l, axis=0, keepdims=True))
    lb_all = jnp.sum(e[:LAYER + 1], axis=0, keepdims=True) / jnp.sum(e, axis=0, keepdims=True)
    gw = gw_ref[...]
    tri = (lax.broadcasted_iota(jnp.int32, (c_len, c_len), 0)
           >= lax.broadcasted_iota(jnp.int32, (c_len, c_len), 1)).astype(BF16)

    def rows_of(c):
        c = jnp.minimum(c, n_chunks - 1)
        return pl.ds(pl.multiple_of(c * c_len, c_len), c_len)

    def stage0(c, par):
        rows = rows_of(c)
        for h in range(n_heads):
            _hgrn_decays(lb_all[:, h * LANES:(h + 1) * LANES], tri, hf_ref[0, h, rows, :],
                         bkc_ref.at[par, h])

    def stage1(c, par, bounded_decay):
        rows = rows_of(c)
        _hgrn_intra([hq_ref[0, h, rows, :].astype(F32) for h in range(n_heads)], bkc_ref.at[par],
                    a_ref.at[par], qdec_ref.at[par], kend_ref.at[par], dec_ref.at[par],
                    bounded_decay)

    def stage2(c, par):
        rows = rows_of(c)
        out_rows = pl.ds(pl.multiple_of((c - step * chunks_per_step) * c_len, c_len), c_len)
        for h in range(n_heads):
            out, st_new = _hgrn_state(gw, hi_ref[0, h, rows, :], hg_ref[0, h, rows, :].astype(F32),
                                      st_ref[h], a_ref.at[par, h], qdec_ref.at[par, h],
                                      kend_ref.at[par, h], dec_ref.at[par, h])
            ohg_ref[0, out_rows, h * LANES:(h + 1) * LANES] = out
            st_ref[h] = st_new

    def hgrn_stages(bounded_decay):
        calls = []
        for r in range(chunks_per_step // 2):
            c = step * chunks_per_step + 2 * r
            calls += [lambda c=c: stage0(c + 2, 0), lambda c=c: stage2(c, 0),
                      lambda c=c: stage1(c + 1, 1, bounded_decay),
                      lambda c=c: stage0(c + 3, 1), lambda c=c: stage2(c + 1, 1),
                      lambda c=c: stage1(c + 2, 0, bounded_decay)]
        return calls

    t = ATT_T
    nq = q1_ref.shape[2] // t
    lam = (jnp.exp(jnp.sum(lq1_ref[...] * lk1_ref[...], axis=-1, keepdims=True))
           - jnp.exp(jnp.sum(lq2_ref[...] * lk2_ref[...], axis=-1, keepdims=True))
           + LAMBDA_INIT)
    q_refs = (q1_ref, q2_ref)
    k_refs = (k1_ref, k2_ref)
    ones = jnp.ones((BF16_ROWS, t), BF16)
    causal = (lax.broadcasted_iota(jnp.int32, (t, t), 0)
              <= lax.broadcasted_iota(jnp.int32, (t, t), 1))
    tiles = [(qi, j) for qi in range(nq) for j in range(qi + 1)]
    n_buf = ATT_LOOKAHEAD + 1

    def key_max(s):
        quarter = t // 4
        parts = [s[i * quarter:(i + 1) * quarter] for i in range(4)]
        part = jnp.maximum(jnp.maximum(parts[0], parts[1]), jnp.maximum(parts[2], parts[3]))
        return jnp.max(part, axis=0, keepdims=True)

    def produce(n):
        qi, j = tiles[n]
        for mp in range(2):
            s = _nt_dot(k_refs[mp][0, 0, j * t:(j + 1) * t, :],
                        q_refs[mp][0, 0, qi * t:(qi + 1) * t, :])
            if j == qi:
                s = jnp.where(causal, s, NEG_BIG)
            s_ref[n % n_buf, mp] = s
            mx_ref[n % n_buf, mp] = key_max(s)

    def finalize(qi, stats):
        (_, acc1), (_, acc2) = stats
        o = (acc1[:LANES] * (1.0 / acc1[LANES:LANES + 1])
             - acc2[:LANES] * (lam / acc2[LANES:LANES + 1]))
        y = o * lax.rsqrt(jnp.mean(o * o, axis=0, keepdims=True) + EPS) * sw_ref[...]
        oda_ref[0, qi * t:(qi + 1) * t, :] = (y * (1.0 - LAMBDA_INIT)).T.astype(oda_ref.dtype)

    def run(bounded_decay):
        @pl.when(step == 0)
        def _():
            st_ref[...] = jnp.zeros_like(st_ref)
            stage0(0, 0)
            stage1(0, 0, bounded_decay)
            stage0(1, 1)

        stages = hgrn_stages(bounded_decay)
        for n in range(min(ATT_LOOKAHEAD, len(tiles))):
            produce(n)
        stats = None
        issued = 0
        for n, (qi, j) in enumerate(tiles):
            if n + ATT_LOOKAHEAD < len(tiles):
                produce(n + ATT_LOOKAHEAD)
            vta = jnp.concatenate([vt_ref[0, 0, j], ones], axis=0)
            new_stats = []
            for mp in range(2):
                s = s_ref[n % n_buf, mp]
                mx = mx_ref[n % n_buf, mp]
                if j == 0:
                    m_new = mx
                    acc = jnp.dot(vta, jnp.exp2(s - m_new).astype(BF16),
                                  preferred_element_type=F32)
                else:
                    m_old, acc_old = stats[mp]
                    m_new = jnp.maximum(m_old, mx)
                    acc = (jnp.exp2(m_old - m_new) * acc_old
                           + jnp.dot(vta, jnp.exp2(s - m_new).astype(BF16),
                                     preferred_element_type=F32))
                new_stats.append((m_new, acc))
            stats = new_stats
            if j == qi:
                finalize(qi, stats)
            due = (n + 1) * len(stages) // len(tiles)
            while issued < due:
                stages[issued]()
                issued += 1

    bounded = HGRN_SUB * jnp.max(-jnp.log2(lb_all)) <= HGRN_MAX_SUB_DECAY_LOG2
    pl.when(bounded)(lambda: run(True))
    pl.when(jnp.logical_not(bounded))(lambda: run(False))


def _mix(oa, of, ovt, lb_logits, gnorm_w, lq1, lk1, lq2, lk2, subln_w):
    bsz, _, t, _ = oa.shape
    tq = ATT_T
    assert DA_HEADS == HG_HEADS
    width = HG_HEADS * LANES
    vec = lambda n: pl.BlockSpec((1, n), lambda b, h: (0, 0))
    head_slab = lambda slot: pl.BlockSpec((1, 1, t, LANES), lambda b, h: (b, slot + h, 0, 0))
    group_slab = lambda slot: pl.BlockSpec((1, HG_HEADS, t, LANES),
                                           lambda b, h: (b, slot // HG_HEADS, 0, 0))
    return pl.pallas_call(
        _mix_kernel,
        grid=(bsz, DA_HEADS),
        in_specs=[vec(DA_HEAD_DIM), vec(DA_HEAD_DIM), vec(DA_HEAD_DIM), vec(DA_HEAD_DIM),
                  pl.BlockSpec((2 * DA_HEAD_DIM, 1), lambda b, h: (0, 0)),
                  head_slab(SLOT_AQ1), head_slab(SLOT_AQ2), head_slab(SLOT_AK1),
                  head_slab(SLOT_AK2),
                  pl.BlockSpec((1, 1, t // tq, LANES, tq), lambda b, h: (b, h, 0, 0, 0)),
                  pl.BlockSpec((lb_logits.shape[0], width), lambda b, h: (0, 0)),
                  pl.BlockSpec((1, LANES), lambda b, h: (0, 0)),
                  group_slab(SLOT_HQ),
                  pl.BlockSpec((1, HG_HEADS, t, LANES), lambda b, h: (b, 0, 0, 0)),
                  group_slab(SLOT_HI), group_slab(SLOT_HG)],
        out_specs=[pl.BlockSpec((1, t, LANES), lambda b, h: (b, 0, h)),
                   pl.BlockSpec((1, t // DA_HEADS, width), lambda b, h: (b, h, 0))],
        out_shape=[jax.ShapeDtypeStruct((bsz, t, DA_HEADS * LANES), BF16),
                   jax.ShapeDtypeStruct((bsz, t, width), BF16)],
        scratch_shapes=[pltpu.VMEM((ATT_LOOKAHEAD + 1, 2, tq, tq), F32),
                        pltpu.VMEM((ATT_LOOKAHEAD + 1, 2, 1, tq), F32),
                        pltpu.VMEM((HG_HEADS, HG_DIM, HG_DIM), F32),
                        pltpu.VMEM((2, HG_HEADS, 3, HGRN_CHUNK, LANES), F32),
                        pltpu.VMEM((2, HG_HEADS, HGRN_CHUNK, HGRN_CHUNK), BF16),
                        pltpu.VMEM((2, HG_HEADS, HGRN_CHUNK, LANES), BF16),
                        pltpu.VMEM((2, HG_HEADS, HGRN_CHUNK, LANES), BF16),
                        pltpu.VMEM((2, HG_HEADS, 1, LANES), F32)],
        compiler_params=pltpu.CompilerParams(
            dimension_semantics=("arbitrary", "arbitrary"), vmem_limit_bytes=VMEM_LIMIT_BYTES),
        name="mix",
    )(lq1, lk1, lq2, lk2, subln_w.reshape(2 * DA_HEAD_DIM, 1), oa, oa, oa, oa, ovt,
      lb_logits, gnorm_w.reshape(1, LANES), oa, of, oa, oa)


def _ffn_kernel(x_ref, mod_ref, ohg_ref, oda_ref, wo_ref, n2_ref, wg_ref, wu_ref, wd_ref, fw_ref,
                o_ref):
    m = mod_ref[0]
    part = x_ref.shape[1] // FFN_PARTS
    rows = [slice(r * part, (r + 1) * part) for r in range(FFN_PARTS)]

    def residual(r):
        mix = jnp.dot(jnp.concatenate([ohg_ref[0, rows[r]], oda_ref[0, rows[r]]], axis=1),
                      wo_ref[...], preferred_element_type=F32)
        return x_ref[0, rows[r]] + m[2:3] * mix

    def modulated(h):
        return (h * _rms_scale(h) * n2_ref[...] * (1.0 + m[4:5]) + m[3:4]).astype(BF16)

    def swiglu(u):
        acc = jnp.zeros((part, wd_ref.shape[1]), F32)
        for c in range(wg_ref.shape[1] // FFN_FC):
            sl = slice(c * FFN_FC, (c + 1) * FFN_FC)
            gate = jnp.dot(u, wg_ref[:, sl], preferred_element_type=F32)
            up = jnp.dot(u, wu_ref[:, sl], preferred_element_type=F32)
            act = (gate * jax.nn.sigmoid(gate) * up).astype(BF16)
            acc = acc + jnp.dot(act, wd_ref[sl, :], preferred_element_type=F32)
        return acc

    hs = [residual(r) for r in range(FFN_PARTS)]
    us = [modulated(h) for h in hs]
    for r in range(FFN_PARTS):
        h2 = hs[r] + m[5:6] * swiglu(us[r])
        o_ref[0, rows[r]] = h2 * _rms_scale(h2) * fw_ref[...]


def _ffn(x, mod3, o_hg, o_da, w_out_bf, norm2_w, wg_bf, wu_bf, wd_bf, final_w):
    bsz, t, d = x.shape
    tm = FFN_TM
    dff = wg_bf.shape[1]
    assert dff % FFN_FC == 0
    tile = lambda w: pl.BlockSpec((1, tm, w), lambda b, i: (b, i, 0))
    const = lambda r, c: pl.BlockSpec((r, c), lambda b, i: (0, 0), pipeline_mode=pl.Buffered(1))
    return pl.pallas_call(
        _ffn_kernel,
        grid=(bsz, t // tm),
        in_specs=[tile(d),
                  pl.BlockSpec((1, N_MOD, d), lambda b, i: (b, 0, 0)),
                  tile(o_hg.shape[2]), tile(o_da.shape[2]),
                  const(d, d), const(1, d), const(d, dff), const(d, dff), const(dff, d),
                  const(1, d)],
        out_specs=tile(d),
        out_shape=jax.ShapeDtypeStruct((bsz, t, d), F32),
        compiler_params=pltpu.CompilerParams(
            dimension_semantics=("arbitrary", "arbitrary"), vmem_limit_bytes=VMEM_LIMIT_BYTES),
        name="ffn",
    )(x, mod3, o_hg, o_da, w_out_bf, norm2_w.reshape(1, d), wg_bf, wu_bf, wd_bf,
      final_w.reshape(1, d))


def kernel(x, c, w_ada, b_ada, norm1_w, w_in, hgrn_lb_logits, hgrn_gnorm_w, diff_lambda_q1,
           diff_lambda_k1, diff_lambda_q2, diff_lambda_k2, diff_subln_w, w_out, norm2_w,
           w_ffn_gate, w_ffn_up, w_ffn_down, final_norm_w):
    bsz, t, d = x.shape
    assert w_ada.shape[0] == 1, "single-layer trunk"
    l = LAYER
    mod, (w_in_bf, w_out_bf) = _ada(c, w_ada[l], b_ada[l], [w_in[l], w_out[l]])
    mod3 = mod.reshape(bsz, N_MOD, d)
    oa, of, ovt, (wg_bf, wu_bf, wd_bf) = _inproj(
        x, mod3, norm1_w[l], w_in_bf, [w_ffn_gate[l], w_ffn_up[l], w_ffn_down[l]])
    o_da, o_hg = _mix(oa, of, ovt, hgrn_lb_logits, hgrn_gnorm_w[l], diff_lambda_q1[l:l + 1],
                      diff_lambda_k1[l:l + 1], diff_lambda_q2[l:l + 1], diff_lambda_k2[l:l + 1],
                      diff_subln_w[l])
    return _ffn(x, mod3, o_hg, o_da, w_out_bf, norm2_w[l], wg_bf, wu_bf, wd_bf, final_norm_w)
```

```python
import math

import jax
import jax.numpy as jnp
import numpy as np
from jax import lax
from jax.experimental import pallas as pl
from jax.experimental.pallas import tpu as pltpu

F32 = jnp.float32
BF16 = jnp.bfloat16

HG_HEADS = 4
HG_DIM = 128
DA_HEADS = 4
DA_HEAD_DIM = 64
N_MOD = 6
EPS = 1e-6
LAYER = 0
LAMBDA_INIT = 0.8 - 0.6 * math.exp(-0.3 * LAYER)
LOG2_E = math.log2(math.e)

LANES = 128
SUBLANES = 8
BF16_ROWS = 16
V7X_VMEM_BYTES = 64 * 1024 * 1024
VMEM_LIMIT_BYTES = V7X_VMEM_BYTES * 7 // 8

HGRN_CHUNK = 64
HGRN_SUB = 8
HGRN_MAX_SUB_DECAY_LOG2 = 64.0
ADA_TN = 768
INPROJ_TM = 1024
INPROJ_PARTS = 2
FFN_TM = 1024
FFN_PARTS = 2
FFN_FC = 256
ATT_T = 256
ATT_LOOKAHEAD = 2
NEG_BIG = -1e30
N_SLOPE_PIECES = 3


def _nt_dot(a, b):
    return lax.dot_general(a, b, (((1,), (1,)), ((), ())), preferred_element_type=F32)


def _rms_scale(x):
    return lax.rsqrt(jnp.mean(x * x, axis=-1, keepdims=True) + EPS)


def _alibi_slope(head):
    return 2.0 ** (-8.0 * (head + 1) / DA_HEADS)


def _cast_specs(weights, n_steps, step_of):
    in_specs, out_specs, out_shapes = [], [], []
    for w in weights:
        rows, cols = w.shape
        n_blocks = n_steps
        while rows % n_blocks or (rows // n_blocks) % BF16_ROWS:
            n_blocks -= 1
        index = lambda *g, n_blocks=n_blocks: (jnp.minimum(step_of(*g), n_blocks - 1), 0)
        in_specs.append(pl.BlockSpec((rows // n_blocks, cols), index))
        out_specs.append(pl.BlockSpec((rows // n_blocks, cols), index))
        out_shapes.append(jax.ShapeDtypeStruct(w.shape, BF16))
    return in_specs, out_specs, out_shapes


def _cast_blocks(in_refs, out_refs):
    for src, dst in zip(in_refs, out_refs):
        dst[...] = src[...].astype(dst.dtype)


def _ada_kernel(c_ref, w_ref, b_ref, *refs):
    n_cast = (len(refs) - 1) // 2
    o_ref = refs[n_cast]
    c = c_ref[...]
    a = c * jax.nn.sigmoid(c)
    o_ref[...] = jnp.dot(a, w_ref[...], preferred_element_type=F32,
                         precision=lax.Precision.HIGHEST) + b_ref[...]
    _cast_blocks(refs[:n_cast], refs[n_cast + 1:])


def _ada(c, w_ada, b_ada, cast_weights):
    bsz, d = c.shape
    n = w_ada.shape[1]
    tn = ADA_TN
    cast_in, cast_out, cast_shapes = _cast_specs(cast_weights, n // tn, lambda j: j)
    mod, *cast = pl.pallas_call(
        _ada_kernel,
        grid=(n // tn,),
        in_specs=[pl.BlockSpec((bsz, d), lambda j: (0, 0)),
                  pl.BlockSpec((d, tn), lambda j: (0, j)),
                  pl.BlockSpec((1, tn), lambda j: (0, j))] + cast_in,
        out_specs=[pl.BlockSpec((bsz, tn), lambda j: (0, j))] + cast_out,
        out_shape=[jax.ShapeDtypeStruct((bsz, n), F32)] + cast_shapes,
        compiler_params=pltpu.CompilerParams(
            dimension_semantics=("arbitrary",), vmem_limit_bytes=VMEM_LIMIT_BYTES),
        name="ada",
    )(c, w_ada, b_ada.reshape(1, n), *cast_weights)
    return mod, cast


SLOT_HQ, SLOT_HI, SLOT_HG, SLOT_AQ1, SLOT_AQ2, SLOT_AK1, SLOT_AK2 = 0, 4, 8, 12, 16, 20, 24
N_SLABS = 28
GRP_HQ, GRP_HF, GRP_HI, GRP_HG, GRP_AQ, GRP_AK, GRP_AV = range(7)


def _bf16_pieces(value, n):
    rest = np.float32(value)
    pieces = []
    for _ in range(n):
        piece = np.float32(np.asarray(rest, dtype=BF16))
        pieces.append(float(piece))
        rest = np.float32(rest - piece)
    return pieces


def _aug_lanes(lane, values, first_map):
    base = DA_HEAD_DIM if first_map else 0
    aug = jnp.zeros(lane.shape, F32)
    for n, e in enumerate(values):
        aug = jnp.where(lane == base + n, e, aug)
    return aug


def _inproj_kernel(x_ref, mod_ref, n1_ref, w_ref, *refs):
    n_cast = (len(refs) - 3) // 2
    oa_ref, of_ref, ovt_ref = refs[n_cast:n_cast + 3]
    _cast_blocks(refs[:n_cast], refs[n_cast + 3:])
    m = mod_ref[0]
    tm = x_ref.shape[1] // INPROJ_PARTS
    for part in range(INPROJ_PARTS):
        _inproj_rows(part, tm, x_ref[0, part * tm:(part + 1) * tm], m, n1_ref, w_ref, oa_ref,
                     of_ref, ovt_ref)


def _inproj_rows(part, tm, x, m, n1_ref, w_ref, oa_ref, of_ref, ovt_ref):
    rows = slice(part * tm, (part + 1) * tm)
    y = x * _rms_scale(x) * n1_ref[...]
    u = (y * (1.0 + m[1:2]) + m[0:1]).astype(BF16)

    lane = lax.broadcasted_iota(jnp.int32, (tm, LANES), 1)
    keep = (lane < DA_HEAD_DIM, lane >= DA_HEAD_DIM)
    maps = ((SLOT_AQ1, SLOT_AK1, True), (SLOT_AQ2, SLOT_AK2, False))
    pos = ((pl.program_id(1) * INPROJ_PARTS + part) * tm
           + lax.broadcasted_iota(jnp.int32, (tm, LANES), 0)).astype(F32)
    pos_hi = pos.astype(BF16).astype(F32)
    pos_pieces = [pos_hi, pos - pos_hi]
    k_aug = [_aug_lanes(lane, pos_pieces * N_SLOPE_PIECES, first) for _, _, first in maps]
    q_aug = []
    for head in range(DA_HEADS):
        c_lanes = [c for c in _bf16_pieces(LOG2_E * _alibi_slope(head), N_SLOPE_PIECES)
                   for _ in pos_pieces]
        q_aug.append([_aug_lanes(lane, c_lanes, first) for _, _, first in maps])

    for j in range(w_ref.shape[1] // (2 * LANES)):
        p = jnp.dot(u, w_ref[:, j * 2 * LANES:(j + 1) * 2 * LANES], preferred_element_type=F32)
        for half in range(2):
            ph = p[:, half * LANES:(half + 1) * LANES]
            grp, head = divmod(2 * j + half, 4)
            if grp == GRP_HF:
                of_ref[0, head, rows] = ph
            elif grp == GRP_AQ:
                ph = ph * (LOG2_E * DA_HEAD_DIM ** -0.5)
                for mp, (q_slot, _, _) in enumerate(maps):
                    oa_ref[0, q_slot + head, rows] = jnp.where(
                        keep[mp], ph, q_aug[head][mp]).astype(BF16)
            elif grp == GRP_AK:
                for mp, (_, k_slot, _) in enumerate(maps):
                    oa_ref[0, k_slot + head, rows] = jnp.where(
                        keep[mp], ph, k_aug[mp]).astype(BF16)
            elif grp == GRP_AV:
                vt = ph.T
                for jt in range(tm // ATT_T):
                    ovt_ref[0, head, part * (tm // ATT_T) + jt] = (
                        vt[:, jt * ATT_T:(jt + 1) * ATT_T].astype(BF16))
            else:
                slot = {GRP_HQ: SLOT_HQ, GRP_HI: SLOT_HI, GRP_HG: SLOT_HG}[grp]
                oa_ref[0, slot + head, rows] = ph.astype(BF16)


def _inproj(x, mod3, norm1_w, w_bf, cast_weights):
    bsz, t, d = x.shape
    n = w_bf.shape[1]
    tm = INPROJ_TM
    nt = t // tm
    cast_in, cast_out, cast_shapes = _cast_specs(cast_weights, bsz * nt, lambda b, i: b * nt + i)
    oa, of, ovt, *cast = pl.pallas_call(
        _inproj_kernel,
        grid=(bsz, nt),
        in_specs=[pl.BlockSpec((1, tm, d), lambda b, i: (b, i, 0)),
                  pl.BlockSpec((1, N_MOD, d), lambda b, i: (b, 0, 0)),
                  pl.BlockSpec((1, d), lambda b, i: (0, 0)),
                  pl.BlockSpec((d, n), lambda b, i: (0, 0), pipeline_mode=pl.Buffered(1))]
        + cast_in,
        out_specs=[pl.BlockSpec((1, N_SLABS, tm, LANES), lambda b, i: (b, 0, i, 0)),
                   pl.BlockSpec((1, HG_HEADS, tm, LANES), lambda b, i: (b, 0, i, 0)),
                   pl.BlockSpec((1, DA_HEADS, tm // ATT_T, LANES, ATT_T),
                                lambda b, i: (b, 0, i, 0, 0))] + cast_out,
        out_shape=[jax.ShapeDtypeStruct((bsz, N_SLABS, t, LANES), BF16),
                   jax.ShapeDtypeStruct((bsz, HG_HEADS, t, LANES), F32),
                   jax.ShapeDtypeStruct((bsz, DA_HEADS, t // ATT_T, LANES, ATT_T), BF16)]
        + cast_shapes,
        compiler_params=pltpu.CompilerParams(
            dimension_semantics=("arbitrary", "arbitrary"), vmem_limit_bytes=VMEM_LIMIT_BYTES),
        name="inproj",
    )(x, mod3, norm1_w.reshape(1, d), w_bf, *cast_weights)
    return oa, of, ovt, cast


def _hgrn_diag_scores(a_blks, q_blks, b_blks, c_rows, col0):
    sub, chunk = a_blks[0].shape
    n_sl = sub // SUBLANES
    row = lax.broadcasted_iota(jnp.int32, (SUBLANES, chunk), 0)
    lane = lax.broadcasted_iota(jnp.int32, (SUBLANES, chunk), 1)
    sl = [slice(r * SUBLANES, (r + 1) * SUBLANES) for r in range(n_sl)]
    parts = [[a[sl[r]] for r in range(n_sl)] for a in a_blks]
    for s in range(sub):
        first = s // SUBLANES
        here = lane == col0 + s
        here_diag = here & (row >= s - first * SUBLANES)
        for h, (qb, bb) in enumerate(zip(q_blks, b_blks)):
            cs = c_rows(h, s)
            for r in range(first, n_sl):
                col = jnp.sum(qb[sl[r]] * jnp.exp2(bb[sl[r]] - cs), axis=-1, keepdims=True)
                parts[h][r] = jnp.where(here_diag if r == first else here, col, parts[h][r])
    return [jnp.concatenate(p, axis=0) if n_sl > 1 else p[0] for p in parts]


def _hgrn_decays(lb, tri, f_logit, bkc_ref):
    f = lb + (1.0 - lb) * jax.nn.sigmoid(f_logit)
    logf = jnp.log2(f)
    k = 1.0 - f
    h1 = logf.astype(BF16)
    r1 = logf - h1.astype(F32)
    h2 = r1.astype(BF16)
    h3 = (r1 - h2.astype(F32)).astype(BF16)
    cs = jnp.dot(tri, jnp.concatenate([h1, h2, h3], axis=1), preferred_element_type=F32)
    b = cs[:, :LANES] + cs[:, LANES:2 * LANES] + cs[:, 2 * LANES:]
    bkc_ref[0] = b
    bkc_ref[1] = k
    bkc_ref[2] = b - jnp.log2(k)


def _hgrn_intra(qs, bkc_ref, a_ref, qdec_ref, kend_ref, dec_ref, bounded_decay):
    c_len, sub = HGRN_CHUNK, HGRN_SUB
    n_heads = len(qs)
    bs = [bkc_ref[h, 0] for h in range(n_heads)]
    ks = [bkc_ref[h, 1] for h in range(n_heads)]
    a_rows = [[] for _ in range(n_heads)]
    causal = (lax.broadcasted_iota(jnp.int32, (sub, c_len), 1)
              - lax.broadcasted_iota(jnp.int32, (sub, c_len), 0))
    for i in range(c_len // sub):
        rows = slice(i * sub, (i + 1) * sub)
        n_cols = (i + 1) * sub if bounded_decay else i * sub
        a_blks = []
        for q, b, k in zip(qs, bs, ks):
            if n_cols == 0:
                a_blks.append(jnp.zeros((sub, c_len), F32))
                continue
            ref = b[i * sub - 1:i * sub] if i else jnp.zeros((1, LANES), F32)
            qt = (q[rows] * jnp.exp2(b[rows] - ref)).astype(BF16)
            kt = k[:n_cols] * jnp.exp2(ref - b[:n_cols])
            if n_cols < c_len:
                kt = jnp.concatenate([kt, jnp.zeros((c_len - n_cols, LANES), F32)], axis=0)
            a_blks.append(_nt_dot(qt, kt.astype(BF16)))
        if bounded_decay:
            scored = [jnp.where(causal <= i * sub, a, 0.0) for a in a_blks]
        else:
            c_rows = lambda h, s, i=i: bkc_ref[h, 2, pl.ds(i * sub + s, SUBLANES, stride=0), :]
            scored = _hgrn_diag_scores(a_blks, [q[rows] for q in qs], [b[rows] for b in bs],
                                       c_rows, i * sub)
        for h in range(n_heads):
            a_rows[h].append(scored[h])
    for h, (q, b, k) in enumerate(zip(qs, bs, ks)):
        b_last = b[c_len - 1:c_len]
        a_ref[h] = jnp.concatenate(a_rows[h], axis=0).astype(BF16)
        qdec_ref[h] = (q * jnp.exp2(b)).astype(BF16)
        kend_ref[h] = (k * jnp.exp2(b_last - b)).astype(BF16)
        dec_ref[h] = jnp.exp2(b_last)


def _hgrn_state(gw, v_bf, g, st, a_ref, qdec_ref, kend_ref, dec_ref):
    o = (jnp.dot(a_ref[...], v_bf, preferred_element_type=F32)
         + _nt_dot(qdec_ref[...], st.astype(BF16)))
    y = o * _rms_scale(o) * gw
    out = (y * (g * jax.nn.sigmoid(g))).astype(BF16)
    st_new = st * dec_ref[...] + jnp.dot(v_bf.astype(F32).T.astype(BF16), kend_ref[...],
                                         preferred_element_type=F32)
    return out, st_new


def _mix_kernel(lq1_ref, lk1_ref, lq2_ref, lk2_ref, sw_ref, q1_ref, q2_ref, k1_ref, k2_ref,
                vt_ref, lbl_ref, gw_ref, hq_ref, hf_ref, hi_ref, hg_ref, oda_ref, ohg_ref,
                s_ref, mx_ref, st_ref, bkc_ref, a_ref, qdec_ref, kend_ref, dec_ref):
    step = pl.program_id(1)

    c_len = HGRN_CHUNK
    n_heads, t_len = hq_ref.shape[1], hq_ref.shape[2]
    n_chunks = t_len // c_len
    chunks_per_step = ohg_ref.shape[1] // c_len
    assert chunks_per_step % 2 == 0
    l = lbl_ref[...]
    e = jnp.exp(l - jnp.max(l, axis=0, keepdims=True))
    lb_all = jnp.sum(e[:LAYER + 1], axis=0, keepdims=True) / jnp.sum(e, axis=0, keepdims=True)
    gw = gw_ref[...]
    tri = (lax.broadcasted_iota(jnp.int32, (c_len, c_len), 0)
           >= lax.broadcasted_iota(jnp.int32, (c_len, c_len), 1)).astype(BF16)

    def rows_of(c):
        c = jnp.minimum(c, n_chunks - 1)
        return pl.ds(pl.multiple_of(c * c_len, c_len), c_len)

    def stage0(c, par):
        rows = rows_of(c)
        for h in range(n_heads):
            _hgrn_decays(lb_all[:, h * LANES:(h + 1) * LANES], tri, hf_ref[0, h, rows, :],
                         bkc_ref.at[par, h])

    def stage1(c, par, bounded_decay):
        rows = rows_of(c)
        _hgrn_intra([hq_ref[0, h, rows, :].astype(F32) for h in range(n_heads)], bkc_ref.at[par],
                    a_ref.at[par], qdec_ref.at[par], kend_ref.at[par], dec_ref.at[par],
                    bounded_decay)

    def stage2(c, par):
        rows = rows_of(c)
        out_rows = pl.ds(pl.multiple_of((c - step * chunks_per_step) * c_len, c_len), c_len)
        for h in range(n_heads):
            out, st_new = _hgrn_state(gw, hi_ref[0, h, rows, :], hg_ref[0, h, rows, :].astype(F32),
                                      st_ref[h], a_ref.at[par, h], qdec_ref.at[par, h],
                                      kend_ref.at[par, h], dec_ref.at[par, h])
            ohg_ref[0, out_rows, h * LANES:(h + 1) * LANES] = out
            st_ref[h] = st_new

    def hgrn_stages(bounded_decay):
        calls = []
        for r in range(chunks_per_step // 2):
            c = step * chunks_per_step + 2 * r
            calls += [lambda c=c: stage0(c + 2, 0), lambda c=c: stage2(c, 0),
                      lambda c=c: stage1(c + 1, 1, bounded_decay),
                      lambda c=c: stage0(c + 3, 1), lambda c=c: stage2(c + 1, 1),
                      lambda c=c: stage1(c + 2, 0, bounded_decay)]
        return calls

    t = ATT_T
    nq = q1_ref.shape[2] // t
    lam = (jnp.exp(jnp.sum(lq1_ref[...] * lk1_ref[...], axis=-1, keepdims=True))
           - jnp.exp(jnp.sum(lq2_ref[...] * lk2_ref[...], axis=-1, keepdims=True))
           + LAMBDA_INIT)
    q_refs = (q1_ref, q2_ref)
    k_refs = (k1_ref, k2_ref)
    ones = jnp.ones((BF16_ROWS, t), BF16)
    causal = (lax.broadcasted_iota(jnp.int32, (t, t), 0)
              <= lax.broadcasted_iota(jnp.int32, (t, t), 1))
    tiles = [(qi, j) for qi in range(nq) for j in range(qi + 1)]
    n_buf = ATT_LOOKAHEAD + 1

    def key_max(s):
        quarter = t // 4
        parts = [s[i * quarter:(i + 1) * quarter] for i in range(4)]
        part = jnp.maximum(jnp.maximum(parts[0], parts[1]), jnp.maximum(parts[2], parts[3]))
        return jnp.max(part, axis=0, keepdims=True)

    def produce(n):
        qi, j = tiles[n]
        for mp in range(2):
            s = _nt_dot(k_refs[mp][0, 0, j * t:(j + 1) * t, :],
                        q_refs[mp][0, 0, qi * t:(qi + 1) * t, :])
            if j == qi:
                s = jnp.where(causal, s, NEG_BIG)
            s_ref[n % n_buf, mp] = s
            mx_ref[n % n_buf, mp] = key_max(s)

    def finalize(qi, stats):
        (_, acc1), (_, acc2) = stats
        o = (acc1[:LANES] * (1.0 / acc1[LANES:LANES + 1])
             - acc2[:LANES] * (lam / acc2[LANES:LANES + 1]))
        y = o * lax.rsqrt(jnp.mean(o * o, axis=0, keepdims=True) + EPS) * sw_ref[...]
        oda_ref[0, qi * t:(qi + 1) * t, :] = (y * (1.0 - LAMBDA_INIT)).T.astype(oda_ref.dtype)

    def run(bounded_decay):
        @pl.when(step == 0)
        def _():
            st_ref[...] = jnp.zeros_like(st_ref)
            stage0(0, 0)
            stage1(0, 0, bounded_decay)
            stage0(1, 1)

        stages = hgrn_stages(bounded_decay)
        for n in range(min(ATT_LOOKAHEAD, len(tiles))):
            produce(n)
        stats = None
        issued = 0
        for n, (qi, j) in enumerate(tiles):
            if n + ATT_LOOKAHEAD < len(tiles):
                produce(n + ATT_LOOKAHEAD)
            vta = jnp.concatenate([vt_ref[0, 0, j], ones], axis=0)
            new_stats = []
            for mp in range(2):
                s = s_ref[n % n_buf, mp]
                mx = mx_ref[n % n_buf, mp]
                if j == 0:
                    m_new = mx
                    acc = jnp.dot(vta, jnp.exp2(s - m_new).astype(BF16),
                                  preferred_element_type=F32)
                else:
                    m_old, acc_old = stats[mp]
                    m_new = jnp.maximum(m_old, mx)
                    acc = (jnp.exp2(m_old - m_new) * acc_old
                           + jnp.dot(vta, jnp.exp2(s - m_new).astype(BF16),
                                     preferred_element_type=F32))
                new_stats.append((m_new, acc))
            stats = new_stats
            if j == qi:
                finalize(qi, stats)
            due = (n + 1) * len(stages) // len(tiles)
            while issued < due:
                stages[issued]()
                issued += 1

    bounded = HGRN_SUB * jnp.max(-jnp.log2(lb_all)) <= HGRN_MAX_SUB_DECAY_LOG2
    pl.when(bounded)(lambda: run(True))
    pl.when(jnp.logical_not(bounded))(lambda: run(False))


def _mix(oa, of, ovt, lb_logits, gnorm_w, lq1, lk1, lq2, lk2, subln_w):
    bsz, _, t, _ = oa.shape
    tq = ATT_T
    assert DA_HEADS == HG_HEADS
    width = HG_HEADS * LANES
    vec = lambda n: pl.BlockSpec((1, n), lambda b, h: (0, 0))
    head_slab = lambda slot: pl.BlockSpec((1, 1, t, LANES), lambda b, h: (b, slot + h, 0, 0))
    group_slab = lambda slot: pl.BlockSpec((1, HG_HEADS, t, LANES),
                                           lambda b, h: (b, slot // HG_HEADS, 0, 0))
    return pl.pallas_call(
        _mix_kernel,
        grid=(bsz, DA_HEADS),
        in_specs=[vec(DA_HEAD_DIM), vec(DA_HEAD_DIM), vec(DA_HEAD_DIM), vec(DA_HEAD_DIM),
                  pl.BlockSpec((2 * DA_HEAD_DIM, 1), lambda b, h: (0, 0)),
                  head_slab(SLOT_AQ1), head_slab(SLOT_AQ2), head_slab(SLOT_AK1),
                  head_slab(SLOT_AK2),
                  pl.BlockSpec((1, 1, t // tq, LANES, tq), lambda b, h: (b, h, 0, 0, 0)),
                  pl.BlockSpec((lb_logits.shape[0], width), lambda b, h: (0, 0)),
                  pl.BlockSpec((1, LANES), lambda b, h: (0, 0)),
                  group_slab(SLOT_HQ),
                  pl.BlockSpec((1, HG_HEADS, t, LANES), lambda b, h: (b, 0, 0, 0)),
                  group_slab(SLOT_HI), group_slab(SLOT_HG)],
        out_specs=[pl.BlockSpec((1, t, LANES), lambda b, h: (b, 0, h)),
                   pl.BlockSpec((1, t // DA_HEADS, width), lambda b, h: (b, h, 0))],
        out_shape=[jax.ShapeDtypeStruct((bsz, t, DA_HEADS * LANES), BF16),
                   jax.ShapeDtypeStruct((bsz, t, width), BF16)],
        scratch_shapes=[pltpu.VMEM((ATT_LOOKAHEAD + 1, 2, tq, tq), F32),
                        pltpu.VMEM((ATT_LOOKAHEAD + 1, 2, 1, tq), F32),
                        pltpu.VMEM((HG_HEADS, HG_DIM, HG_DIM), F32),
                        pltpu.VMEM((2, HG_HEADS, 3, HGRN_CHUNK, LANES), F32),
                        pltpu.VMEM((2, HG_HEADS, HGRN_CHUNK, HGRN_CHUNK), BF16),
                        pltpu.VMEM((2, HG_HEADS, HGRN_CHUNK, LANES), BF16),
                        pltpu.VMEM((2, HG_HEADS, HGRN_CHUNK, LANES), BF16),
                        pltpu.VMEM((2, HG_HEADS, 1, LANES), F32)],
        compiler_params=pltpu.CompilerParams(
            dimension_semantics=("arbitrary", "arbitrary"), vmem_limit_bytes=VMEM_LIMIT_BYTES),
        name="mix",
    )(lq1, lk1, lq2, lk2, subln_w.reshape(2 * DA_HEAD_DIM, 1), oa, oa, oa, oa, ovt,
      lb_logits, gnorm_w.reshape(1, LANES), oa, of, oa, oa)


def _ffn_kernel(x_ref, mod_ref, ohg_ref, oda_ref, wo_ref, n2_ref, wg_ref, wu_ref, wd_ref, fw_ref,
                o_ref):
    m = mod_ref[0]
    part = x_ref.shape[1] // FFN_PARTS
    rows = [slice(r * part, (r + 1) * part) for r in range(FFN_PARTS)]

    def residual(r):
        mix = jnp.dot(jnp.concatenate([ohg_ref[0, rows[r]], oda_ref[0, rows[r]]], axis=1),
                      wo_ref[...], preferred_element_type=F32)
        return x_ref[0, rows[r]] + m[2:3] * mix

    def modulated(h):
        return (h * _rms_scale(h) * n2_ref[...] * (1.0 + m[4:5]) + m[3:4]).astype(BF16)

    def swiglu(u):
        acc = jnp.zeros((part, wd_ref.shape[1]), F32)
        for c in range(wg_ref.shape[1] // FFN_FC):
            sl = slice(c * FFN_FC, (c + 1) * FFN_FC)
            gate = jnp.dot(u, wg_ref[:, sl], preferred_element_type=F32)
            up = jnp.dot(u, wu_ref[:, sl], preferred_element_type=F32)
            act = (gate * jax.nn.sigmoid(gate) * up).astype(BF16)
            acc = acc + jnp.dot(act, wd_ref[sl, :], preferred_element_type=F32)
        return acc

    hs = [residual(r) for r in range(FFN_PARTS)]
    us = [modulated(h) for h in hs]
    for r in range(FFN_PARTS):
        h2 = hs[r] + m[5:6] * swiglu(us[r])
        o_ref[0, rows[r]] = h2 * _rms_scale(h2) * fw_ref[...]


def _ffn(x, mod3, o_hg, o_da, w_out_bf, norm2_w, wg_bf, wu_bf, wd_bf, final_w):
    bsz, t, d = x.shape
    tm = FFN_TM
    dff = wg_bf.shape[1]
    assert dff % FFN_FC == 0
    tile = lambda w: pl.BlockSpec((1, tm, w), lambda b, i: (b, i, 0))
    const = lambda r, c: pl.BlockSpec((r, c), lambda b, i: (0, 0), pipeline_mode=pl.Buffered(1))
    return pl.pallas_call(
        _ffn_kernel,
        grid=(bsz, t // tm),
        in_specs=[tile(d),
                  pl.BlockSpec((1, N_MOD, d), lambda b, i: (b, 0, 0)),
                  tile(o_hg.shape[2]), tile(o_da.shape[2]),
                  const(d, d), const(1, d), const(d, dff), const(d, dff), const(dff, d),
                  const(1, d)],
        out_specs=tile(d),
        out_shape=jax.ShapeDtypeStruct((bsz, t, d), F32),
        compiler_params=pltpu.CompilerParams(
            dimension_semantics=("arbitrary", "arbitrary"), vmem_limit_bytes=VMEM_LIMIT_BYTES),
        name="ffn",
    )(x, mod3, o_hg, o_da, w_out_bf, norm2_w.reshape(1, d), wg_bf, wu_bf, wd_bf,
      final_w.reshape(1, d))


def kernel(x, c, w_ada, b_ada, norm1_w, w_in, hgrn_lb_logits, hgrn_gnorm_w, diff_lambda_q1,
           diff_lambda_k1, diff_lambda_q2, diff_lambda_k2, diff_subln_w, w_out, norm2_w,
           w_ffn_gate, w_ffn_up, w_ffn_down, final_norm_w):
    bsz, t, d = x.shape
    assert w_ada.shape[0] == 1, "single-layer trunk"
    l = LAYER
    mod, (w_in_bf, w_out_bf) = _ada(c, w_ada[l], b_ada[l], [w_in[l], w_out[l]])
    mod3 = mod.reshape(bsz, N_MOD, d)
    oa, of, ovt, (wg_bf, wu_bf, wd_bf) = _inproj(
        x, mod3, norm1_w[l], w_in_bf, [w_ffn_gate[l], w_ffn_up[l], w_ffn_down[l]])
    o_da, o_hg = _mix(oa, of, ovt, hgrn_lb_logits, hgrn_gnorm_w[l], diff_lambda_q1[l:l + 1],
                      diff_lambda_k1[l:l + 1], diff_lambda_q2[l:l + 1], diff_lambda_k2[l:l + 1],
                      diff_subln_w[l])
    return _ffn(x, mod3, o_hg, o_da, w_out_bf, norm2_w[l], wg_bf, wu_bf, wd_bf, final_norm_w)
```

```python
import math

import jax
import jax.numpy as jnp
import numpy as np
from jax import lax
from jax.experimental import pallas as pl
from jax.experimental.pallas import tpu as pltpu

F32 = jnp.float32
BF16 = jnp.bfloat16

HG_HEADS = 4
HG_DIM = 128
DA_HEADS = 4
DA_HEAD_DIM = 64
N_MOD = 6
EPS = 1e-6
LAYER = 0
LAMBDA_INIT = 0.8 - 0.6 * math.exp(-0.3 * LAYER)
LOG2_E = math.log2(math.e)

LANES = 128
SUBLANES = 8
BF16_ROWS = 16
V7X_VMEM_BYTES = 64 * 1024 * 1024
VMEM_LIMIT_BYTES = V7X_VMEM_BYTES * 7 // 8

HGRN_CHUNK = 64
HGRN_SUB = 16
HGRN_MAX_SUB_DECAY_LOG2 = 96.0
ADA_TN = 768
INPROJ_TM = 1024
INPROJ_PARTS = 2
FFN_TM = 1024
FFN_PARTS = 2
FFN_FC = 256
ATT_T = 256
ATT_LOOKAHEAD = 2
NEG_BIG = -1e30
N_SLOPE_PIECES = 3


def _nt_dot(a, b):
    return lax.dot_general(a, b, (((1,), (1,)), ((), ())), preferred_element_type=F32)


def _rms_scale(x):
    return lax.rsqrt(jnp.mean(x * x, axis=-1, keepdims=True) + EPS)


def _alibi_slope(head):
    return 2.0 ** (-8.0 * (head + 1) / DA_HEADS)


def _cast_specs(weights, n_steps, step_of):
    in_specs, out_specs, out_shapes = [], [], []
    for w in weights:
        rows, cols = w.shape
        n_blocks = n_steps
        while rows % n_blocks or (rows // n_blocks) % BF16_ROWS:
            n_blocks -= 1
        index = lambda *g, n_blocks=n_blocks: (jnp.minimum(step_of(*g), n_blocks - 1), 0)
        in_specs.append(pl.BlockSpec((rows // n_blocks, cols), index))
        out_specs.append(pl.BlockSpec((rows // n_blocks, cols), index))
        out_shapes.append(jax.ShapeDtypeStruct(w.shape, BF16))
    return in_specs, out_specs, out_shapes


def _cast_blocks(in_refs, out_refs):
    for src, dst in zip(in_refs, out_refs):
        dst[...] = src[...].astype(dst.dtype)


def _ada_kernel(c_ref, w_ref, b_ref, *refs):
    n_cast = (len(refs) - 1) // 2
    o_ref = refs[n_cast]
    c = c_ref[...]
    a = c * jax.nn.sigmoid(c)
    o_ref[...] = jnp.dot(a, w_ref[...], preferred_element_type=F32,
                         precision=lax.Precision.HIGHEST) + b_ref[...]
    _cast_blocks(refs[:n_cast], refs[n_cast + 1:])


def _ada(c, w_ada, b_ada, cast_weights):
    bsz, d = c.shape
    n = w_ada.shape[1]
    tn = ADA_TN
    cast_in, cast_out, cast_shapes = _cast_specs(cast_weights, n // tn, lambda j: j)
    mod, *cast = pl.pallas_call(
        _ada_kernel,
        grid=(n // tn,),
        in_specs=[pl.BlockSpec((bsz, d), lambda j: (0, 0)),
                  pl.BlockSpec((d, tn), lambda j: (0, j)),
                  pl.BlockSpec((1, tn), lambda j: (0, j))] + cast_in,
        out_specs=[pl.BlockSpec((bsz, tn), lambda j: (0, j))] + cast_out,
        out_shape=[jax.ShapeDtypeStruct((bsz, n), F32)] + cast_shapes,
        compiler_params=pltpu.CompilerParams(
            dimension_semantics=("arbitrary",), vmem_limit_bytes=VMEM_LIMIT_BYTES),
        name="ada",
    )(c, w_ada, b_ada.reshape(1, n), *cast_weights)
    return mod, cast


SLOT_HQ, SLOT_HI, SLOT_HG, SLOT_AQ1, SLOT_AQ2, SLOT_AK1, SLOT_AK2 = 0, 4, 8, 12, 16, 20, 24
N_SLABS = 28
GRP_HQ, GRP_HF, GRP_HI, GRP_HG, GRP_AQ, GRP_AK, GRP_AV = range(7)


def _bf16_pieces(value, n):
    rest = np.float32(value)
    pieces = []
    for _ in range(n):
        piece = np.float32(np.asarray(rest, dtype=BF16))
        pieces.append(float(piece))
        rest = np.float32(rest - piece)
    return pieces


def _aug_lanes(lane, values, first_map):
    base = DA_HEAD_DIM if first_map else 0
    aug = jnp.zeros(lane.shape, F32)
    for n, e in enumerate(values):
        aug = jnp.where(lane == base + n, e, aug)
    return aug


def _inproj_kernel(x_ref, mod_ref, n1_ref, w_ref, *refs):
    n_cast = (len(refs) - 3) // 2
    oa_ref, of_ref, ovt_ref = refs[n_cast:n_cast + 3]
    _cast_blocks(refs[:n_cast], refs[n_cast + 3:])
    m = mod_ref[0]
    tm = x_ref.shape[1] // INPROJ_PARTS
    for part in range(INPROJ_PARTS):
        _inproj_rows(part, tm, x_ref[0, part * tm:(part + 1) * tm], m, n1_ref, w_ref, oa_ref,
                     of_ref, ovt_ref)


def _inproj_rows(part, tm, x, m, n1_ref, w_ref, oa_ref, of_ref, ovt_ref):
    rows = slice(part * tm, (part + 1) * tm)
    y = x * _rms_scale(x) * n1_ref[...]
    u = (y * (1.0 + m[1:2]) + m[0:1]).astype(BF16)

    lane = lax.broadcasted_iota(jnp.int32, (tm, LANES), 1)
    keep = (lane < DA_HEAD_DIM, lane >= DA_HEAD_DIM)
    maps = ((SLOT_AQ1, SLOT_AK1, True), (SLOT_AQ2, SLOT_AK2, False))
    pos = ((pl.program_id(1) * INPROJ_PARTS + part) * tm
           + lax.broadcasted_iota(jnp.int32, (tm, LANES), 0)).astype(F32)
    pos_hi = pos.astype(BF16).astype(F32)
    pos_pieces = [pos_hi, pos - pos_hi]
    k_aug = [_aug_lanes(lane, pos_pieces * N_SLOPE_PIECES, first) for _, _, first in maps]
    q_aug = []
    for head in range(DA_HEADS):
        c_lanes = [c for c in _bf16_pieces(LOG2_E * _alibi_slope(head), N_SLOPE_PIECES)
                   for _ in pos_pieces]
        q_aug.append([_aug_lanes(lane, c_lanes, first) for _, _, first in maps])

    for j in range(w_ref.shape[1] // (2 * LANES)):
        p = jnp.dot(u, w_ref[:, j * 2 * LANES:(j + 1) * 2 * LANES], preferred_element_type=F32)
        for half in range(2):
            ph = p[:, half * LANES:(half + 1) * LANES]
            grp, head = divmod(2 * j + half, 4)
            if grp == GRP_HF:
                of_ref[0, head, rows] = ph
            elif grp == GRP_AQ:
                ph = ph * (LOG2_E * DA_HEAD_DIM ** -0.5)
                for mp, (q_slot, _, _) in enumerate(maps):
                    oa_ref[0, q_slot + head, rows] = jnp.where(
                        keep[mp], ph, q_aug[head][mp]).astype(BF16)
            elif grp == GRP_AK:
                for mp, (_, k_slot, _) in enumerate(maps):
                    oa_ref[0, k_slot + head, rows] = jnp.where(
                        keep[mp], ph, k_aug[mp]).astype(BF16)
            elif grp == GRP_AV:
                vt = ph.T
                for jt in range(tm // ATT_T):
                    ovt_ref[0, head, part * (tm // ATT_T) + jt] = (
                        vt[:, jt * ATT_T:(jt + 1) * ATT_T].astype(BF16))
            else:
                slot = {GRP_HQ: SLOT_HQ, GRP_HI: SLOT_HI, GRP_HG: SLOT_HG}[grp]
                oa_ref[0, slot + head, rows] = ph.astype(BF16)


def _inproj(x, mod3, norm1_w, w_bf, cast_weights):
    bsz, t, d = x.shape
    n = w_bf.shape[1]
    tm = INPROJ_TM
    nt = t // tm
    cast_in, cast_out, cast_shapes = _cast_specs(cast_weights, bsz * nt, lambda b, i: b * nt + i)
    oa, of, ovt, *cast = pl.pallas_call(
        _inproj_kernel,
        grid=(bsz, nt),
        in_specs=[pl.BlockSpec((1, tm, d), lambda b, i: (b, i, 0)),
                  pl.BlockSpec((1, N_MOD, d), lambda b, i: (b, 0, 0)),
                  pl.BlockSpec((1, d), lambda b, i: (0, 0)),
                  pl.BlockSpec((d, n), lambda b, i: (0, 0), pipeline_mode=pl.Buffered(1))]
        + cast_in,
        out_specs=[pl.BlockSpec((1, N_SLABS, tm, LANES), lambda b, i: (b, 0, i, 0)),
                   pl.BlockSpec((1, HG_HEADS, tm, LANES), lambda b, i: (b, 0, i, 0)),
                   pl.BlockSpec((1, DA_HEADS, tm // ATT_T, LANES, ATT_T),
                                lambda b, i: (b, 0, i, 0, 0))] + cast_out,
        out_shape=[jax.ShapeDtypeStruct((bsz, N_SLABS, t, LANES), BF16),
                   jax.ShapeDtypeStruct((bsz, HG_HEADS, t, LANES), F32),
                   jax.ShapeDtypeStruct((bsz, DA_HEADS, t // ATT_T, LANES, ATT_T), BF16)]
        + cast_shapes,
        compiler_params=pltpu.CompilerParams(
            dimension_semantics=("arbitrary", "arbitrary"), vmem_limit_bytes=VMEM_LIMIT_BYTES),
        name="inproj",
    )(x, mod3, norm1_w.reshape(1, d), w_bf, *cast_weights)
    return oa, of, ovt, cast


def _hgrn_diag_scores(a_blks, q_blks, b_blks, c_rows, col0):
    sub, chunk = a_blks[0].shape
    n_sl = sub // SUBLANES
    row = lax.broadcasted_iota(jnp.int32, (SUBLANES, chunk), 0)
    lane = lax.broadcasted_iota(jnp.int32, (SUBLANES, chunk), 1)
    sl = [slice(r * SUBLANES, (r + 1) * SUBLANES) for r in range(n_sl)]
    parts = [[a[sl[r]] for r in range(n_sl)] for a in a_blks]
    for s in range(sub):
        first = s // SUBLANES
        here = lane == col0 + s
        here_diag = here & (row >= s - first * SUBLANES)
        for h, (qb, bb) in enumerate(zip(q_blks, b_blks)):
            cs = c_rows(h, s)
            for r in range(first, n_sl):
                col = jnp.sum(qb[sl[r]] * jnp.exp2(bb[sl[r]] - cs), axis=-1, keepdims=True)
                parts[h][r] = jnp.where(here_diag if r == first else here, col, parts[h][r])
    return [jnp.concatenate(p, axis=0) if n_sl > 1 else p[0] for p in parts]


def _hgrn_decays(lb, tri, f_logit, bkc_ref):
    f = lb + (1.0 - lb) * jax.nn.sigmoid(f_logit)
    logf = jnp.log2(f)
    k = 1.0 - f
    h1 = logf.astype(BF16)
    r1 = logf - h1.astype(F32)
    h2 = r1.astype(BF16)
    h3 = (r1 - h2.astype(F32)).astype(BF16)
    cs = jnp.dot(tri, jnp.concatenate([h1, h2, h3], axis=1), preferred_element_type=F32)
    b = cs[:, :LANES] + cs[:, LANES:2 * LANES] + cs[:, 2 * LANES:]
    bkc_ref[0] = b
    bkc_ref[1] = k
    bkc_ref[2] = b - jnp.log2(k)


def _hgrn_intra(qs, bkc_ref, a_ref, qdec_ref, kend_ref, dec_ref, bounded_decay):
    c_len, sub = HGRN_CHUNK, HGRN_SUB
    n_heads = len(qs)
    bs = [bkc_ref[h, 0] for h in range(n_heads)]
    ks = [bkc_ref[h, 1] for h in range(n_heads)]
    a_rows = [[] for _ in range(n_heads)]
    causal = (lax.broadcasted_iota(jnp.int32, (sub, c_len), 1)
              - lax.broadcasted_iota(jnp.int32, (sub, c_len), 0))
    for i in range(c_len // sub):
        rows = slice(i * sub, (i + 1) * sub)
        n_cols = (i + 1) * sub if bounded_decay else i * sub
        a_blks = []
        for q, b, k in zip(qs, bs, ks):
            if n_cols == 0:
                a_blks.append(jnp.zeros((sub, c_len), F32))
                continue
            ref = b[i * sub - 1:i * sub] if i else jnp.zeros((1, LANES), F32)
            qt = (q[rows] * jnp.exp2(b[rows] - ref)).astype(BF16)
            kt = k[:n_cols] * jnp.exp2(ref - b[:n_cols])
            if n_cols < c_len:
                kt = jnp.concatenate([kt, jnp.zeros((c_len - n_cols, LANES), F32)], axis=0)
            a_blks.append(_nt_dot(qt, kt.astype(BF16)))
        if bounded_decay:
            scored = [jnp.where(causal <= i * sub, a, 0.0) for a in a_blks]
        else:
            c_rows = lambda h, s, i=i: bkc_ref[h, 2, pl.ds(i * sub + s, SUBLANES, stride=0), :]
            scored = _hgrn_diag_scores(a_blks, [q[rows] for q in qs], [b[rows] for b in bs],
                                       c_rows, i * sub)
        for h in range(n_heads):
            a_rows[h].append(scored[h])
    for h, (q, b, k) in enumerate(zip(qs, bs, ks)):
        b_last = b[c_len - 1:c_len]
        a_ref[h] = jnp.concatenate(a_rows[h], axis=0).astype(BF16)
        qdec_ref[h] = (q * jnp.exp2(b)).astype(BF16)
        kend_ref[h] = (k * jnp.exp2(b_last - b)).astype(BF16)
        dec_ref[h] = jnp.exp2(b_last)


def _hgrn_state(gw, v_bf, g, st, a_ref, qdec_ref, kend_ref, dec_ref):
    o = (jnp.dot(a_ref[...], v_bf, preferred_element_type=F32)
         + _nt_dot(qdec_ref[...], st.astype(BF16)))
    y = o * _rms_scale(o) * gw
    out = (y * (g * jax.nn.sigmoid(g))).astype(BF16)
    st_new = st * dec_ref[...] + jnp.dot(v_bf.astype(F32).T.astype(BF16), kend_ref[...],
                                         preferred_element_type=F32)
    return out, st_new


def _mix_kernel(lq1_ref, lk1_ref, lq2_ref, lk2_ref, sw_ref, q1_ref, q2_ref, k1_ref, k2_ref,
                vt_ref, lbl_ref, gw_ref, hq_ref, hf_ref, hi_ref, hg_ref, oda_ref, ohg_ref,
                s_ref, mx_ref, st_ref, bkc_ref, a_ref, qdec_ref, kend_ref, dec_ref):
    step = pl.program_id(1)

    c_len = HGRN_CHUNK
    n_heads, t_len = hq_ref.shape[1], hq_ref.shape[2]
    n_chunks = t_len // c_len
    chunks_per_step = ohg_ref.shape[1] // c_len
    assert chunks_per_step % 2 == 0
    l = lbl_ref[...]
    e = jnp.exp(l - jnp.max(l, axis=0, keepdims=True))
    lb_all = jnp.sum(e[:LAYER + 1], axis=0, keepdims=True) / jnp.sum(e, axis=0, keepdims=True)
    gw = gw_ref[...]
    tri = (lax.broadcasted_iota(jnp.int32, (c_len, c_len), 0)
           >= lax.broadcasted_iota(jnp.int32, (c_len, c_len), 1)).astype(BF16)

    def rows_of(c):
        c = jnp.minimum(c, n_chunks - 1)
        return pl.ds(pl.multiple_of(c * c_len, c_len), c_len)

    def stage0(c, par):
        rows = rows_of(c)
        for h in range(n_heads):
            _hgrn_decays(lb_all[:, h * LANES:(h + 1) * LANES], tri, hf_ref[0, h, rows, :],
                         bkc_ref.at[par, h])

    def stage1(c, par, bounded_decay):
        rows = rows_of(c)
        _hgrn_intra([hq_ref[0, h, rows, :].astype(F32) for h in range(n_heads)], bkc_ref.at[par],
                    a_ref.at[par], qdec_ref.at[par], kend_ref.at[par], dec_ref.at[par],
                    bounded_decay)

    def stage2(c, par):
        rows = rows_of(c)
        out_rows = pl.ds(pl.multiple_of((c - step * chunks_per_step) * c_len, c_len), c_len)
        for h in range(n_heads):
            out, st_new = _hgrn_state(gw, hi_ref[0, h, rows, :], hg_ref[0, h, rows, :].astype(F32),
                                      st_ref[h], a_ref.at[par, h], qdec_ref.at[par, h],
                                      kend_ref.at[par, h], dec_ref.at[par, h])
            ohg_ref[0, out_rows, h * LANES:(h + 1) * LANES] = out
            st_ref[h] = st_new

    def hgrn_stages(bounded_decay):
        calls = []
        for r in range(chunks_per_step // 2):
            c = step * chunks_per_step + 2 * r
            calls += [lambda c=c: stage0(c + 2, 0), lambda c=c: stage2(c, 0),
                      lambda c=c: stage1(c + 1, 1, bounded_decay),
                      lambda c=c: stage0(c + 3, 1), lambda c=c: stage2(c + 1, 1),
                      lambda c=c: stage1(c + 2, 0, bounded_decay)]
        return calls

    t = ATT_T
    nq = q1_ref.shape[2] // t
    lam = (jnp.exp(jnp.sum(lq1_ref[...] * lk1_ref[...], axis=-1, keepdims=True))
           - jnp.exp(jnp.sum(lq2_ref[...] * lk2_ref[...], axis=-1, keepdims=True))
           + LAMBDA_INIT)
    q_refs = (q1_ref, q2_ref)
    k_refs = (k1_ref, k2_ref)
    ones = jnp.ones((BF16_ROWS, t), BF16)
    causal = (lax.broadcasted_iota(jnp.int32, (t, t), 0)
              <= lax.broadcasted_iota(jnp.int32, (t, t), 1))
    tiles = [(qi, j) for qi in range(nq) for j in range(qi + 1)]
    n_buf = ATT_LOOKAHEAD + 1

    def key_max(s):
        quarter = t // 4
        parts = [s[i * quarter:(i + 1) * quarter] for i in range(4)]
        part = jnp.maximum(jnp.maximum(parts[0], parts[1]), jnp.maximum(parts[2], parts[3]))
        return jnp.max(part, axis=0, keepdims=True)

    def produce(n):
        qi, j = tiles[n]
        for mp in range(2):
            s = _nt_dot(k_refs[mp][0, 0, j * t:(j + 1) * t, :],
                        q_refs[mp][0, 0, qi * t:(qi + 1) * t, :])
            if j == qi:
                s = jnp.where(causal, s, NEG_BIG)
            s_ref[n % n_buf, mp] = s
            mx_ref[n % n_buf, mp] = key_max(s)

    def finalize(qi, stats):
        (_, acc1), (_, acc2) = stats
        o = (acc1[:LANES] * (1.0 / acc1[LANES:LANES + 1])
             - acc2[:LANES] * (lam / acc2[LANES:LANES + 1]))
        y = o * lax.rsqrt(jnp.mean(o * o, axis=0, keepdims=True) + EPS) * sw_ref[...]
        oda_ref[0, qi * t:(qi + 1) * t, :] = (y * (1.0 - LAMBDA_INIT)).T.astype(oda_ref.dtype)

    def run(bounded_decay):
        @pl.when(step == 0)
        def _():
            st_ref[...] = jnp.zeros_like(st_ref)
            stage0(0, 0)
            stage1(0, 0, bounded_decay)
            stage0(1, 1)

        stages = hgrn_stages(bounded_decay)
        for n in range(min(ATT_LOOKAHEAD, len(tiles))):
            produce(n)
        stats = None
        issued = 0
        for n, (qi, j) in enumerate(tiles):
            if n + ATT_LOOKAHEAD < len(tiles):
                produce(n + ATT_LOOKAHEAD)
            vta = jnp.concatenate([vt_ref[0, 0, j], ones], axis=0)
            new_stats = []
            for mp in range(2):
                s = s_ref[n % n_buf, mp]
                mx = mx_ref[n % n_buf, mp]
                if j == 0:
                    m_new = mx
                    acc = jnp.dot(vta, jnp.exp2(s - m_new).astype(BF16),
                                  preferred_element_type=F32)
                else:
                    m_old, acc_old = stats[mp]
                    m_new = jnp.maximum(m_old, mx)
                    acc = (jnp.exp2(m_old - m_new) * acc_old
                           + jnp.dot(vta, jnp.exp2(s - m_new).astype(BF16),
                                     preferred_element_type=F32))
                new_stats.append((m_new, acc))
            stats = new_stats
            if j == qi:
                finalize(qi, stats)
            due = (n + 1) * len(stages) // len(tiles)
            while issued < due:
                stages[issued]()
                issued += 1

    bounded = HGRN_SUB * jnp.max(-jnp.log2(lb_all)) <= HGRN_MAX_SUB_DECAY_LOG2
    pl.when(bounded)(lambda: run(True))
    pl.when(jnp.logical_not(bounded))(lambda: run(False))


def _mix(oa, of, ovt, lb_logits, gnorm_w, lq1, lk1, lq2, lk2, subln_w):
    bsz, _, t, _ = oa.shape
    tq = ATT_T
    assert DA_HEADS == HG_HEADS
    width = HG_HEADS * LANES
    vec = lambda n: pl.BlockSpec((1, n), lambda b, h: (0, 0))
    head_slab = lambda slot: pl.BlockSpec((1, 1, t, LANES), lambda b, h: (b, slot + h, 0, 0))
    group_slab = lambda slot: pl.BlockSpec((1, HG_HEADS, t, LANES),
                                           lambda b, h: (b, slot // HG_HEADS, 0, 0))
    return pl.pallas_call(
        _mix_kernel,
        grid=(bsz, DA_HEADS),
        in_specs=[vec(DA_HEAD_DIM), vec(DA_HEAD_DIM), vec(DA_HEAD_DIM), vec(DA_HEAD_DIM),
                  pl.BlockSpec((2 * DA_HEAD_DIM, 1), lambda b, h: (0, 0)),
                  head_slab(SLOT_AQ1), head_slab(SLOT_AQ2), head_slab(SLOT_AK1),
                  head_slab(SLOT_AK2),
                  pl.BlockSpec((1, 1, t // tq, LANES, tq), lambda b, h: (b, h, 0, 0, 0)),
                  pl.BlockSpec((lb_logits.shape[0], width), lambda b, h: (0, 0)),
                  pl.BlockSpec((1, LANES), lambda b, h: (0, 0)),
                  group_slab(SLOT_HQ),
                  pl.BlockSpec((1, HG_HEADS, t, LANES), lambda b, h: (b, 0, 0, 0)),
                  group_slab(SLOT_HI), group_slab(SLOT_HG)],
        out_specs=[pl.BlockSpec((1, t, LANES), lambda b, h: (b, 0, h)),
                   pl.BlockSpec((1, t // DA_HEADS, width), lambda b, h: (b, h, 0))],
        out_shape=[jax.ShapeDtypeStruct((bsz, t, DA_HEADS * LANES), BF16),
                   jax.ShapeDtypeStruct((bsz, t, width), BF16)],
        scratch_shapes=[pltpu.VMEM((ATT_LOOKAHEAD + 1, 2, tq, tq), F32),
                        pltpu.VMEM((ATT_LOOKAHEAD + 1, 2, 1, tq), F32),
                        pltpu.VMEM((HG_HEADS, HG_DIM, HG_DIM), F32),
                        pltpu.VMEM((2, HG_HEADS, 3, HGRN_CHUNK, LANES), F32),
                        pltpu.VMEM((2, HG_HEADS, HGRN_CHUNK, HGRN_CHUNK), BF16),
                        pltpu.VMEM((2, HG_HEADS, HGRN_CHUNK, LANES), BF16),
                        pltpu.VMEM((2, HG_HEADS, HGRN_CHUNK, LANES), BF16),
                        pltpu.VMEM((2, HG_HEADS, 1, LANES), F32)],
        compiler_params=pltpu.CompilerParams(
            dimension_semantics=("arbitrary", "arbitrary"), vmem_limit_bytes=VMEM_LIMIT_BYTES),
        name="mix",
    )(lq1, lk1, lq2, lk2, subln_w.reshape(2 * DA_HEAD_DIM, 1), oa, oa, oa, oa, ovt,
      lb_logits, gnorm_w.reshape(1, LANES), oa, of, oa, oa)


def _ffn_kernel(x_ref, mod_ref, ohg_ref, oda_ref, wo_ref, n2_ref, wg_ref, wu_ref, wd_ref, fw_ref,
                o_ref):
    m = mod_ref[0]
    part = x_ref.shape[1] // FFN_PARTS
    rows = [slice(r * part, (r + 1) * part) for r in range(FFN_PARTS)]

    def residual(r):
        mix = jnp.dot(jnp.concatenate([ohg_ref[0, rows[r]], oda_ref[0, rows[r]]], axis=1),
                      wo_ref[...], preferred_element_type=F32)
        return x_ref[0, rows[r]] + m[2:3] * mix

    def modulated(h):
        return (h * _rms_scale(h) * n2_ref[...] * (1.0 + m[4:5]) + m[3:4]).astype(BF16)

    def swiglu(u):
        acc = jnp.zeros((part, wd_ref.shape[1]), F32)
        for c in range(wg_ref.shape[1] // FFN_FC):
            sl = slice(c * FFN_FC, (c + 1) * FFN_FC)
            gate = jnp.dot(u, wg_ref[:, sl], preferred_element_type=F32)
            up = jnp.dot(u, wu_ref[:, sl], preferred_element_type=F32)
            act = (gate * jax.nn.sigmoid(gate) * up).astype(BF16)
            acc = acc + jnp.dot(act, wd_ref[sl, :], preferred_element_type=F32)
        return acc

    hs = [residual(r) for r in range(FFN_PARTS)]
    us = [modulated(h) for h in hs]
    for r in range(FFN_PARTS):
        h2 = hs[r] + m[5:6] * swiglu(us[r])
        o_ref[0, rows[r]] = h2 * _rms_scale(h2) * fw_ref[...]


def _ffn(x, mod3, o_hg, o_da, w_out_bf, norm2_w, wg_bf, wu_bf, wd_bf, final_w):
    bsz, t, d = x.shape
    tm = FFN_TM
    dff = wg_bf.shape[1]
    assert dff % FFN_FC == 0
    tile = lambda w: pl.BlockSpec((1, tm, w), lambda b, i: (b, i, 0))
    const = lambda r, c: pl.BlockSpec((r, c), lambda b, i: (0, 0), pipeline_mode=pl.Buffered(1))
    return pl.pallas_call(
        _ffn_kernel,
        grid=(bsz, t // tm),
        in_specs=[tile(d),
                  pl.BlockSpec((1, N_MOD, d), lambda b, i: (b, 0, 0)),
                  tile(o_hg.shape[2]), tile(o_da.shape[2]),
                  const(d, d), const(1, d), const(d, dff), const(d, dff), const(dff, d),
                  const(1, d)],
        out_specs=tile(d),
        out_shape=jax.ShapeDtypeStruct((bsz, t, d), F32),
        compiler_params=pltpu.CompilerParams(
            dimension_semantics=("arbitrary", "arbitrary"), vmem_limit_bytes=VMEM_LIMIT_BYTES),
        name="ffn",
    )(x, mod3, o_hg, o_da, w_out_bf, norm2_w.reshape(1, d), wg_bf, wu_bf, wd_bf,
      final_w.reshape(1, d))


def kernel(x, c, w_ada, b_ada, norm1_w, w_in, hgrn_lb_logits, hgrn_gnorm_w, diff_lambda_q1,
           diff_lambda_k1, diff_lambda_q2, diff_lambda_k2, diff_subln_w, w_out, norm2_w,
           w_ffn_gate, w_ffn_up, w_ffn_down, final_norm_w):
    bsz, t, d = x.shape
    assert w_ada.shape[0] == 1, "single-layer trunk"
    l = LAYER
    mod, (w_in_bf, w_out_bf) = _ada(c, w_ada[l], b_ada[l], [w_in[l], w_out[l]])
    mod3 = mod.reshape(bsz, N_MOD, d)
    oa, of, ovt, (wg_bf, wu_bf, wd_bf) = _inproj(
        x, mod3, norm1_w[l], w_in_bf, [w_ffn_gate[l], w_ffn_up[l], w_ffn_down[l]])
    o_da, o_hg = _mix(oa, of, ovt, hgrn_lb_logits, hgrn_gnorm_w[l], diff_lambda_q1[l:l + 1],
                      diff_lambda_k1[l:l + 1], diff_lambda_q2[l:l + 1], diff_lambda_k2[l:l + 1],
                      diff_subln_w[l])
    return _ffn(x, mod3, o_hg, o_da, w_out_bf, norm2_w[l], wg_bf, wu_bf, wd_bf, final_norm_w)
```

```python
import math

import jax
import jax.numpy as jnp
import numpy as np
from jax import lax
from jax.experimental import pallas as pl
from jax.experimental.pallas import tpu as pltpu

F32 = jnp.float32
BF16 = jnp.bfloat16

HG_HEADS = 4
HG_DIM = 128
DA_HEADS = 4
DA_HEAD_DIM = 64
N_MOD = 6
EPS = 1e-6
LAYER = 0
LAMBDA_INIT = 0.8 - 0.6 * math.exp(-0.3 * LAYER)
LOG2_E = math.log2(math.e)

LANES = 128
SUBLANES = 8
BF16_ROWS = 16
V7X_VMEM_BYTES = 64 * 1024 * 1024
VMEM_LIMIT_BYTES = V7X_VMEM_BYTES * 7 // 8

HGRN_CHUNK = 64
HGRN_SUB = 16
HGRN_MAX_SUB_DECAY_LOG2 = 96.0
ADA_TN = 768
INPROJ_TM = 1024
INPROJ_PARTS = 2
FFN_TM = 1024
FFN_PARTS = 2
FFN_FC = 256
ATT_T = 256
ATT_LOOKAHEAD = 2
NEG_BIG = -1e30
N_SLOPE_PIECES = 3


def _nt_dot(a, b):
    return lax.dot_general(a, b, (((1,), (1,)), ((), ())), preferred_element_type=F32)


def _rms_scale(x):
    return lax.rsqrt(jnp.mean(x * x, axis=-1, keepdims=True) + EPS)


def _alibi_slope(head):
    return 2.0 ** (-8.0 * (head + 1) / DA_HEADS)


def _cast_specs(weights, n_steps, step_of):
    in_specs, out_specs, out_shapes = [], [], []
    for w in weights:
        rows, cols = w.shape
        n_blocks = n_steps
        while rows % n_blocks or (rows // n_blocks) % BF16_ROWS:
            n_blocks -= 1
        index = lambda *g, n_blocks=n_blocks: (jnp.minimum(step_of(*g), n_blocks - 1), 0)
        in_specs.append(pl.BlockSpec((rows // n_blocks, cols), index))
        out_specs.append(pl.BlockSpec((rows // n_blocks, cols), index))
        out_shapes.append(jax.ShapeDtypeStruct(w.shape, BF16))
    return in_specs, out_specs, out_shapes


def _cast_blocks(in_refs, out_refs):
    for src, dst in zip(in_refs, out_refs):
        dst[...] = src[...].astype(dst.dtype)


def _ada_kernel(c_ref, w_ref, b_ref, *refs):
    n_cast = (len(refs) - 1) // 2
    o_ref = refs[n_cast]
    c = c_ref[...]
    a = c * jax.nn.sigmoid(c)
    o_ref[...] = jnp.dot(a, w_ref[...], preferred_element_type=F32,
                         precision=lax.Precision.HIGHEST) + b_ref[...]
    _cast_blocks(refs[:n_cast], refs[n_cast + 1:])


def _ada(c, w_ada, b_ada, cast_weights):
    bsz, d = c.shape
    n = w_ada.shape[1]
    tn = ADA_TN
    cast_in, cast_out, cast_shapes = _cast_specs(cast_weights, n // tn, lambda j: j)
    mod, *cast = pl.pallas_call(
        _ada_kernel,
        grid=(n // tn,),
        in_specs=[pl.BlockSpec((bsz, d), lambda j: (0, 0)),
                  pl.BlockSpec((d, tn), lambda j: (0, j)),
                  pl.BlockSpec((1, tn), lambda j: (0, j))] + cast_in,
        out_specs=[pl.BlockSpec((bsz, tn), lambda j: (0, j))] + cast_out,
        out_shape=[jax.ShapeDtypeStruct((bsz, n), F32)] + cast_shapes,
        compiler_params=pltpu.CompilerParams(
            dimension_semantics=("arbitrary",), vmem_limit_bytes=VMEM_LIMIT_BYTES),
        name="ada",
    )(c, w_ada, b_ada.reshape(1, n), *cast_weights)
    return mod, cast


SLOT_HQ, SLOT_HI, SLOT_HG, SLOT_AQ1, SLOT_AQ2, SLOT_AK1, SLOT_AK2 = 0, 4, 8, 12, 16, 20, 24
N_SLABS = 28
GRP_HQ, GRP_HF, GRP_HI, GRP_HG, GRP_AQ, GRP_AK, GRP_AV = range(7)


def _bf16_pieces(value, n):
    rest = np.float32(value)
    pieces = []
    for _ in range(n):
        piece = np.float32(np.asarray(rest, dtype=BF16))
        pieces.append(float(piece))
        rest = np.float32(rest - piece)
    return pieces


def _aug_lanes(lane, values, first_map):
    base = DA_HEAD_DIM if first_map else 0
    aug = jnp.zeros(lane.shape, F32)
    for n, e in enumerate(values):
        aug = jnp.where(lane == base + n, e, aug)
    return aug


def _inproj_kernel(x_ref, mod_ref, n1_ref, w_ref, *refs):
    n_cast = (len(refs) - 3) // 2
    oa_ref, of_ref, ovt_ref = refs[n_cast:n_cast + 3]
    _cast_blocks(refs[:n_cast], refs[n_cast + 3:])
    m = mod_ref[0]
    tm = x_ref.shape[1] // INPROJ_PARTS
    for part in range(INPROJ_PARTS):
        _inproj_rows(part, tm, x_ref[0, part * tm:(part + 1) * tm], m, n1_ref, w_ref, oa_ref,
                     of_ref, ovt_ref)


def _inproj_rows(part, tm, x, m, n1_ref, w_ref, oa_ref, of_ref, ovt_ref):
    rows = slice(part * tm, (part + 1) * tm)
    y = x * _rms_scale(x) * n1_ref[...]
    u = (y * (1.0 + m[1:2]) + m[0:1]).astype(BF16)

    lane = lax.broadcasted_iota(jnp.int32, (tm, LANES), 1)
    keep = (lane < DA_HEAD_DIM, lane >= DA_HEAD_DIM)
    maps = ((SLOT_AQ1, SLOT_AK1, True), (SLOT_AQ2, SLOT_AK2, False))
    pos = ((pl.program_id(1) * INPROJ_PARTS + part) * tm
           + lax.broadcasted_iota(jnp.int32, (tm, LANES), 0)).astype(F32)
    pos_hi = pos.astype(BF16).astype(F32)
    pos_pieces = [pos_hi, pos - pos_hi]
    k_aug = [_aug_lanes(lane, pos_pieces * N_SLOPE_PIECES, first) for _, _, first in maps]
    q_aug = []
    for head in range(DA_HEADS):
        c_lanes = [c for c in _bf16_pieces(LOG2_E * _alibi_slope(head), N_SLOPE_PIECES)
                   for _ in pos_pieces]
        q_aug.append([_aug_lanes(lane, c_lanes, first) for _, _, first in maps])

    for j in range(w_ref.shape[1] // (2 * LANES)):
        p = jnp.dot(u, w_ref[:, j * 2 * LANES:(j + 1) * 2 * LANES], preferred_element_type=F32)
        for half in range(2):
            ph = p[:, half * LANES:(half + 1) * LANES]
            grp, head = divmod(2 * j + half, 4)
            if grp == GRP_HF:
                of_ref[0, head, rows] = ph
            elif grp == GRP_AQ:
                ph = ph * (LOG2_E * DA_HEAD_DIM ** -0.5)
                for mp, (q_slot, _, _) in enumerate(maps):
                    oa_ref[0, q_slot + head, rows] = jnp.where(
                        keep[mp], ph, q_aug[head][mp]).astype(BF16)
            elif grp == GRP_AK:
                for mp, (_, k_slot, _) in enumerate(maps):
                    oa_ref[0, k_slot + head, rows] = jnp.where(
                        keep[mp], ph, k_aug[mp]).astype(BF16)
            elif grp == GRP_AV:
                vt = ph.T
                for jt in range(tm // ATT_T):
                    ovt_ref[0, head, part * (tm // ATT_T) + jt] = (
                        vt[:, jt * ATT_T:(jt + 1) * ATT_T].astype(BF16))
            else:
                slot = {GRP_HQ: SLOT_HQ, GRP_HI: SLOT_HI, GRP_HG: SLOT_HG}[grp]
                oa_ref[0, slot + head, rows] = ph.astype(BF16)


def _inproj(x, mod3, norm1_w, w_bf, cast_weights):
    bsz, t, d = x.shape
    n = w_bf.shape[1]
    tm = INPROJ_TM
    nt = t // tm
    cast_in, cast_out, cast_shapes = _cast_specs(cast_weights, bsz * nt, lambda b, i: b * nt + i)
    oa, of, ovt, *cast = pl.pallas_call(
        _inproj_kernel,
        grid=(bsz, nt),
        in_specs=[pl.BlockSpec((1, tm, d), lambda b, i: (b, i, 0)),
                  pl.BlockSpec((1, N_MOD, d), lambda b, i: (b, 0, 0)),
                  pl.BlockSpec((1, d), lambda b, i: (0, 0)),
                  pl.BlockSpec((d, n), lambda b, i: (0, 0), pipeline_mode=pl.Buffered(1))]
        + cast_in,
        out_specs=[pl.BlockSpec((1, N_SLABS, tm, LANES), lambda b, i: (b, 0, i, 0)),
                   pl.BlockSpec((1, HG_HEADS, tm, LANES), lambda b, i: (b, 0, i, 0)),
                   pl.BlockSpec((1, DA_HEADS, tm // ATT_T, LANES, ATT_T),
                                lambda b, i: (b, 0, i, 0, 0))] + cast_out,
        out_shape=[jax.ShapeDtypeStruct((bsz, N_SLABS, t, LANES), BF16),
                   jax.ShapeDtypeStruct((bsz, HG_HEADS, t, LANES), F32),
                   jax.ShapeDtypeStruct((bsz, DA_HEADS, t // ATT_T, LANES, ATT_T), BF16)]
        + cast_shapes,
        compiler_params=pltpu.CompilerParams(
            dimension_semantics=("arbitrary", "arbitrary"), vmem_limit_bytes=VMEM_LIMIT_BYTES),
        name="inproj",
    )(x, mod3, norm1_w.reshape(1, d), w_bf, *cast_weights)
    return oa, of, ovt, cast


def _hgrn_diag_scores(a_blks, q_blks, b_blks, c_rows, col0):
    sub, chunk = a_blks[0].shape
    n_sl = sub // SUBLANES
    row = lax.broadcasted_iota(jnp.int32, (SUBLANES, chunk), 0)
    lane = lax.broadcasted_iota(jnp.int32, (SUBLANES, chunk), 1)
    sl = [slice(r * SUBLANES, (r + 1) * SUBLANES) for r in range(n_sl)]
    parts = [[a[sl[r]] for r in range(n_sl)] for a in a_blks]
    for s in range(sub):
        first = s // SUBLANES
        here = lane == col0 + s
        here_diag = here & (row >= s - first * SUBLANES)
        for h, (qb, bb) in enumerate(zip(q_blks, b_blks)):
            cs = c_rows(h, s)
            for r in range(first, n_sl):
                col = jnp.sum(qb[sl[r]] * jnp.exp2(bb[sl[r]] - cs), axis=-1, keepdims=True)
                parts[h][r] = jnp.where(here_diag if r == first else here, col, parts[h][r])
    return [jnp.concatenate(p, axis=0) if n_sl > 1 else p[0] for p in parts]


def _hgrn_decays(lb, tri, f_logit, bkc_ref):
    f = lb + (1.0 - lb) * jax.nn.sigmoid(f_logit)
    logf = jnp.log2(f)
    k = 1.0 - f
    h1 = logf.astype(BF16)
    r1 = logf - h1.astype(F32)
    h2 = r1.astype(BF16)
    h3 = (r1 - h2.astype(F32)).astype(BF16)
    cs = jnp.dot(tri, jnp.concatenate([h1, h2, h3], axis=1), preferred_element_type=F32)
    b = cs[:, :LANES] + cs[:, LANES:2 * LANES] + cs[:, 2 * LANES:]
    bkc_ref[0] = b
    bkc_ref[1] = k
    bkc_ref[2] = b - jnp.log2(k)


def _hgrn_intra(qs, bkc_ref, a_ref, qdec_ref, kend_ref, dec_ref, bounded_decay):
    c_len, sub = HGRN_CHUNK, HGRN_SUB
    n_heads = len(qs)
    bs = [bkc_ref[h, 0] for h in range(n_heads)]
    ks = [bkc_ref[h, 1] for h in range(n_heads)]
    a_rows = [[] for _ in range(n_heads)]
    causal = (lax.broadcasted_iota(jnp.int32, (sub, c_len), 1)
              - lax.broadcasted_iota(jnp.int32, (sub, c_len), 0))
    for i in range(c_len // sub):
        rows = slice(i * sub, (i + 1) * sub)
        n_cols = (i + 1) * sub if bounded_decay else i * sub
        a_blks = []
        for q, b, k in zip(qs, bs, ks):
            if n_cols == 0:
                a_blks.append(jnp.zeros((sub, c_len), F32))
                continue
            ref = b[i * sub - 1:i * sub] if i else jnp.zeros((1, LANES), F32)
            qt = (q[rows] * jnp.exp2(b[rows] - ref)).astype(BF16)
            kt = k[:n_cols] * jnp.exp2(ref - b[:n_cols])
            if n_cols < c_len:
                kt = jnp.concatenate([kt, jnp.zeros((c_len - n_cols, LANES), F32)], axis=0)
            a_blks.append(_nt_dot(qt, kt.astype(BF16)))
        if bounded_decay:
            scored = [jnp.where(causal <= i * sub, a, 0.0) for a in a_blks]
        else:
            c_rows = lambda h, s, i=i: bkc_ref[h, 2, pl.ds(i * sub + s, SUBLANES, stride=0), :]
            scored = _hgrn_diag_scores(a_blks, [q[rows] for q in qs], [b[rows] for b in bs],
                                       c_rows, i * sub)
        for h in range(n_heads):
            a_rows[h].append(scored[h])
    for h, (q, b, k) in enumerate(zip(qs, bs, ks)):
        b_last = b[c_len - 1:c_len]
        a_ref[h] = jnp.concatenate(a_rows[h], axis=0).astype(BF16)
        qdec_ref[h] = (q * jnp.exp2(b)).astype(BF16)
        kend_ref[h] = (k * jnp.exp2(b_last - b)).astype(BF16)
        dec_ref[h] = jnp.exp2(b_last)


def _hgrn_state(gw, v_bf, g, st, a_ref, qdec_ref, kend_ref, dec_ref):
    o = (jnp.dot(a_ref[...], v_bf, preferred_element_type=F32)
         + _nt_dot(qdec_ref[...], st.astype(BF16)))
    y = o * _rms_scale(o) * gw
    out = (y * (g * jax.nn.sigmoid(g))).astype(BF16)
    st_new = st * dec_ref[...] + lax.dot_general(v_bf, kend_ref[...], (((0,), (0,)), ((), ())),
                                                 preferred_element_type=F32)
    return out, st_new


def _mix_kernel(lq1_ref, lk1_ref, lq2_ref, lk2_ref, sw_ref, q1_ref, q2_ref, k1_ref, k2_ref,
                vt_ref, lbl_ref, gw_ref, hq_ref, hf_ref, hi_ref, hg_ref, oda_ref, ohg_ref,
                s_ref, mx_ref, st_ref, bkc_ref, a_ref, qdec_ref, kend_ref, dec_ref):
    step = pl.program_id(1)

    c_len = HGRN_CHUNK
    n_heads, t_len = hq_ref.shape[1], hq_ref.shape[2]
    n_chunks = t_len // c_len
    chunks_per_step = ohg_ref.shape[1] // c_len
    assert chunks_per_step % 2 == 0
    l = lbl_ref[...]
    e = jnp.exp(l - jnp.max(l, axis=0, keepdims=True))
    lb_all = jnp.sum(e[:LAYER + 1], axis=0, keepdims=True) / jnp.sum(e, axis=0, keepdims=True)
    gw = gw_ref[...]
    tri = (lax.broadcasted_iota(jnp.int32, (c_len, c_len), 0)
           >= lax.broadcasted_iota(jnp.int32, (c_len, c_len), 1)).astype(BF16)

    def rows_of(c):
        c = jnp.minimum(c, n_chunks - 1)
        return pl.ds(pl.multiple_of(c * c_len, c_len), c_len)

    def stage0(c, par):
        rows = rows_of(c)
        for h in range(n_heads):
            _hgrn_decays(lb_all[:, h * LANES:(h + 1) * LANES], tri, hf_ref[0, h, rows, :],
                         bkc_ref.at[par, h])

    def stage1(c, par, bounded_decay):
        rows = rows_of(c)
        _hgrn_intra([hq_ref[0, h, rows, :].astype(F32) for h in range(n_heads)], bkc_ref.at[par],
                    a_ref.at[par], qdec_ref.at[par], kend_ref.at[par], dec_ref.at[par],
                    bounded_decay)

    def stage2(c, par):
        rows = rows_of(c)
        out_rows = pl.ds(pl.multiple_of((c - step * chunks_per_step) * c_len, c_len), c_len)
        for h in range(n_heads):
            out, st_new = _hgrn_state(gw, hi_ref[0, h, rows, :], hg_ref[0, h, rows, :].astype(F32),
                                      st_ref[h], a_ref.at[par, h], qdec_ref.at[par, h],
                                      kend_ref.at[par, h], dec_ref.at[par, h])
            ohg_ref[0, out_rows, h * LANES:(h + 1) * LANES] = out
            st_ref[h] = st_new

    def hgrn_stages(bounded_decay):
        calls = []
        for r in range(chunks_per_step // 2):
            c = step * chunks_per_step + 2 * r
            calls += [lambda c=c: stage0(c + 2, 0), lambda c=c: stage2(c, 0),
                      lambda c=c: stage1(c + 1, 1, bounded_decay),
                      lambda c=c: stage0(c + 3, 1), lambda c=c: stage2(c + 1, 1),
                      lambda c=c: stage1(c + 2, 0, bounded_decay)]
        return calls

    t = ATT_T
    nq = q1_ref.shape[2] // t
    lam = (jnp.exp(jnp.sum(lq1_ref[...] * lk1_ref[...], axis=-1, keepdims=True))
           - jnp.exp(jnp.sum(lq2_ref[...] * lk2_ref[...], axis=-1, keepdims=True))
           + LAMBDA_INIT)
    q_refs = (q1_ref, q2_ref)
    k_refs = (k1_ref, k2_ref)
    ones = jnp.ones((BF16_ROWS, t), BF16)
    causal = (lax.broadcasted_iota(jnp.int32, (t, t), 0)
              <= lax.broadcasted_iota(jnp.int32, (t, t), 1))
    tiles = [(qi, j) for qi in range(nq) for j in range(qi + 1)]
    n_buf = ATT_LOOKAHEAD + 1

    def key_max(s):
        quarter = t // 4
        parts = [s[i * quarter:(i + 1) * quarter] for i in range(4)]
        part = jnp.maximum(jnp.maximum(parts[0], parts[1]), jnp.maximum(parts[2], parts[3]))
        return jnp.max(part, axis=0, keepdims=True)

    def produce(n):
        qi, j = tiles[n]
        for mp in range(2):
            s = _nt_dot(k_refs[mp][0, 0, j * t:(j + 1) * t, :],
                        q_refs[mp][0, 0, qi * t:(qi + 1) * t, :])
            if j == qi:
                s = jnp.where(causal, s, NEG_BIG)
            s_ref[n % n_buf, mp] = s
            mx_ref[n % n_buf, mp] = key_max(s)

    def finalize(qi, stats):
        (_, acc1), (_, acc2) = stats
        o = (acc1[:LANES] * (1.0 / acc1[LANES:LANES + 1])
             - acc2[:LANES] * (lam / acc2[LANES:LANES + 1]))
        y = o * lax.rsqrt(jnp.mean(o * o, axis=0, keepdims=True) + EPS) * sw_ref[...]
        oda_ref[0, qi * t:(qi + 1) * t, :] = (y * (1.0 - LAMBDA_INIT)).T.astype(oda_ref.dtype)

    def run(bounded_decay):
        @pl.when(step == 0)
        def _():
            st_ref[...] = jnp.zeros_like(st_ref)
            stage0(0, 0)
            stage1(0, 0, bounded_decay)
            stage0(1, 1)

        stages = hgrn_stages(bounded_decay)
        for n in range(min(ATT_LOOKAHEAD, len(tiles))):
            produce(n)
        stats = None
        issued = 0
        for n, (qi, j) in enumerate(tiles):
            if n + ATT_LOOKAHEAD < len(tiles):
                produce(n + ATT_LOOKAHEAD)
            vta = jnp.concatenate([vt_ref[0, 0, j], ones], axis=0)
            new_stats = []
            for mp in range(2):
                s = s_ref[n % n_buf, mp]
                mx = mx_ref[n % n_buf, mp]
                if j == 0:
                    m_new = mx
                    acc = jnp.dot(vta, jnp.exp2(s - m_new).astype(BF16),
                                  preferred_element_type=F32)
                else:
                    m_old, acc_old = stats[mp]
                    m_new = jnp.maximum(m_old, mx)
                    acc = (jnp.exp2(m_old - m_new) * acc_old
                           + jnp.dot(vta, jnp.exp2(s - m_new).astype(BF16),
                                     preferred_element_type=F32))
                new_stats.append((m_new, acc))
            stats = new_stats
            if j == qi:
                finalize(qi, stats)
            due = (n + 1) * len(stages) // len(tiles)
            while issued < due:
                stages[issued]()
                issued += 1

    bounded = HGRN_SUB * jnp.max(-jnp.log2(lb_all)) <= HGRN_MAX_SUB_DECAY_LOG2
    pl.when(bounded)(lambda: run(True))
    pl.when(jnp.logical_not(bounded))(lambda: run(False))


def _mix(oa, of, ovt, lb_logits, gnorm_w, lq1, lk1, lq2, lk2, subln_w):
    bsz, _, t, _ = oa.shape
    tq = ATT_T
    assert DA_HEADS == HG_HEADS
    width = HG_HEADS * LANES
    vec = lambda n: pl.BlockSpec((1, n), lambda b, h: (0, 0))
    head_slab = lambda slot: pl.BlockSpec((1, 1, t, LANES), lambda b, h: (b, slot + h, 0, 0))
    group_slab = lambda slot: pl.BlockSpec((1, HG_HEADS, t, LANES),
                                           lambda b, h: (b, slot // HG_HEADS, 0, 0))
    return pl.pallas_call(
        _mix_kernel,
        grid=(bsz, DA_HEADS),
        in_specs=[vec(DA_HEAD_DIM), vec(DA_HEAD_DIM), vec(DA_HEAD_DIM), vec(DA_HEAD_DIM),
                  pl.BlockSpec((2 * DA_HEAD_DIM, 1), lambda b, h: (0, 0)),
                  head_slab(SLOT_AQ1), head_slab(SLOT_AQ2), head_slab(SLOT_AK1),
                  head_slab(SLOT_AK2),
                  pl.BlockSpec((1, 1, t // tq, LANES, tq), lambda b, h: (b, h, 0, 0, 0)),
                  pl.BlockSpec((lb_logits.shape[0], width), lambda b, h: (0, 0)),
                  pl.BlockSpec((1, LANES), lambda b, h: (0, 0)),
                  group_slab(SLOT_HQ),
                  pl.BlockSpec((1, HG_HEADS, t, LANES), lambda b, h: (b, 0, 0, 0)),
                  group_slab(SLOT_HI), group_slab(SLOT_HG)],
        out_specs=[pl.BlockSpec((1, t, LANES), lambda b, h: (b, 0, h)),
                   pl.BlockSpec((1, t // DA_HEADS, width), lambda b, h: (b, h, 0))],
        out_shape=[jax.ShapeDtypeStruct((bsz, t, DA_HEADS * LANES), BF16),
                   jax.ShapeDtypeStruct((bsz, t, width), BF16)],
        scratch_shapes=[pltpu.VMEM((ATT_LOOKAHEAD + 1, 2, tq, tq), F32),
                        pltpu.VMEM((ATT_LOOKAHEAD + 1, 2, 1, tq), F32),
                        pltpu.VMEM((HG_HEADS, HG_DIM, HG_DIM), F32),
                        pltpu.VMEM((2, HG_HEADS, 3, HGRN_CHUNK, LANES), F32),
                        pltpu.VMEM((2, HG_HEADS, HGRN_CHUNK, HGRN_CHUNK), BF16),
                        pltpu.VMEM((2, HG_HEADS, HGRN_CHUNK, LANES), BF16),
                        pltpu.VMEM((2, HG_HEADS, HGRN_CHUNK, LANES), BF16),
                        pltpu.VMEM((2, HG_HEADS, 1, LANES), F32)],
        compiler_params=pltpu.CompilerParams(
            dimension_semantics=("arbitrary", "arbitrary"), vmem_limit_bytes=VMEM_LIMIT_BYTES),
        name="mix",
    )(lq1, lk1, lq2, lk2, subln_w.reshape(2 * DA_HEAD_DIM, 1), oa, oa, oa, oa, ovt,
      lb_logits, gnorm_w.reshape(1, LANES), oa, of, oa, oa)


def _ffn_kernel(x_ref, mod_ref, ohg_ref, oda_ref, wo_ref, n2_ref, wg_ref, wu_ref, wd_ref, fw_ref,
                o_ref):
    m = mod_ref[0]
    part = x_ref.shape[1] // FFN_PARTS
    rows = [slice(r * part, (r + 1) * part) for r in range(FFN_PARTS)]

    def residual(r):
        mix = jnp.dot(jnp.concatenate([ohg_ref[0, rows[r]], oda_ref[0, rows[r]]], axis=1),
                      wo_ref[...], preferred_element_type=F32)
        return x_ref[0, rows[r]] + m[2:3] * mix

    def modulated(h):
        return (h * _rms_scale(h) * n2_ref[...] * (1.0 + m[4:5]) + m[3:4]).astype(BF16)

    def swiglu(u):
        acc = jnp.zeros((part, wd_ref.shape[1]), F32)
        for c in range(wg_ref.shape[1] // FFN_FC):
            sl = slice(c * FFN_FC, (c + 1) * FFN_FC)
            gate = jnp.dot(u, wg_ref[:, sl], preferred_element_type=F32)
            up = jnp.dot(u, wu_ref[:, sl], preferred_element_type=F32)
            act = (gate * jax.nn.sigmoid(gate) * up).astype(BF16)
            acc = acc + jnp.dot(act, wd_ref[sl, :], preferred_element_type=F32)
        return acc

    hs = [residual(r) for r in range(FFN_PARTS)]
    us = [modulated(h) for h in hs]
    for r in range(FFN_PARTS):
        h2 = hs[r] + m[5:6] * swiglu(us[r])
        o_ref[0, rows[r]] = h2 * _rms_scale(h2) * fw_ref[...]


def _ffn(x, mod3, o_hg, o_da, w_out_bf, norm2_w, wg_bf, wu_bf, wd_bf, final_w):
    bsz, t, d = x.shape
    tm = FFN_TM
    dff = wg_bf.shape[1]
    assert dff % FFN_FC == 0
    tile = lambda w: pl.BlockSpec((1, tm, w), lambda b, i: (b, i, 0))
    const = lambda r, c: pl.BlockSpec((r, c), lambda b, i: (0, 0), pipeline_mode=pl.Buffered(1))
    return pl.pallas_call(
        _ffn_kernel,
        grid=(bsz, t // tm),
        in_specs=[tile(d),
                  pl.BlockSpec((1, N_MOD, d), lambda b, i: (b, 0, 0)),
                  tile(o_hg.shape[2]), tile(o_da.shape[2]),
                  const(d, d), const(1, d), const(d, dff), const(d, dff), const(dff, d),
                  const(1, d)],
        out_specs=tile(d),
        out_shape=jax.ShapeDtypeStruct((bsz, t, d), F32),
        compiler_params=pltpu.CompilerParams(
            dimension_semantics=("arbitrary", "arbitrary"), vmem_limit_bytes=VMEM_LIMIT_BYTES),
        name="ffn",
    )(x, mod3, o_hg, o_da, w_out_bf, norm2_w.reshape(1, d), wg_bf, wu_bf, wd_bf,
      final_w.reshape(1, d))


def kernel(x, c, w_ada, b_ada, norm1_w, w_in, hgrn_lb_logits, hgrn_gnorm_w, diff_lambda_q1,
           diff_lambda_k1, diff_lambda_q2, diff_lambda_k2, diff_subln_w, w_out, norm2_w,
           w_ffn_gate, w_ffn_up, w_ffn_down, final_norm_w):
    bsz, t, d = x.shape
    assert w_ada.shape[0] == 1, "single-layer trunk"
    l = LAYER
    mod, (w_in_bf, w_out_bf) = _ada(c, w_ada[l], b_ada[l], [w_in[l], w_out[l]])
    mod3 = mod.reshape(bsz, N_MOD, d)
    oa, of, ovt, (wg_bf, wu_bf, wd_bf) = _inproj(
        x, mod3, norm1_w[l], w_in_bf, [w_ffn_gate[l], w_ffn_up[l], w_ffn_down[l]])
    o_da, o_hg = _mix(oa, of, ovt, hgrn_lb_logits, hgrn_gnorm_w[l], diff_lambda_q1[l:l + 1],
                      diff_lambda_k1[l:l + 1], diff_lambda_q2[l:l + 1], diff_lambda_k2[l:l + 1],
                      diff_subln_w[l])
    return _ffn(x, mod3, o_hg, o_da, w_out_bf, norm2_w[l], wg_bf, wu_bf, wd_bf, final_norm_w)
```
